```python
import jax
import jax.numpy as jnp
from jax import lax
import numpy as np

D_MODEL = 4096
BATCH = 4
SEQ = 4096
DEPTH = 2

HEAD_DIM = 128
ROPE_DIM = HEAD_DIM // 4
ROPE_THETA = 500000.0
NORM_EPS = 1e-6
NEG_INF = -1e30

NSA_HEADS = D_MODEL // (2 * HEAD_DIM)
NSA_KV_HEADS = 4
NSA_GROUP = NSA_HEADS // NSA_KV_HEADS
CMP_BLOCK = 32
CMP_STRIDE = 16
SEL_BLOCK = 64
SEL_TOPN = 16
WINDOW = 512
NSA_QCHUNK = 32
FORCED_SCORE = 1e4

RWKV_WIDTH = D_MODEL // 2
RWKV_HEAD = 64
RWKV_HEADS = RWKV_WIDTH // RWKV_HEAD
LORA_DECAY = max(32, int(round(1.8 * RWKV_WIDTH ** 0.5 / 32)) * 32)
LORA_AAA = max(32, int(round(1.8 * RWKV_WIDTH ** 0.5 / 32)) * 32)
LORA_GATE = max(32, int(round(0.6 * RWKV_WIDTH ** 0.8 / 32)) * 32)
GN_EPS = 64e-5

DSA_HEADS = D_MODEL // HEAD_DIM
DSA_KV_HEADS = 4
DSA_GROUP = DSA_HEADS // DSA_KV_HEADS
IDX_HEADS = 32
IDX_DIM = 128
DSA_TOPK_MAX = 256
DSA_QCHUNK = 128

MEM_TOKENS = 256
MEM_HEADS = 4
MEM_WIDTH = MEM_HEADS * HEAD_DIM

D_FF = -(-(8 * D_MODEL) // (3 * 256)) * 256

NSA_Q = NSA_HEADS * HEAD_DIM
NSA_KV = NSA_KV_HEADS * HEAD_DIM
NSA_SPLITS = (NSA_Q, NSA_KV, NSA_KV, NSA_KV, NSA_KV, NSA_KV, NSA_KV, 3 * NSA_HEADS)
NSA_COLS = sum(NSA_SPLITS)
RWKV_SPLITS = (RWKV_WIDTH, RWKV_WIDTH, RWKV_WIDTH, LORA_DECAY, LORA_AAA, LORA_GATE)
RWKV_COLS = sum(RWKV_SPLITS)
EVEN_IN = NSA_COLS + RWKV_COLS
EVEN_OUT = NSA_Q + RWKV_WIDTH
DSA_SPLITS = (DSA_HEADS * HEAD_DIM, DSA_KV_HEADS * HEAD_DIM, DSA_KV_HEADS * HEAD_DIM,
              IDX_HEADS * IDX_DIM, IDX_DIM, IDX_HEADS)
ODD_IN = sum(DSA_SPLITS)
ODD_OUT = DSA_HEADS * HEAD_DIM

kernel_name = "hybrid_nsa_rwkv7_dsa_memory_trunk"


def rms_norm(x, g, eps=NORM_EPS):
    xf = x.astype(jnp.float32)
    y = xf * lax.rsqrt(jnp.mean(xf * xf, axis=-1, keepdims=True) + eps)
    return (y * g.astype(jnp.float32)).astype(x.dtype)


def split_cols(y, sizes):
    bounds = [int(b) for b in np.cumsum(sizes)[:-1]]
    return jnp.split(y, bounds, axis=-1)


def rope_tables(pos):
    inv = ROPE_THETA ** (-jnp.arange(0, ROPE_DIM, 2, dtype=jnp.float32) / ROPE_DIM)
    ang = pos.astype(jnp.float32)[..., None] * inv
    return jnp.cos(ang)[:, :, None, :], jnp.sin(ang)[:, :, None, :]


def partial_rope(x, cos, sin):
    half = ROPE_DIM // 2
    x1 = x[..., :half].astype(jnp.float32)
    x2 = x[..., half:ROPE_DIM].astype(jnp.float32)
    rot = jnp.concatenate([x1 * cos - x2 * sin, x2 * cos + x1 * sin], axis=-1)
    return jnp.concatenate([rot.astype(x.dtype), x[..., ROPE_DIM:]], axis=-1)


def compress_blocks(kv, pe, w1, w2):
    B, T, H, D = kv.shape
    ch = kv.reshape(B, T // CMP_STRIDE, CMP_STRIDE, H, D)
    blocks = jnp.concatenate([ch[:, :-1], ch[:, 1:]], axis=2) + pe[None, None, :, None, :]
    n_cmp = blocks.shape[1]
    flat = blocks.transpose(0, 1, 3, 2, 4).reshape(B, n_cmp, H, CMP_BLOCK * D)
    return jax.nn.gelu(flat @ w1) @ w2


def cmp_to_sel_matrix(n_cmp, n_sel):
    cs = CMP_STRIDE * np.arange(n_cmp)[:, None]
    ss = SEL_BLOCK * np.arange(n_sel)[None, :]
    ov = np.clip(np.minimum(cs + CMP_BLOCK, ss + SEL_BLOCK) - np.maximum(cs, ss), 0, None)
    return jnp.asarray(ov / CMP_BLOCK, dtype=jnp.float32)


def nsa_attention(pa, cos, sin, cmp_cos, cmp_sin, q_norm, kc_norm, ks_norm, kw_norm,
                  pe_k, pe_v, ck_w1, ck_w2, cv_w1, cv_w2):
    B, T, _ = pa.shape
    f32 = jnp.float32
    scale = HEAD_DIM ** -0.5
    q, kc, vc, ks, vs, kw, vw, gates = split_cols(pa, NSA_SPLITS)
    kv_shape = (B, T, NSA_KV_HEADS, HEAD_DIM)
    q = partial_rope(rms_norm(q.reshape(B, T, NSA_HEADS, HEAD_DIM), q_norm), cos, sin)
    qg = q.reshape(B, T, NSA_KV_HEADS, NSA_GROUP, HEAD_DIM)
    ks = partial_rope(rms_norm(ks.reshape(kv_shape), ks_norm), cos, sin)
    kw = partial_rope(rms_norm(kw.reshape(kv_shape), kw_norm), cos, sin)
    vs = vs.reshape(kv_shape)
    vw = vw.reshape(kv_shape)
    t = jnp.arange(T)

    kc = compress_blocks(kc.reshape(kv_shape), pe_k, ck_w1, ck_w2)
    kc = partial_rope(rms_norm(kc, kc_norm), cmp_cos, cmp_sin)
    vc = compress_blocks(vc.reshape(kv_shape), pe_v, cv_w1, cv_w2)
    n_cmp = kc.shape[1]
    cmp_end = CMP_STRIDE * jnp.arange(n_cmp) + CMP_BLOCK - 1
    cmp_ok = cmp_end[None, :] <= t[:, None]
    s_c = jnp.einsum('btkgd,bckd->bkgtc', qg, kc).astype(f32) * scale
    p_c = jax.nn.softmax(jnp.where(cmp_ok, s_c, NEG_INF), axis=-1)
    p_c = p_c * (t >= CMP_BLOCK - 1)[:, None]
    o_c = jnp.einsum('bkgtc,bckd->btkgd', p_c.astype(vc.dtype), vc)

    n_sel = T // SEL_BLOCK
    imp = jnp.einsum('bkgtc,cj->btkj', p_c, cmp_to_sel_matrix(n_cmp, n_sel))
    jt = (t // SEL_BLOCK)[:, None]
    jj = jnp.arange(n_sel)[None, :]
    forced = (jj == 0) | (jj == jt) | (jj == jt - 1)
    imp = jnp.where(forced[None, :, None, :], FORCED_SCORE, imp)
    imp = jnp.where((jj <= jt)[None, :, None, :], imp, -jnp.inf)
    _, sel_idx = lax.top_k(imp, min(SEL_TOPN, n_sel))

    ks_blk = ks.reshape(B, n_sel, SEL_BLOCK, NSA_KV_HEADS, HEAD_DIM).transpose(0, 3, 1, 2, 4)
    vs_blk = vs.reshape(B, n_sel, SEL_BLOCK, NSA_KV_HEADS, HEAD_DIM).transpose(0, 3, 1, 2, 4)
    pad = ((0, 0), (WINDOW, 0), (0, 0), (0, 0))
    kw_pad = jnp.pad(kw, pad)
    vw_pad = jnp.pad(vw, pad)
    b_ix = jnp.arange(B)[:, None, None, None]
    h_ix = jnp.arange(NSA_KV_HEADS)[None, None, :, None]
    blk_off = jnp.arange(SEL_BLOCK)
    win_off = jnp.arange(WINDOW + NSA_QCHUNK)

    def chunk(c):
        t0 = c * NSA_QCHUNK
        tq = t0 + jnp.arange(NSA_QCHUNK)
        q_c = lax.dynamic_slice_in_dim(qg, t0, NSA_QCHUNK, axis=1)
        idx = lax.dynamic_slice_in_dim(sel_idx, t0, NSA_QCHUNK, axis=1)
        k_sel = ks_blk[b_ix, h_ix, idx]
        v_sel = vs_blk[b_ix, h_ix, idx]
        kpos = idx[..., None] * SEL_BLOCK + blk_off
        ok = (kpos <= tq[None, :, None, None, None])[:, :, :, None]
        s = jnp.einsum('bqkgd,bqknsd->bqkgns', q_c, k_sel).astype(f32) * scale
        p = jax.nn.softmax(jnp.where(ok, s, NEG_INF), axis=(-2, -1))
        o_s = jnp.einsum('bqkgns,bqknsd->bqkgd', p.astype(v_sel.dtype), v_sel)
        k_w = lax.dynamic_slice_in_dim(kw_pad, t0, WINDOW + NSA_QCHUNK, axis=1)
        v_w = lax.dynamic_slice_in_dim(vw_pad, t0, WINDOW + NSA_QCHUNK, axis=1)
        spos = t0 - WINDOW + win_off
        diff = tq[:, None] - spos[None, :]
        okw = (spos[None, :] >= 0) & (diff >= 0) & (diff < WINDOW)
        s = jnp.einsum('bqkgd,bskd->bqkgs', q_c, k_w).astype(f32) * scale
        p = jax.nn.softmax(jnp.where(okw[None, :, None, None, :], s, NEG_INF), axis=-1)
        o_w = jnp.einsum('bqkgs,bskd->bqkgd', p.astype(v_w.dtype), v_w)
        return o_s, o_w

    o_s, o_w = lax.map(chunk, jnp.arange(T // NSA_QCHUNK))
    heads = (B, T, NSA_HEADS, HEAD_DIM)
    o_s = jnp.moveaxis(o_s, 0, 1).reshape(heads)
    o_w = jnp.moveaxis(o_w, 0, 1).reshape(heads)
    o_c = o_c.reshape(heads)
    g = jax.nn.sigmoid(gates.reshape(B, T, NSA_HEADS, 3).astype(f32)).astype(pa.dtype)
    o = g[..., 0:1] * o_c + g[..., 1:2] * o_s + g[..., 2:3] * o_w
    return o.reshape(B, T, NSA_Q)


def rwkv7_time_mix(pb, mu, w0, w2, a0, a2, g2, kk_gain, k_a, r_k, gn_g, gn_b):
    B, T, _ = pb.shape
    f32 = jnp.float32
    prev = jnp.pad(pb, ((0, 0), (1, 0), (0, 0)))[:, :T]
    xs = pb + (prev - pb) * mu
    r, k, v, wd, ad, gd = split_cols(xs, RWKV_SPLITS)
    w_log = -jax.nn.softplus(-(w0 + jnp.tanh(wd) @ w2).astype(f32)) - 0.5
    a = jax.nn.sigmoid((a0 + ad @ a2).astype(f32))
    g = jax.nn.sigmoid(gd) @ g2
    heads = lambda z: z.reshape(B, T, RWKV_HEADS, RWKV_HEAD).astype(f32)
    kk = heads(k * kk_gain)
    kk = kk / jnp.maximum(jnp.linalg.norm(kk, axis=-1, keepdims=True), 1e-12)
    k = k * (1.0 + (a - 1.0) * k_a)
    r_h, k_h, v_h, a_h = heads(r), heads(k), heads(v), heads(a)
    decay = jnp.exp(-jnp.exp(heads(w_log)))

    def step(S, inp):
        r_t, w_t, k_t, v_t, kk_t, b_t = inp
        sa = jnp.einsum('bhij,bhj->bhi', S, -kk_t)
        S = (S * w_t[:, :, None, :] + sa[..., None] * b_t[:, :, None, :]
             + v_t[..., None] * k_t[:, :, None, :])
        return S, jnp.einsum('bhij,bhj->bhi', S, r_t)

    tm = lambda z: jnp.moveaxis(z, 1, 0)
    S0 = jnp.zeros((B, RWKV_HEADS, RWKV_HEAD, RWKV_HEAD), f32)
    _, y = lax.scan(step, S0, (tm(r_h), tm(decay), tm(k_h), tm(v_h), tm(kk), tm(kk * a_h)))
    y = jnp.moveaxis(y, 0, 1)
    mean = jnp.mean(y, axis=-1, keepdims=True)
    var = jnp.mean(jnp.square(y - mean), axis=-1, keepdims=True)
    y = ((y - mean) * lax.rsqrt(var + GN_EPS)).reshape(B, T, RWKV_WIDTH) * gn_g + gn_b
    bonus = jnp.sum(r_h * k_h * r_k, axis=-1, keepdims=True) * v_h
    y = y + bonus.reshape(B, T, RWKV_WIDTH)
    return (y * g).astype(pb.dtype)


def nsa_rwkv_mixer(h, cos, sin, cmp_cos, cmp_sin, w_in, w_out, nsa_w, rwkv_w):
    proj = h @ w_in
    o_a = nsa_attention(proj[..., :NSA_COLS], cos, sin, cmp_cos, cmp_sin, *nsa_w)
    o_b = rwkv7_time_mix(proj[..., NSA_COLS:], *rwkv_w)
    return jnp.concatenate([o_a, o_b], axis=-1) @ w_out


def dsa_attention(po, cos, sin, q_norm, k_norm, ki_norm):
    B, T, _ = po.shape
    f32 = jnp.float32
    scale = HEAD_DIM ** -0.5
    q, k, v, qi, ki, wi = split_cols(po, DSA_SPLITS)
    q = partial_rope(rms_norm(q.reshape(B, T, DSA_HEADS, HEAD_DIM), q_norm), cos, sin)
    qg = q.reshape(B, T, DSA_KV_HEADS, DSA_GROUP, HEAD_DIM)
    k = partial_rope(rms_norm(k.reshape(B, T, DSA_KV_HEADS, HEAD_DIM), k_norm), cos, sin)
    v = v.reshape(B, T, DSA_KV_HEADS, HEAD_DIM)
    qi = partial_rope(qi.reshape(B, T, IDX_HEADS, IDX_DIM), cos, sin)
    ki = partial_rope(rms_norm(ki[:, :, None, :], ki_norm), cos, sin)[:, :, 0]
    wi = wi.astype(f32) * IDX_HEADS ** -0.5
    topk = min(DSA_TOPK_MAX, T // 4)
    b_ix = jnp.arange(B)[:, None, None]
    key_pos = jnp.arange(T)

    def chunk(c):
        t0 = c * DSA_QCHUNK
        tq = t0 + jnp.arange(DSA_QCHUNK)
        qi_c = lax.dynamic_slice_in_dim(qi, t0, DSA_QCHUNK, axis=1)
        wi_c = lax.dynamic_slice_in_dim(wi, t0, DSA_QCHUNK, axis=1)
        logits = jnp.einsum('bqhd,bsd->bqhs', qi_c, ki).astype(f32) * IDX_DIM ** -0.5
        score = jnp.einsum('bqh,bqhs->bqs', wi_c, jax.nn.relu(logits))
        score = jnp.where(key_pos[None, None, :] <= tq[None, :, None], score, -jnp.inf)
        _, sel = lax.top_k(score, topk)
        k_sel = k[b_ix, sel]
        v_sel = v[b_ix, sel]
        q_c = lax.dynamic_slice_in_dim(qg, t0, DSA_QCHUNK, axis=1)
        s = jnp.einsum('bqkgd,bqnkd->bqkgn', q_c, k_sel).astype(f32) * scale
        ok = (sel <= tq[None, :, None])[:, :, None, None, :]
        p = jax.nn.softmax(jnp.where(ok, s, NEG_INF), axis=-1)
        return jnp.einsum('bqkgn,bqnkd->bqkgd', p.astype(v_sel.dtype), v_sel)

    o = lax.map(chunk, jnp.arange(T // DSA_QCHUNK))
    return jnp.moveaxis(o, 0, 1).reshape(B, T, ODD_OUT)


def dsa_mixer(h, cos, sin, w_in, w_out, q_norm, k_norm, ki_norm):
    return dsa_attention(h @ w_in, cos, sin, q_norm, k_norm, ki_norm) @ w_out


def memory_kv(mem, mem_norm, w_kv, k_norm):
    B, M, _ = mem.shape
    mk, mv = split_cols(rms_norm(mem, mem_norm) @ w_kv, (MEM_WIDTH, MEM_WIDTH))
    mk = rms_norm(mk.reshape(B, M, MEM_HEADS, HEAD_DIM), k_norm)
    return mk, mv.reshape(B, M, MEM_HEADS, HEAD_DIM)


def memory_xattn(h, mem_k, mem_v, wq, q_norm, wo):
    B, T, _ = h.shape
    q = rms_norm((h @ wq).reshape(B, T, MEM_HEADS, HEAD_DIM), q_norm)
    s = jnp.einsum('bthd,bmhd->bhtm', q, mem_k).astype(jnp.float32) * HEAD_DIM ** -0.5
    p = jax.nn.softmax(s, axis=-1).astype(mem_v.dtype)
    o = jnp.einsum('bhtm,bmhd->bthd', p, mem_v).reshape(B, T, MEM_WIDTH)
    return o @ wo


def swiglu(h, w1, w3, w2):
    return (jax.nn.silu(h @ w1) * (h @ w3)) @ w2


def setup_inputs(seed: int = 0) -> dict:
    key = jax.random.key(seed)
    keys = iter(jax.random.split(key, 64))
    f32 = jnp.float32

    def dense(shape):
        return jax.random.normal(next(keys), shape, f32) * shape[0] ** -0.5

    def gain(shape):
        return 1.0 + 0.02 * jax.random.normal(next(keys), shape, f32)

    def small(shape, s):
        return s * jax.random.normal(next(keys), shape, f32)

    x = jax.random.normal(next(keys), (BATCH, SEQ, D_MODEL), f32)
    mem = jax.random.normal(next(keys), (BATCH, MEM_TOKENS, D_MODEL), f32)
    steps = jax.random.randint(next(keys), (BATCH, SEQ), 1, 3, dtype=jnp.int32)
    positions = jnp.cumsum(steps, axis=1, dtype=jnp.int32) - 1
    return {
        "x": x,
        "mem": mem,
        "positions": positions,
        "mem_norm": gain((D_MODEL,)),
        "mem_w_kv": dense((D_MODEL, 2 * MEM_WIDTH)),
        "mem_k_norm": gain((HEAD_DIM,)),
        "l0_mix_norm": gain((D_MODEL,)),
        "l0_w_in": dense((D_MODEL, EVEN_IN)),
        "l0_w_out": dense((EVEN_OUT, D_MODEL)),
        "nsa_q_norm": gain((HEAD_DIM,)),
        "nsa_kc_norm": gain((HEAD_DIM,)),
        "nsa_ks_norm": gain((HEAD_DIM,)),
        "nsa_kw_norm": gain((HEAD_DIM,)),
        "nsa_pe_k": small((CMP_BLOCK, HEAD_DIM), 0.02),
        "nsa_pe_v": small((CMP_BLOCK, HEAD_DIM), 0.02),
        "nsa_ck_w1": dense((CMP_BLOCK * HEAD_DIM, HEAD_DIM)),
        "nsa_ck_w2": dense((HEAD_DIM, HEAD_DIM)),
        "nsa_cv_w1": dense((CMP_BLOCK * HEAD_DIM, HEAD_DIM)),
        "nsa_cv_w2": dense((HEAD_DIM, HEAD_DIM)),
        "rwkv_mu": jax.random.uniform(next(keys), (RWKV_COLS,), f32),
        "rwkv_w0": jax.random.uniform(next(keys), (RWKV_WIDTH,), f32, -3.0, 1.0),
        "rwkv_w2": dense((LORA_DECAY, RWKV_WIDTH)),
        "rwkv_a0": small((RWKV_WIDTH,), 0.5),
        "rwkv_a2": dense((LORA_AAA, RWKV_WIDTH)),
        "rwkv_g2": dense((LORA_GATE, RWKV_WIDTH)),
        "rwkv_kk": gain((RWKV_WIDTH,)),
        "rwkv_ka": gain((RWKV_WIDTH,)),
        "rwkv_rk": small((RWKV_HEADS, RWKV_HEAD), 0.1),
        "rwkv_gn_g": gain((RWKV_WIDTH,)),
        "rwkv_gn_b": small((RWKV_WIDTH,), 0.02),
        "l0_xattn_norm": gain((D_MODEL,)),
        "l0_mem_wq": dense((D_MODEL, MEM_WIDTH)),
        "l0_mem_q_norm": gain((HEAD_DIM,)),
        "l0_mem_wo": dense((MEM_WIDTH, D_MODEL)),
        "l0_ffn_norm": gain((D_MODEL,)),
        "l0_w1": dense((D_MODEL, D_FF)),
        "l0_w3": dense((D_MODEL, D_FF)),
        "l0_w2": dense((D_FF, D_MODEL)),
        "l1_mix_norm": gain((D_MODEL,)),
        "l1_w_in": dense((D_MODEL, ODD_IN)),
        "l1_w_out": dense((ODD_OUT, D_MODEL)),
        "dsa_q_norm": gain((HEAD_DIM,)),
        "dsa_k_norm": gain((HEAD_DIM,)),
        "dsa_ki_norm": gain((IDX_DIM,)),
        "l1_xattn_norm": gain((D_MODEL,)),
        "l1_mem_wq": dense((D_MODEL, MEM_WIDTH)),
        "l1_mem_q_norm": gain((HEAD_DIM,)),
        "l1_mem_wo": dense((MEM_WIDTH, D_MODEL)),
        "l1_ffn_norm": gain((D_MODEL,)),
        "l1_w1": dense((D_MODEL, D_FF)),
        "l1_w3": dense((D_MODEL, D_FF)),
        "l1_w2": dense((D_FF, D_MODEL)),
    }


def reference(x, mem, positions, mem_norm, mem_w_kv, mem_k_norm,
              l0_mix_norm, l0_w_in, l0_w_out,
              nsa_q_norm, nsa_kc_norm, nsa_ks_norm, nsa_kw_norm, nsa_pe_k, nsa_pe_v,
              nsa_ck_w1, nsa_ck_w2, nsa_cv_w1, nsa_cv_w2,
              rwkv_mu, rwkv_w0, rwkv_w2, rwkv_a0, rwkv_a2, rwkv_g2, rwkv_kk, rwkv_ka,
              rwkv_rk, rwkv_gn_g, rwkv_gn_b,
              l0_xattn_norm, l0_mem_wq, l0_mem_q_norm, l0_mem_wo,
              l0_ffn_norm, l0_w1, l0_w3, l0_w2,
              l1_mix_norm, l1_w_in, l1_w_out, dsa_q_norm, dsa_k_norm, dsa_ki_norm,
              l1_xattn_norm, l1_mem_wq, l1_mem_q_norm, l1_mem_wo,
              l1_ffn_norm, l1_w1, l1_w3, l1_w2):
    T = x.shape[1]
    cos, sin = rope_tables(positions)
    n_cmp = T // CMP_STRIDE - 1
    cmp_pos = positions[:, CMP_STRIDE * np.arange(n_cmp) + CMP_BLOCK - 1]
    cmp_cos, cmp_sin = rope_tables(cmp_pos)
    mem_k, mem_v = memory_kv(mem, mem_norm, mem_w_kv, mem_k_norm)

    nsa_w = (nsa_q_norm, nsa_kc_norm, nsa_ks_norm, nsa_kw_norm, nsa_pe_k, nsa_pe_v,
             nsa_ck_w1, nsa_ck_w2, nsa_cv_w1, nsa_cv_w2)
    rwkv_w = (rwkv_mu, rwkv_w0, rwkv_w2, rwkv_a0, rwkv_a2, rwkv_g2, rwkv_kk, rwkv_ka,
              rwkv_rk, rwkv_gn_g, rwkv_gn_b)
    xattn_w = ((l0_xattn_norm, l0_mem_wq, l0_mem_q_norm, l0_mem_wo),
               (l1_xattn_norm, l1_mem_wq, l1_mem_q_norm, l1_mem_wo))
    ffn_w = ((l0_ffn_norm, l0_w1, l0_w3, l0_w2),
             (l1_ffn_norm, l1_w1, l1_w3, l1_w2))

    for layer in range(DEPTH):
        if layer % 2 == 0:
            x = x + nsa_rwkv_mixer(rms_norm(x, l0_mix_norm), cos, sin, cmp_cos, cmp_sin,
                                   l0_w_in, l0_w_out, nsa_w, rwkv_w)
        else:
            x = x + dsa_mixer(rms_norm(x, l1_mix_norm), cos, sin, l1_w_in, l1_w_out,
                              dsa_q_norm, dsa_k_norm, dsa_ki_norm)
        xn, wq, qn, wo = xattn_w[layer]
        x = x + memory_xattn(rms_norm(x, xn), mem_k, mem_v, wq, qn, wo)
        fn, w1, w3, w2 = ffn_w[layer]
        x = x + swiglu(rms_norm(x, fn), w1, w3, w2)
    return x
```

```python
import functools

import numpy as np
import jax
import jax.numpy as jnp
from jax import lax
from jax.experimental import pallas as pl
from jax.experimental.pallas import tpu as pltpu

F32 = jnp.float32
BF16 = jnp.bfloat16

HEAD_DIM = 128
ROPE_DIM = HEAD_DIM // 4
ROPE_HALF = ROPE_DIM // 2
ROPE_THETA = 500000.0
NORM_EPS = 1e-6
NEG_INF = -1e30

NSA_HEADS = 16
NSA_KV_HEADS = 4
NSA_GROUP = NSA_HEADS // NSA_KV_HEADS
CMP_BLOCK = 32
CMP_STRIDE = 16
SEL_BLOCK = 64
SEL_TOPN = 16
WINDOW = 512
FORCED_SCORE = 1e4

RWKV_WIDTH = 2048
RWKV_HEAD = 64
RWKV_HEADS = RWKV_WIDTH // RWKV_HEAD
LORA_DECAY = 96
LORA_AAA = 96
LORA_GATE = 256
GN_EPS = 64e-5

DSA_HEADS = 32
DSA_KV_HEADS = 4
DSA_GROUP = DSA_HEADS // DSA_KV_HEADS
IDX_HEADS = 32
IDX_DIM = 128
DSA_TOPK_MAX = 256

MEM_HEADS = 4
MEM_WIDTH = MEM_HEADS * HEAD_DIM

LANES = 128
VMEM_BIG = 56 * 1024 * 1024
VMEM_MID = 40 * 1024 * 1024
INT_MIN = -2 ** 31


def _params(sem, vmem=VMEM_MID):
    return pltpu.CompilerParams(dimension_semantics=sem, vmem_limit_bytes=vmem)


def _sigmoid(x):
    return 1.0 / (1.0 + jnp.exp(-x))


def _dot(a, b, preferred_element_type=F32, precision=None):
    return lax.dot_general(a, b, (((1,), (0,)), ((), ())), precision=precision,
                           preferred_element_type=preferred_element_type)


def _nt_dot(a, b):
    return lax.dot_general(a, b, (((1,), (1,)), ((), ())), preferred_element_type=F32)


def _split_dot(x, w_bf16):
    hi = x.astype(BF16)
    lo = (x - hi.astype(F32)).astype(BF16)
    return (_dot(hi, w_bf16, preferred_element_type=F32)
            + _dot(lo, w_bf16, preferred_element_type=F32))


def _rmsnorm_kernel(x_ref, g_ref, o_ref):
    x = x_ref[...]
    ms = jnp.mean(x * x, axis=-1, keepdims=True)
    o_ref[...] = (x * lax.rsqrt(ms + NORM_EPS) * g_ref[...]).astype(o_ref.dtype)


def rmsnorm(x2d, g, tm=256):
    m, d = x2d.shape
    return pl.pallas_call(
        _rmsnorm_kernel,
        grid=(m // tm,),
        in_specs=[pl.BlockSpec((tm, d), lambda i: (i, 0)),
                  pl.BlockSpec((1, d), lambda i: (0, 0))],
        out_specs=pl.BlockSpec((tm, d), lambda i: (i, 0)),
        out_shape=jax.ShapeDtypeStruct((m, d), BF16),
        compiler_params=_params(("parallel",)),
        name="rmsnorm",
    )(x2d, g.reshape(1, d))


def _mm_kernel(*refs, nk, nb, has_res):
    a_ref = refs[0]
    b_refs = refs[1:1 + nb]
    pos = 1 + nb
    res_ref = refs[pos] if has_res else None
    pos += int(has_res)
    o_ref = refs[pos]
    acc_refs = refs[pos + 1:]
    parts = [_dot(a_ref[...], b[...], preferred_element_type=F32) for b in b_refs]

    def finish(vals):
        if nb == 2:
            y = vals[0] * _sigmoid(vals[0]) * vals[1]
        else:
            y = vals[0]
        if has_res:
            y = y + res_ref[...]
        o_ref[...] = y.astype(o_ref.dtype)

    if nk == 1:
        finish(parts)
    else:
        k = pl.program_id(2)

        @pl.when(k == 0)
        def _():
            for acc, p in zip(acc_refs, parts):
                acc[...] = p

        @pl.when(k > 0)
        def _():
            for acc, p in zip(acc_refs, parts):
                acc[...] += p

        @pl.when(k == nk - 1)
        def _():
            finish([acc[...] for acc in acc_refs])


def matmul(a, bs, *, tm, tn, tk=None, res=None, out_dtype=F32, name="matmul"):
    if not isinstance(bs, (list, tuple)):
        bs = [bs]
    m, kdim = a.shape
    n = bs[0].shape[1]
    tk = kdim if tk is None else tk
    tm = min(tm, m)
    tn = min(tn, n)
    nk = kdim // tk
    assert m % tm == 0 and n % tn == 0 and kdim % tk == 0
    nb = len(bs)
    in_specs = [pl.BlockSpec((tm, tk), lambda i, j, k: (i, k))]
    in_specs += [pl.BlockSpec((tk, tn), lambda i, j, k: (k, j)) for _ in bs]
    args = [a, *bs]
    if res is not None:
        in_specs.append(pl.BlockSpec((tm, tn), lambda i, j, k: (i, j)))
        args.append(res)
    scratch = [pltpu.VMEM((tm, tn), F32) for _ in bs] if nk > 1 else []
    return pl.pallas_call(
        functools.partial(_mm_kernel, nk=nk, nb=nb, has_res=res is not None),
        grid=(m // tm, n // tn, nk),
        in_specs=in_specs,
        out_specs=pl.BlockSpec((tm, tn), lambda i, j, k: (i, j)),
        out_shape=jax.ShapeDtypeStruct((m, n), out_dtype),
        scratch_shapes=scratch,
        compiler_params=_params(("parallel", "parallel", "arbitrary"), VMEM_BIG),
        name=name,
    )(*args)


def _rope_tab_kernel(pos_ref, inv_ref, c_ref, sa_ref, sb_ref):
    ang = pos_ref[...].astype(F32) * inv_ref[...]
    c = jnp.cos(ang)
    s = jnp.sin(ang)
    lane = lax.broadcasted_iota(jnp.int32, ang.shape, 1)
    c_ref[...] = jnp.where(lane < ROPE_DIM, c, 1.0)
    sa_ref[...] = jnp.where(lane < ROPE_HALF, -s, 0.0)
    sb_ref[...] = jnp.where((lane >= ROPE_HALF) & (lane < ROPE_DIM), s, 0.0)


def rope_tables(pos_flat, tm=256):
    n = pos_flat.shape[0]
    tm = min(tm, n)
    inv = ROPE_THETA ** (-jnp.arange(0, ROPE_DIM, 2, dtype=F32) / ROPE_DIM)
    inv_row = jnp.concatenate([inv, inv, jnp.zeros((LANES - ROPE_DIM,), F32)]).reshape(1, LANES)
    pos_b = jnp.broadcast_to(pos_flat[:, None], (n, LANES))
    spec = pl.BlockSpec((tm, LANES), lambda i: (i, 0))
    shp = jax.ShapeDtypeStruct((n, LANES), F32)
    return pl.pallas_call(
        _rope_tab_kernel,
        grid=(n // tm,),
        in_specs=[spec, pl.BlockSpec((1, LANES), lambda i: (0, 0))],
        out_specs=[spec, spec, spec],
        out_shape=[shp, shp, shp],
        compiler_params=_params(("parallel",)),
        name="rope_tables",
    )(pos_b, inv_row)


def _head_norm(xh, g):
    ms = jnp.mean(xh * xh, axis=-1, keepdims=True)
    return xh * lax.rsqrt(ms + NORM_EPS) * g


def _rope(xh, c, sa, sb):
    return (xh * c + pltpu.roll(xh, LANES - ROPE_HALF, 1) * sa
            + pltpu.roll(xh, ROPE_HALF, 1) * sb)


def _norm_rope_kernel(*refs, heads, do_norm, do_rope, scale):
    x_ref = refs[0]
    pos = 1
    g = None
    if do_norm:
        g = refs[pos][...]
        pos += 1
    if do_rope:
        c, sa, sb = refs[pos][...], refs[pos + 1][...], refs[pos + 2][...]
        pos += 3
    o_ref = refs[pos]
    for h in range(heads):
        xh = x_ref[:, h * HEAD_DIM:(h + 1) * HEAD_DIM]
        if do_norm:
            xh = _head_norm(xh, g)
        if do_rope:
            xh = _rope(xh, c, sa, sb)
        if scale != 1.0:
            xh = xh * scale
        o_ref[:, h * HEAD_DIM:(h + 1) * HEAD_DIM] = xh.astype(o_ref.dtype)


def norm_rope(x2d, col_block, heads, gain, tabs, scale=1.0, tm=256, name="norm_rope"):
    m = x2d.shape[0]
    w = heads * HEAD_DIM
    tm = min(tm, m)
    in_specs = [pl.BlockSpec((tm, w), lambda i: (i, col_block))]
    args = [x2d]
    if gain is not None:
        in_specs.append(pl.BlockSpec((1, HEAD_DIM), lambda i: (0, 0)))
        args.append(gain.reshape(1, HEAD_DIM))
    if tabs is not None:
        in_specs += [pl.BlockSpec((tm, LANES), lambda i: (i, 0))] * 3
        args += list(tabs)
    return pl.pallas_call(
        functools.partial(_norm_rope_kernel, heads=heads, do_norm=gain is not None,
                          do_rope=tabs is not None, scale=scale),
        grid=(m // tm,),
        in_specs=in_specs,
        out_specs=pl.BlockSpec((tm, w), lambda i: (i, 0)),
        out_shape=jax.ShapeDtypeStruct((m, w), BF16),
        compiler_params=_params(("parallel",)),
        name=name,
    )(*args)


def _compress_kernel(*refs, is_key):
    if is_key:
        x_ref, w1_ref, w2_ref, pe_ref, g_ref, c_ref, sa_ref, sb_ref, o_ref = refs
    else:
        x_ref, w1_ref, w2_ref, pe_ref, o_ref = refs
    half = CMP_STRIDE * HEAD_DIM
    x = x_ref[0, 0]
    nc = x.shape[0]
    top = _dot(x, w1_ref[:half, :], preferred_element_type=F32)
    bot = _dot(x, w1_ref[half:, :], preferred_element_type=F32)
    pe_term = _dot(pe_ref[...], w1_ref[...], preferred_element_type=F32)[0:1, :]
    pre = top + pltpu.roll(bot, nc - 1, 0) + pe_term
    hid = jax.nn.gelu(pre)
    out = _dot(hid.astype(BF16), w2_ref[...], preferred_element_type=F32)
    if is_key:
        out = _rope(_head_norm(out, g_ref[...]), c_ref[0], sa_ref[0], sb_ref[0])
    o_ref[0, 0] = out.astype(o_ref.dtype)


def compress(xblk, w1, w2, pe, key_extras=None, name="compress"):
    b, h, nc, wdt = xblk.shape
    pe8 = jnp.broadcast_to(pe.reshape(1, CMP_BLOCK * HEAD_DIM), (8, CMP_BLOCK * HEAD_DIM)).astype(BF16)
    in_specs = [pl.BlockSpec((1, 1, nc, wdt), lambda i, j: (i, j, 0, 0)),
                pl.BlockSpec((CMP_BLOCK * HEAD_DIM, HEAD_DIM), lambda i, j: (0, 0)),
                pl.BlockSpec((HEAD_DIM, HEAD_DIM), lambda i, j: (0, 0)),
                pl.BlockSpec((8, CMP_BLOCK * HEAD_DIM), lambda i, j: (0, 0))]
    args = [xblk, w1.astype(BF16), w2.astype(BF16), pe8]
    if key_extras is not None:
        gain, tabs = key_extras
        in_specs.append(pl.BlockSpec((1, HEAD_DIM), lambda i, j: (0, 0)))
        in_specs += [pl.BlockSpec((1, nc, LANES), lambda i, j: (i, 0, 0))] * 3
        args += [gain.reshape(1, HEAD_DIM)] + [t.reshape(b, nc, LANES) for t in tabs]
    out_dtype = BF16
    return pl.pallas_call(
        functools.partial(_compress_kernel, is_key=key_extras is not None),
        grid=(b, h),
        in_specs=in_specs,
        out_specs=pl.BlockSpec((1, 1, nc, HEAD_DIM), lambda i, j: (i, j, 0, 0)),
        out_shape=jax.ShapeDtypeStruct((b, h, nc, HEAD_DIM), out_dtype),
        compiler_params=_params(("parallel", "parallel")),
        name=name,
    )(*args)


def _stack_heads(q_ref, group, tq):
    return jnp.concatenate([q_ref[0, :, g * HEAD_DIM:(g + 1) * HEAD_DIM] for g in range(group)], axis=0)


def _unstack_heads(o_ref, o_t, group, tq):
    o = o_t.T
    for g in range(group):
        o_ref[0, :, g * HEAD_DIM:(g + 1) * HEAD_DIM] = o[g * tq:(g + 1) * tq, :].astype(o_ref.dtype)


def _cmp_kernel(q_ref, kc_ref, vct_ref, mt_ref, o_ref, sel_ref, *, tq, n_sel, topn):
    group = NSA_GROUP
    t0 = pl.program_id(2) * tq
    q = _stack_heads(q_ref, group, tq)
    kc = kc_ref[0, 0]
    nc = kc.shape[0]
    s = _nt_dot(kc, q)
    c_idx = lax.broadcasted_iota(jnp.int32, (nc, tq), 0)
    t_idx = t0 + lax.broadcasted_iota(jnp.int32, (nc, tq), 1)
    ok1 = (CMP_STRIDE * c_idx + CMP_BLOCK - 1) <= t_idx
    bias1 = jnp.where(ok1, 0.0, NEG_INF)
    okf1 = jnp.where(ok1, 1.0, 0.0)
    bias = jnp.concatenate([bias1] * group, axis=1)
    okf = jnp.concatenate([okf1] * group, axis=1)
    s = s + bias
    m = jnp.max(s, axis=0, keepdims=True)
    e = jnp.exp(s - m) * okf
    l = jnp.sum(e, axis=0, keepdims=True)
    p = e * jnp.where(l > 0.0, 1.0 / l, 0.0)
    o_t = _dot(vct_ref[0, 0], p.astype(BF16), preferred_element_type=F32)
    _unstack_heads(o_ref, o_t, group, tq)

    psum = p[:, 0:tq]
    for g in range(1, group):
        psum = psum + p[:, g * tq:(g + 1) * tq]
    imp = _dot(mt_ref[...], psum, preferred_element_type=F32,
                  precision=lax.Precision.HIGHEST)
    j_idx = lax.broadcasted_iota(jnp.int32, (n_sel, tq), 0)
    jt = (t0 + lax.broadcasted_iota(jnp.int32, (n_sel, tq), 1)) // SEL_BLOCK
    forced = (j_idx == 0) | (j_idx == jt) | (j_idx == jt - 1)
    imp = jnp.where(forced, FORCED_SCORE, imp)
    imp = jnp.where(j_idx <= jt, imp, -jnp.inf)
    rank = jnp.zeros((n_sel, tq), F32)
    for jp in range(n_sel):
        row = imp[jp:jp + 1, :]
        before = (row > imp) | ((row == imp) & (j_idx > jp))
        rank = rank + jnp.where(before, 1.0, 0.0)
    sel_ref[0, 0] = jnp.where(rank < topn, 1.0, 0.0).astype(sel_ref.dtype)


def nsa_compressed(qn, kc, vct, b, t, tq=256):
    nc = kc.shape[2]
    n_sel = t // SEL_BLOCK
    topn = min(SEL_TOPN, n_sel)
    tq = min(tq, t)
    cs = CMP_STRIDE * np.arange(nc)[:, None]
    ss = SEL_BLOCK * np.arange(n_sel)[None, :]
    ov = np.clip(np.minimum(cs + CMP_BLOCK, ss + SEL_BLOCK) - np.maximum(cs, ss), 0, None) / CMP_BLOCK
    ov[nc - 1, :] = 0.0
    mt = jnp.asarray(ov.T, dtype=F32)
    gw = NSA_GROUP * HEAD_DIM
    return pl.pallas_call(
        functools.partial(_cmp_kernel, tq=tq, n_sel=n_sel, topn=topn),
        grid=(b, NSA_KV_HEADS, t // tq),
        in_specs=[pl.BlockSpec((1, tq, gw), lambda i, h, j: (i, j, h)),
                  pl.BlockSpec((1, 1, nc, HEAD_DIM), lambda i, h, j: (i, h, 0, 0)),
                  pl.BlockSpec((1, 1, HEAD_DIM, nc), lambda i, h, j: (i, h, 0, 0)),
                  pl.BlockSpec((n_sel, nc), lambda i, h, j: (0, 0))],
        out_specs=[pl.BlockSpec((1, tq, gw), lambda i, h, j: (i, j, h)),
                   pl.BlockSpec((1, 1, n_sel, tq), lambda i, h, j: (i, h, 0, j))],
        out_shape=[jax.ShapeDtypeStruct((b, t, NSA_HEADS * HEAD_DIM), F32),
                   jax.ShapeDtypeStruct((b, NSA_KV_HEADS, n_sel, t), BF16)],
        compiler_params=_params(("parallel", "parallel", "parallel")),
        name="nsa_compressed",
    )(qn, kc, vct, mt)


def _flash_kernel(*refs, mode, group, tq, tk, nkv):
    if mode == "sel":
        q_ref, k_ref, vt_ref, sel_ref, et_ref, o_ref, q_s, m_s, l_s, acc_s = refs
    elif mode == "dsa":
        q_ref, k_ref, vt_ref, mask_ref, o_ref, q_s, m_s, l_s, acc_s = refs
    else:
        q_ref, k_ref, vt_ref, o_ref, q_s, m_s, l_s, acc_s = refs
    qi = pl.program_id(2)
    kj = pl.program_id(3)
    t0 = qi * tq
    if mode == "window":
        kidx = (t0 - WINDOW) // tk + kj
        active = kidx >= 0
        k0 = kidx * tk
    else:
        k0 = kj * tk
        active = k0 <= t0 + tq - 1

    @pl.when(kj == 0)
    def _():
        q_s[...] = _stack_heads(q_ref, group, tq)
        m_s[...] = jnp.full(m_s.shape, NEG_INF, F32)
        l_s[...] = jnp.zeros(l_s.shape, F32)
        acc_s[...] = jnp.zeros(acc_s.shape, F32)

    @pl.when(active)
    def _():
        s = _nt_dot(k_ref[0], q_s[...])
        kpos = k0 + lax.broadcasted_iota(jnp.int32, (tk, tq), 0)
        tpos = t0 + lax.broadcasted_iota(jnp.int32, (tk, tq), 1)
        ok = kpos <= tpos
        if mode == "window":
            ok = ok & (tpos - kpos < WINDOW)
        elif mode == "sel":
            member = _dot(et_ref[...], sel_ref[0, 0], preferred_element_type=F32)
            ok = ok & (member > 0.5)
        else:
            ok = ok & (mask_ref[0].astype(F32) > 0.5)
        bias1 = jnp.where(ok, 0.0, NEG_INF)
        s = s + jnp.concatenate([bias1] * group, axis=1)
        m_old = m_s[...]
        m_new = jnp.maximum(m_old, jnp.max(s, axis=0, keepdims=True))
        alpha = jnp.exp(m_old - m_new)
        p = jnp.exp(s - m_new)
        l_s[...] = alpha * l_s[...] + jnp.sum(p, axis=0, keepdims=True)
        acc_s[...] = acc_s[...] * alpha + _dot(vt_ref[0, 0], p.astype(BF16),
                                                   preferred_element_type=F32)
        m_s[...] = m_new

    @pl.when(kj == nkv - 1)
    def _():
        _unstack_heads(o_ref, acc_s[...] * (1.0 / l_s[...]), group, tq)


def flash_masked(qn, kn, vt, *, mode, group, b, t, tq, tk, extra=None, name="flash"):
    kvh = kn.shape[2] // HEAD_DIM
    gw = group * HEAD_DIM
    tq = min(tq, t)
    tk = min(tk, t)
    if mode == "window":
        nkv = (WINDOW + tq) // tk if t > tq else 1

        def kmap(qi, kj):
            return jnp.maximum((qi * tq - WINDOW) // tk + kj, 0)
    else:
        nkv = t // tk

        def kmap(qi, kj):
            return jnp.minimum(kj, (qi * tq + tq - 1) // tk)
    in_specs = [pl.BlockSpec((1, tq, gw), lambda i, h, qi, kj: (i, qi, h)),
                pl.BlockSpec((1, tk, HEAD_DIM), lambda i, h, qi, kj: (i, kmap(qi, kj), h)),
                pl.BlockSpec((1, 1, HEAD_DIM, tk), lambda i, h, qi, kj: (i, h, 0, kmap(qi, kj)))]
    args = [qn, kn, vt]
    if mode == "sel":
        n_sel = t // SEL_BLOCK
        et = jnp.asarray(np.arange(t)[:, None] // SEL_BLOCK == np.arange(n_sel)[None, :], dtype=BF16)
        in_specs += [pl.BlockSpec((1, 1, n_sel, tq), lambda i, h, qi, kj: (i, h, 0, qi)),
                     pl.BlockSpec((tk, n_sel), lambda i, h, qi, kj: (kmap(qi, kj), 0))]
        args += [extra, et]
    elif mode == "dsa":
        in_specs.append(pl.BlockSpec((1, tk, tq), lambda i, h, qi, kj: (i, kmap(qi, kj), qi)))
        args.append(extra)
    return pl.pallas_call(
        functools.partial(_flash_kernel, mode=mode, group=group, tq=tq, tk=tk, nkv=nkv),
        grid=(b, kvh, t // tq, nkv),
        in_specs=in_specs,
        out_specs=pl.BlockSpec((1, tq, gw), lambda i, h, qi, kj: (i, qi, h)),
        out_shape=jax.ShapeDtypeStruct((b, t, kvh * gw), F32),
        scratch_shapes=[pltpu.VMEM((group * tq, HEAD_DIM), BF16),
                        pltpu.VMEM((1, group * tq), F32),
                        pltpu.VMEM((1, group * tq), F32),
                        pltpu.VMEM((HEAD_DIM, group * tq), F32)],
        compiler_params=_params(("parallel", "parallel", "parallel", "arbitrary")),
        name=name,
    )(*args)


def _nsa_merge_kernel(oc_ref, os_ref, ow_ref, g_ref, e_ref, o_ref):
    gate = _sigmoid(g_ref[...])
    out = None
    for j, src in enumerate((oc_ref, os_ref, ow_ref)):
        gj = _split_dot(gate, e_ref[j])
        term = gj * src[...]
        out = term if out is None else out + term
    o_ref[...] = out.astype(o_ref.dtype)


def nsa_merge(oc, os_, ow, pc, tm=256):
    m, w = oc.shape
    tm = min(tm, m)
    e = np.zeros((3, LANES, w), np.float32)
    for h in range(NSA_HEADS):
        for j in range(3):
            e[j, h * 3 + j, h * HEAD_DIM:(h + 1) * HEAD_DIM] = 1.0
    spec = pl.BlockSpec((tm, w), lambda i: (i, 0))
    return pl.pallas_call(
        _nsa_merge_kernel,
        grid=(m // tm,),
        in_specs=[spec, spec, spec,
                  pl.BlockSpec((tm, LANES), lambda i: (i, 0)),
                  pl.BlockSpec((3, LANES, w), lambda i: (0, 0, 0))],
        out_specs=spec,
        out_shape=jax.ShapeDtypeStruct((m, w), BF16),
        compiler_params=_params(("parallel",)),
        name="nsa_merge",
    )(oc, os_, ow, pc, jnp.asarray(e, dtype=BF16))


def _seg_sum(x, bd):
    outs = []
    for s in range(x.shape[1] // LANES):
        outs.append(_split_dot(x[:, s * LANES:(s + 1) * LANES], bd))
    return jnp.concatenate(outs, axis=1)


def _shifted(x, prev_row, first):
    rolled = pltpu.roll(x, 1, 0)
    row0 = jnp.where(first, 0.0, prev_row)
    ridx = lax.broadcasted_iota(jnp.int32, x.shape, 0)
    return jnp.where(ridx == 0, row0, rolled)


def _rwkv_prep_kernel(pb_ref, pbp_ref, pc_ref, pcp_ref, mub_ref, muc_ref, w0_ref, w2_ref, a0_ref, a2_ref,
                      g2_ref, kkg_ref, ka_ref, bd_ref,
                      r_ref, w_ref, k_ref, v_ref, kk_ref, b_ref, g_ref, *, tiles_per_seq):
    first = (pl.program_id(0) % tiles_per_seq) == 0
    w = RWKV_WIDTH
    xb = pb_ref[...]
    xb = xb + (_shifted(xb, pbp_ref[7:8, :], first) - xb) * mub_ref[...]
    xc = pc_ref[...]
    xc = xc + (_shifted(xc, pcp_ref[7:8, :], first) - xc) * muc_ref[...]
    r, k, v = xb[:, :w], xb[:, w:2 * w], xb[:, 2 * w:]
    wd, ad, gd = xc[:, LANES:2 * LANES], xc[:, 2 * LANES:3 * LANES], xc[:, 3 * LANES:]
    z = w0_ref[...] + _dot(jnp.tanh(wd).astype(BF16), w2_ref[...], preferred_element_type=F32)
    nz = -z
    softplus = jnp.maximum(nz, 0.0) + jnp.log(1.0 + jnp.exp(-jnp.abs(nz)))
    w_log = -softplus - 0.5
    a = _sigmoid(a0_ref[...] + _dot(ad.astype(BF16), a2_ref[...], preferred_element_type=F32))
    g = _dot(_sigmoid(gd).astype(BF16), g2_ref[...], preferred_element_type=F32)
    kkv = k * kkg_ref[...]
    norm = jnp.sqrt(_seg_sum(kkv * kkv, bd_ref[...]))
    kkv = kkv / jnp.maximum(norm, 1e-12)
    r_ref[...] = r
    w_ref[...] = jnp.exp(-jnp.exp(w_log))
    k_ref[...] = k * (1.0 + (a - 1.0) * ka_ref[...])
    v_ref[...] = v
    kk_ref[...] = kkv
    b_ref[...] = kkv * a
    g_ref[...] = g


def _pad_rows(wm, rows):
    return jnp.pad(wm, ((0, rows - wm.shape[0]), (0, 0)))


def _head_block_ones():
    bd = np.zeros((LANES, LANES), np.float32)
    bd[:RWKV_HEAD, :RWKV_HEAD] = 1.0
    bd[RWKV_HEAD:, RWKV_HEAD:] = 1.0
    return jnp.asarray(bd, dtype=BF16)


def rwkv_prep(pb, pc, t, mu, w0, w2, a0, a2, g2, kk_gain, k_a, tm=128):
    m = pb.shape[0]
    w = RWKV_WIDTH
    tm = min(tm, t)
    mu_r, mu_k, mu_v, mu_wd, mu_ad, mu_gd = jnp.split(
        mu, [int(x) for x in np.cumsum([w, w, w, LORA_DECAY, LORA_AAA])])
    mub = jnp.concatenate([mu_r, mu_k, mu_v]).reshape(1, 3 * w)
    pad = lambda z: jnp.pad(z, (0, LANES - z.shape[0]))
    muc = jnp.concatenate([jnp.zeros((LANES,), F32), pad(mu_wd), pad(mu_ad), mu_gd]).reshape(1, -1)
    cw = pc.shape[1]
    row = lambda z: z.reshape(1, w)
    full = lambda shape: pl.BlockSpec(shape, lambda i: (0,) * len(shape))
    tile = lambda width: pl.BlockSpec((tm, width), lambda i: (i, 0))
    prev = lambda width: pl.BlockSpec((8, width), lambda i: (jnp.maximum(i * (tm // 8) - 1, 0), 0))
    out_spec = tile(w)
    shp = jax.ShapeDtypeStruct((m, w), F32)
    return pl.pallas_call(
        functools.partial(_rwkv_prep_kernel, tiles_per_seq=t // tm),
        grid=(m // tm,),
        in_specs=[tile(3 * w), prev(3 * w), tile(cw), prev(cw), full((1, 3 * w)), full((1, cw)),
                  full((1, w)), full((LANES, w)), full((1, w)), full((LANES, w)), full((LORA_GATE, w)),
                  full((1, w)), full((1, w)), full((LANES, LANES))],
        out_specs=[out_spec] * 7,
        out_shape=[shp] * 7,
        compiler_params=_params(("parallel",), VMEM_BIG),
        name="rwkv_prep",
    )(pb, pb, pc, pc, mub, muc, row(w0), _pad_rows(w2, LANES).astype(BF16), row(a0),
      _pad_rows(a2, LANES).astype(BF16), g2.astype(BF16), row(kk_gain), row(k_a), _head_block_ones())


def _rwkv_scan_kernel(w_ref, kk_ref, b_ref, k_ref, r_ref, v_ref, y_ref, s_ref, *, tb, rows):
    ig = pl.program_id(1)

    @pl.when(pl.program_id(0) == 0)
    def _():
        s_ref[ig] = jnp.zeros(s_ref.shape[1:], F32)

    def step(t, s):
        kk = kk_ref[t]
        sa = -jnp.sum(s * kk[None], axis=1)
        vrow = v_ref[t, 0]
        s = (s * w_ref[t][None] + sa[:, None, :] * b_ref[t][None]
             + vrow[:, None, :] * k_ref[t][None])
        y_ref[t, 0] = jnp.sum(s * r_ref[t][None], axis=1)
        return s

    s_ref[ig] = lax.fori_loop(0, tb, step, s_ref[ig])


def rwkv_scan(w, kk, bb, k, r, v, tb=64, rows=4):
    t, n, lanes = w.shape
    groups = n // rows
    tb = min(tb, t)
    op = pl.BlockSpec((tb, n, lanes), lambda i, j: (i, 0, 0))
    vy = pl.BlockSpec((tb, 1, rows, lanes), lambda i, j: (i, j, 0, 0))
    return pl.pallas_call(
        functools.partial(_rwkv_scan_kernel, tb=tb, rows=rows),
        grid=(t // tb, groups),
        in_specs=[op, op, op, op, op, vy],
        out_specs=vy,
        out_shape=jax.ShapeDtypeStruct((t, groups, rows, lanes), F32),
        scratch_shapes=[pltpu.VMEM((groups, rows, n, lanes), F32)],
        compiler_params=_params(("arbitrary", "arbitrary"), VMEM_BIG),
        name="rwkv_scan",
    )(w, kk, bb, k, r, v)


def _rwkv_post_kernel(y_ref, r_ref, k_ref, v_ref, g_ref, rk_ref, gg_ref, gb_ref, bd_ref, o_ref):
    bd = bd_ref[...]
    y = y_ref[...]
    inv_n = 1.0 / RWKV_HEAD
    mean = _seg_sum(y, bd) * inv_n
    d = y - mean
    var = _seg_sum(d * d, bd) * inv_n
    yn = d * lax.rsqrt(var + GN_EPS) * gg_ref[...] + gb_ref[...]
    bonus = _seg_sum(r_ref[...] * k_ref[...] * rk_ref[...], bd) * v_ref[...]
    o_ref[...] = ((yn + bonus) * g_ref[...]).astype(o_ref.dtype)


def rwkv_post(y, r, k, v, g, r_k, gn_g, gn_b, tm=256):
    m, w = y.shape
    tm = min(tm, m)
    tile = pl.BlockSpec((tm, w), lambda i: (i, 0))
    rowspec = pl.BlockSpec((1, w), lambda i: (0, 0))
    return pl.pallas_call(
        _rwkv_post_kernel,
        grid=(m // tm,),
        in_specs=[tile] * 5 + [rowspec] * 3 + [pl.BlockSpec((LANES, LANES), lambda i: (0, 0))],
        out_specs=tile,
        out_shape=jax.ShapeDtypeStruct((m, w), BF16),
        compiler_params=_params(("parallel",)),
        name="rwkv_post",
    )(y, r, k, v, g, r_k.reshape(1, w), gn_g.reshape(1, w), gn_b.reshape(1, w), _head_block_ones())


def rwkv7(pb, pc, b, t, mu, w0, w2, a0, a2, g2, kk_gain, k_a, r_k, gn_g, gn_b):
    r, w, k, v, kk, bb, g = rwkv_prep(pb, pc, t, mu, w0, w2, a0, a2, g2, kk_gain, k_a)
    rows = 4

    def to_scan(z):
        return z.reshape(b, t, RWKV_HEADS, RWKV_HEAD).transpose(1, 3, 0, 2).reshape(t, RWKV_HEAD, b * RWKV_HEADS)

    v_s = to_scan(v).reshape(t, RWKV_HEAD // rows, rows, b * RWKV_HEADS)
    y = rwkv_scan(to_scan(w), to_scan(kk), to_scan(bb), to_scan(k), to_scan(r), v_s, rows=rows)
    y = y.reshape(t, RWKV_HEAD, b, RWKV_HEADS).transpose(2, 0, 3, 1).reshape(b * t, RWKV_WIDTH)
    return rwkv_post(y, r, k, v, g, r_k, gn_g, gn_b)


def _dsa_select_kernel(qi_ref, ki_ref, wt_ref, mask_ref, q_s, score_s, *, t, tq, kc, topk):
    t0 = pl.program_id(1) * tq
    for h in range(IDX_HEADS):
        q_s[h * tq:(h + 1) * tq, :] = qi_ref[0, :, h * IDX_DIM:(h + 1) * IDX_DIM]
    score_s[...] = jnp.full(score_s.shape, -jnp.inf, F32)
    wt = wt_ref[0]

    def chunk(c, carry):
        k0 = pl.multiple_of(c * kc, kc)
        lg = _nt_dot(ki_ref[0, pl.ds(k0, kc), :], q_s[...])
        acc = jnp.zeros((kc, tq), F32)
        for h in range(IDX_HEADS):
            acc = acc + jnp.maximum(lg[:, h * tq:(h + 1) * tq], 0.0) * wt[h:h + 1, :]
        spos = k0 + lax.broadcasted_iota(jnp.int32, (kc, tq), 0)
        tpos = t0 + lax.broadcasted_iota(jnp.int32, (kc, tq), 1)
        score_s[pl.ds(k0, kc), :] = jnp.where(spos <= tpos, acc + 0.0, -jnp.inf)
        return carry

    lax.fori_loop(0, (t0 + tq - 1) // kc + 1, chunk, 0)

    bits = pltpu.bitcast(score_s[...], jnp.int32)
    key = jnp.where(bits < 0, bits ^ jnp.int32(0x7FFFFFFF), bits)
    kf = jnp.float32(topk)

    def count(pred):
        return jnp.sum(jnp.where(pred, 1.0, 0.0), axis=0, keepdims=True)

    def value_bit(it, tau_u):
        cand_u = tau_u | lax.shift_left(jnp.int32(1), 31 - it)
        cand_s = cand_u ^ jnp.int32(INT_MIN)
        return jnp.where(count(key >= cand_s) >= kf, cand_u, tau_u)

    tau_u = lax.fori_loop(0, 32, value_bit, jnp.zeros((1, tq), jnp.int32))
    tau = tau_u ^ jnp.int32(INT_MIN)
    gt = key > tau
    eq = key == tau
    need = kf - count(gt)
    spos = lax.broadcasted_iota(jnp.int32, (t, tq), 0)
    nbits = int(np.log2(t))

    def index_bit(it, jj):
        cand = jj | lax.shift_left(jnp.int32(1), nbits - 1 - it)
        return jnp.where(count(eq & (spos < cand)) < need, cand, jj)

    jj = lax.fori_loop(0, nbits, index_bit, jnp.zeros((1, tq), jnp.int32))
    keep = gt | (eq & (spos <= jj))
    mask_ref[0] = jnp.where(keep, 1.0, 0.0).astype(mask_ref.dtype)


def dsa_select(qi, ki, wt, b, t, tq=128, kc=512):
    topk = min(DSA_TOPK_MAX, t // 4)
    kc = min(kc, t)
    return pl.pallas_call(
        functools.partial(_dsa_select_kernel, t=t, tq=tq, kc=kc, topk=topk),
        grid=(b, t // tq),
        in_specs=[pl.BlockSpec((1, tq, IDX_HEADS * IDX_DIM), lambda i, j: (i, j, 0)),
                  pl.BlockSpec((1, t, IDX_DIM), lambda i, j: (i, 0, 0)),
                  pl.BlockSpec((1, IDX_HEADS, tq), lambda i, j: (i, 0, j))],
        out_specs=pl.BlockSpec((1, t, tq), lambda i, j: (i, 0, j)),
        out_shape=jax.ShapeDtypeStruct((b, t, t), BF16),
        scratch_shapes=[pltpu.VMEM((IDX_HEADS * tq, IDX_DIM), BF16),
                        pltpu.VMEM((t, tq), F32)],
        compiler_params=_params(("parallel", "parallel"), VMEM_BIG),
        name="dsa_select",
    )(qi, ki, wt)


def _xattn_kernel(q_ref, g_ref, mk_ref, mv_ref, o_ref):
    scale = HEAD_DIM ** -0.5
    for h in range(MEM_HEADS):
        sl = slice(h * HEAD_DIM, (h + 1) * HEAD_DIM)
        qh = (_head_norm(q_ref[:, sl], g_ref[...]) * scale).astype(BF16)
        s = _nt_dot(qh, mk_ref[0, :, sl])
        m = jnp.max(s, axis=-1, keepdims=True)
        e = jnp.exp(s - m)
        p = e / jnp.sum(e, axis=-1, keepdims=True)
        o_ref[:, sl] = _dot(p.astype(BF16), mv_ref[0, :, sl],
                               preferred_element_type=F32).astype(o_ref.dtype)


def mem_xattn(qm, gain, mk, mv, b, t, tq=256):
    tq = min(tq, t)
    nt = t // tq
    mtok = mk.shape[1]
    return pl.pallas_call(
        _xattn_kernel,
        grid=(b, nt),
        in_specs=[pl.BlockSpec((tq, MEM_WIDTH), lambda i, j: (i * nt + j, 0)),
                  pl.BlockSpec((1, HEAD_DIM), lambda i, j: (0, 0)),
                  pl.BlockSpec((1, mtok, MEM_WIDTH), lambda i, j: (i, 0, 0)),
                  pl.BlockSpec((1, mtok, MEM_WIDTH), lambda i, j: (i, 0, 0))],
        out_specs=pl.BlockSpec((tq, MEM_WIDTH), lambda i, j: (i * nt + j, 0)),
        out_shape=jax.ShapeDtypeStruct((b * t, MEM_WIDTH), BF16),
        compiler_params=_params(("parallel", "parallel")),
        name="mem_xattn",
    )(qm, gain.reshape(1, HEAD_DIM), mk, mv)


def _kv_transposed(x2d, lo, b, t, heads):
    v = x2d[:, lo:lo + heads * HEAD_DIM].astype(BF16)
    return v.reshape(b, t, heads, HEAD_DIM).transpose(0, 2, 3, 1)


def nsa_mixer(pa, pc, b, t, tabs, cmp_tabs, q_norm, kc_norm, ks_norm, kw_norm, pe_k, pe_v,
              ck_w1, ck_w2, cv_w1, cv_w2):
    kvw = NSA_KV_HEADS * HEAD_DIM
    scale = HEAD_DIM ** -0.5
    qn = norm_rope(pa, 0, NSA_HEADS, q_norm, tabs, scale=scale, name="nsa_q_prep")
    ksn = norm_rope(pa, 6, NSA_KV_HEADS, ks_norm, tabs, name="nsa_ks_prep").reshape(b, t, kvw)
    kwn = norm_rope(pa, 8, NSA_KV_HEADS, kw_norm, tabs, name="nsa_kw_prep").reshape(b, t, kvw)
    vst = _kv_transposed(pa, 7 * kvw, b, t, NSA_KV_HEADS)
    vwt = _kv_transposed(pa, 9 * kvw, b, t, NSA_KV_HEADS)
    qn = qn.reshape(b, t, NSA_HEADS * HEAD_DIM)

    nc = t // CMP_STRIDE

    def chunked(lo):
        z = pa[:, lo:lo + kvw].astype(BF16).reshape(b, nc, CMP_STRIDE, NSA_KV_HEADS, HEAD_DIM)
        return z.transpose(0, 3, 1, 2, 4).reshape(b, NSA_KV_HEADS, nc, CMP_STRIDE * HEAD_DIM)

    kc = compress(chunked(4 * kvw), ck_w1, ck_w2, pe_k, key_extras=(kc_norm, cmp_tabs), name="compress_k")
    vc = compress(chunked(5 * kvw), cv_w1, cv_w2, pe_v, name="compress_v")
    vct = vc.transpose(0, 1, 3, 2)
    o_c, sel_t = nsa_compressed(qn, kc, vct, b, t)
    o_s = flash_masked(qn, ksn, vst, mode="sel", group=NSA_GROUP, b=b, t=t, tq=256, tk=256,
                       extra=sel_t, name="nsa_selected")
    o_w = flash_masked(qn, kwn, vwt, mode="window", group=NSA_GROUP, b=b, t=t, tq=256, tk=256,
                       name="nsa_window")
    m = b * t
    w = NSA_HEADS * HEAD_DIM
    return nsa_merge(o_c.reshape(m, w), o_s.reshape(m, w), o_w.reshape(m, w), pc)


def dsa_mixer_core(p1, p2, b, t, tabs, q_norm, k_norm, ki_norm):
    scale = HEAD_DIM ** -0.5
    qw = DSA_HEADS * HEAD_DIM
    kvw = DSA_KV_HEADS * HEAD_DIM
    qn = norm_rope(p1, 0, DSA_HEADS, q_norm, tabs, scale=scale, name="dsa_q_prep").reshape(b, t, qw)
    kn = norm_rope(p1, qw // kvw, DSA_KV_HEADS, k_norm, tabs, name="dsa_k_prep").reshape(b, t, kvw)
    vt = _kv_transposed(p1, qw + kvw, b, t, DSA_KV_HEADS)
    qi = norm_rope(p1[:, qw + 2 * kvw:], 0, IDX_HEADS, None, tabs, name="dsa_qi_prep")
    qi = qi.reshape(b, t, IDX_HEADS * IDX_DIM)
    ki = norm_rope(p2, 0, 1, ki_norm, tabs, name="dsa_ki_prep").reshape(b, t, IDX_DIM)
    wscale = IDX_HEADS ** -0.5 * IDX_DIM ** -0.5
    wt = (p2[:, IDX_DIM:IDX_DIM + IDX_HEADS] * wscale).reshape(b, t, IDX_HEADS).transpose(0, 2, 1)
    mask_t = dsa_select(qi, ki, wt, b, t)
    o = flash_masked(qn, kn, vt, mode="dsa", group=DSA_GROUP, b=b, t=t, tq=128, tk=512,
                     extra=mask_t, name="dsa_attention")
    return o.reshape(b * t, qw)


def kernel(x, mem, positions, mem_norm, mem_w_kv, mem_k_norm, l0_mix_norm, l0_w_in, l0_w_out, nsa_q_norm, nsa_kc_norm, nsa_ks_norm, nsa_kw_norm, nsa_pe_k, nsa_pe_v, nsa_ck_w1, nsa_ck_w2, nsa_cv_w1, nsa_cv_w2, rwkv_mu, rwkv_w0, rwkv_w2, rwkv_a0, rwkv_a2, rwkv_g2, rwkv_kk, rwkv_ka, rwkv_rk, rwkv_gn_g, rwkv_gn_b, l0_xattn_norm, l0_mem_wq, l0_mem_q_norm, l0_mem_wo, l0_ffn_norm, l0_w1, l0_w3, l0_w2, l1_mix_norm, l1_w_in, l1_w_out, dsa_q_norm, dsa_k_norm, dsa_ki_norm, l1_xattn_norm, l1_mem_wq, l1_mem_q_norm, l1_mem_wo, l1_ffn_norm, l1_w1, l1_w3, l1_w2):
    b, t, d = x.shape
    m = b * t
    bf = lambda z: z.astype(BF16)
    x2 = x.reshape(m, d)

    tabs = rope_tables(positions.reshape(m))
    nc = t // CMP_STRIDE
    cmp_pos = positions[:, CMP_BLOCK - 1::CMP_STRIDE]
    cmp_pos = jnp.concatenate([cmp_pos, cmp_pos[:, -1:]], axis=1)
    cmp_tabs = rope_tables(cmp_pos.reshape(b * nc))

    mtok = mem.shape[1]
    memn = rmsnorm(mem.reshape(b * mtok, d), mem_norm)
    mkv = matmul(memn, bf(mem_w_kv), tm=512, tn=512, name="mem_kv_proj")
    mk = norm_rope(mkv, 0, MEM_HEADS, mem_k_norm, None, name="mem_k_norm").reshape(b, mtok, MEM_WIDTH)
    mv = bf(mkv[:, MEM_WIDTH:]).reshape(b, mtok, MEM_WIDTH)

    def tail(x2, xn, wq, qn, wo, fn, w1, w3, w2):
        h = rmsnorm(x2, xn)
        qm = matmul(h, bf(wq), tm=1024, tn=512, name="xattn_q_proj")
        o = mem_xattn(qm, qn, mk, mv, b, t)
        x2 = matmul(o, bf(wo), tm=1024, tn=512, res=x2, name="xattn_out_proj")
        h = rmsnorm(x2, fn)
        u = matmul(h, [bf(w1), bf(w3)], tm=1024, tn=256, out_dtype=BF16, name="ffn_up")
        return matmul(u, bf(w2), tm=512, tn=512, tk=w2.shape[0] // 2, res=x2, name="ffn_down")

    nsa_cols = NSA_HEADS * HEAD_DIM + 6 * NSA_KV_HEADS * HEAD_DIM
    gate_cols = 3 * NSA_HEADS
    rw0 = nsa_cols + gate_cols
    rw1 = rw0 + 3 * RWKV_WIDTH
    pad_cols = lambda wm: jnp.pad(wm, ((0, 0), (0, LANES - wm.shape[1])))
    w_c = jnp.concatenate([
        pad_cols(l0_w_in[:, nsa_cols:rw0]),
        pad_cols(l0_w_in[:, rw1:rw1 + LORA_DECAY]),
        pad_cols(l0_w_in[:, rw1 + LORA_DECAY:rw1 + LORA_DECAY + LORA_AAA]),
        l0_w_in[:, rw1 + LORA_DECAY + LORA_AAA:]], axis=1)
    h = rmsnorm(x2, l0_mix_norm)
    pa = matmul(h, bf(l0_w_in[:, :nsa_cols]), tm=1024, tn=512, name="l0_proj_nsa")
    pb = matmul(h, bf(l0_w_in[:, rw0:rw1]), tm=1024, tn=512, name="l0_proj_rwkv")
    pc = matmul(h, bf(w_c), tm=1024, tn=w_c.shape[1], name="l0_proj_small")
    o_a = nsa_mixer(pa, pc, b, t, tabs, cmp_tabs, nsa_q_norm, nsa_kc_norm, nsa_ks_norm, nsa_kw_norm,
                    nsa_pe_k, nsa_pe_v, nsa_ck_w1, nsa_ck_w2, nsa_cv_w1, nsa_cv_w2)
    o_b = rwkv7(pb, pc, b, t, rwkv_mu, rwkv_w0, rwkv_w2, rwkv_a0, rwkv_a2, rwkv_g2, rwkv_kk, rwkv_ka,
                rwkv_rk, rwkv_gn_g, rwkv_gn_b)
    x2 = matmul(jnp.concatenate([o_a, o_b], axis=1), bf(l0_w_out), tm=1024, tn=512, res=x2,
                name="l0_out_proj")
    x2 = tail(x2, l0_xattn_norm, l0_mem_wq, l0_mem_q_norm, l0_mem_wo, l0_ffn_norm, l0_w1, l0_w3, l0_w2)

    main_cols = DSA_HEADS * HEAD_DIM + 2 * DSA_KV_HEADS * HEAD_DIM + IDX_HEADS * IDX_DIM
    w_s = jnp.concatenate([l1_w_in[:, main_cols:main_cols + IDX_DIM],
                           pad_cols(l1_w_in[:, main_cols + IDX_DIM:])], axis=1)
    h = rmsnorm(x2, l1_mix_norm)
    p1 = matmul(h, bf(l1_w_in[:, :main_cols]), tm=1024, tn=512, name="l1_proj_main")
    p2 = matmul(h, bf(w_s), tm=1024, tn=w_s.shape[1], name="l1_proj_small")
    o = dsa_mixer_core(p1, p2, b, t, tabs, dsa_q_norm, dsa_k_norm, dsa_ki_norm)
    x2 = matmul(bf(o), bf(l1_w_out), tm=1024, tn=512, res=x2, name="l1_out_proj")
    x2 = tail(x2, l1_xattn_norm, l1_mem_wq, l1_mem_q_norm, l1_mem_wo, l1_ffn_norm, l1_w1, l1_w3, l1_w2)
    return x2.reshape(b, t, d)
```

```python
import functools

import numpy as np
import jax
import jax.numpy as jnp
from jax import lax
from jax.experimental import pallas as pl
from jax.experimental.pallas import tpu as pltpu

F32 = jnp.float32
BF16 = jnp.bfloat16

HEAD_DIM = 128
ROPE_DIM = HEAD_DIM // 4
ROPE_HALF = ROPE_DIM // 2
ROPE_THETA = 500000.0
NORM_EPS = 1e-6
NEG_INF = -1e30

NSA_HEADS = 16
NSA_KV_HEADS = 4
NSA_GROUP = NSA_HEADS // NSA_KV_HEADS
CMP_BLOCK = 32
CMP_STRIDE = 16
SEL_BLOCK = 64
SEL_TOPN = 16
WINDOW = 512
FORCED_SCORE = 1e4

RWKV_WIDTH = 2048
RWKV_HEAD = 64
RWKV_HEADS = RWKV_WIDTH // RWKV_HEAD
LORA_DECAY = 96
LORA_AAA = 96
LORA_GATE = 256
GN_EPS = 64e-5

DSA_HEADS = 32
DSA_KV_HEADS = 4
DSA_GROUP = DSA_HEADS // DSA_KV_HEADS
IDX_HEADS = 32
IDX_DIM = 128
DSA_TOPK_MAX = 256

MEM_HEADS = 4
MEM_WIDTH = MEM_HEADS * HEAD_DIM

LANES = 128
VMEM_BIG = 56 * 1024 * 1024
VMEM_MID = 40 * 1024 * 1024
INT_MIN = -2 ** 31
MASK_BIG = 1e30


def _params(sem, vmem=VMEM_MID):
    return pltpu.CompilerParams(dimension_semantics=sem, vmem_limit_bytes=vmem)


def _sigmoid(x):
    return 1.0 / (1.0 + jnp.exp(-x))


def _dot(a, b, preferred_element_type=F32, precision=None):
    return lax.dot_general(a, b, (((1,), (0,)), ((), ())), precision=precision,
                           preferred_element_type=preferred_element_type)


def _nt_dot(a, b):
    return lax.dot_general(a, b, (((1,), (1,)), ((), ())), preferred_element_type=F32)


def _split_dot(x, w_bf16):
    hi = x.astype(BF16)
    lo = (x - hi.astype(F32)).astype(BF16)
    return (_dot(hi, w_bf16, preferred_element_type=F32)
            + _dot(lo, w_bf16, preferred_element_type=F32))


def _rmsnorm_kernel(x_ref, g_ref, o_ref):
    x = x_ref[...]
    ms = jnp.mean(x * x, axis=-1, keepdims=True)
    o_ref[...] = (x * lax.rsqrt(ms + NORM_EPS) * g_ref[...]).astype(o_ref.dtype)


def rmsnorm(x2d, g, tm=256):
    m, d = x2d.shape
    return pl.pallas_call(
        _rmsnorm_kernel,
        grid=(m // tm,),
        in_specs=[pl.BlockSpec((tm, d), lambda i: (i, 0)),
                  pl.BlockSpec((1, d), lambda i: (0, 0))],
        out_specs=pl.BlockSpec((tm, d), lambda i: (i, 0)),
        out_shape=jax.ShapeDtypeStruct((m, d), BF16),
        compiler_params=_params(("parallel",)),
        name="rmsnorm",
    )(x2d, g.reshape(1, d))


def _mm_kernel(*refs, nk, nb, has_res):
    a_ref = refs[0]
    b_refs = refs[1:1 + nb]
    pos = 1 + nb
    res_ref = refs[pos] if has_res else None
    pos += int(has_res)
    o_ref = refs[pos]
    acc_refs = refs[pos + 1:]
    parts = [_dot(a_ref[...], b[...], preferred_element_type=F32) for b in b_refs]

    def finish(vals):
        if nb == 2:
            y = vals[0] * _sigmoid(vals[0]) * vals[1]
        else:
            y = vals[0]
        if has_res:
            y = y + res_ref[...]
        o_ref[...] = y.astype(o_ref.dtype)

    if nk == 1:
        finish(parts)
    else:
        k = pl.program_id(2)

        @pl.when(k == 0)
        def _():
            for acc, p in zip(acc_refs, parts):
                acc[...] = p

        @pl.when(k > 0)
        def _():
            for acc, p in zip(acc_refs, parts):
                acc[...] += p

        @pl.when(k == nk - 1)
        def _():
            finish([acc[...] for acc in acc_refs])


def matmul(a, bs, *, tm, tn, tk=None, res=None, out_dtype=F32, name="matmul"):
    if not isinstance(bs, (list, tuple)):
        bs = [bs]
    m, kdim = a.shape
    n = bs[0].shape[1]
    tk = kdim if tk is None else tk
    tm = min(tm, m)
    tn = min(tn, n)
    nk = kdim // tk
    assert m % tm == 0 and n % tn == 0 and kdim % tk == 0
    nb = len(bs)
    in_specs = [pl.BlockSpec((tm, tk), lambda i, j, k: (i, k))]
    in_specs += [pl.BlockSpec((tk, tn), lambda i, j, k: (k, j)) for _ in bs]
    args = [a, *bs]
    if res is not None:
        in_specs.append(pl.BlockSpec((tm, tn), lambda i, j, k: (i, j)))
        args.append(res)
    scratch = [pltpu.VMEM((tm, tn), F32) for _ in bs] if nk > 1 else []
    return pl.pallas_call(
        functools.partial(_mm_kernel, nk=nk, nb=nb, has_res=res is not None),
        grid=(m // tm, n // tn, nk),
        in_specs=in_specs,
        out_specs=pl.BlockSpec((tm, tn), lambda i, j, k: (i, j)),
        out_shape=jax.ShapeDtypeStruct((m, n), out_dtype),
        scratch_shapes=scratch,
        compiler_params=_params(("parallel", "parallel", "arbitrary"), VMEM_BIG),
        name=name,
    )(*args)


def _rope_tab_kernel(pos_ref, inv_ref, c_ref, sa_ref, sb_ref):
    ang = pos_ref[...].astype(F32) * inv_ref[...]
    c = jnp.cos(ang)
    s = jnp.sin(ang)
    lane = lax.broadcasted_iota(jnp.int32, ang.shape, 1)
    c_ref[...] = jnp.where(lane < ROPE_DIM, c, 1.0)
    sa_ref[...] = jnp.where(lane < ROPE_HALF, -s, 0.0)
    sb_ref[...] = jnp.where((lane >= ROPE_HALF) & (lane < ROPE_DIM), s, 0.0)


def rope_tables(pos_flat, tm=256):
    n = pos_flat.shape[0]
    tm = min(tm, n)
    inv = ROPE_THETA ** (-jnp.arange(0, ROPE_DIM, 2, dtype=F32) / ROPE_DIM)
    inv_row = jnp.concatenate([inv, inv, jnp.zeros((LANES - ROPE_DIM,), F32)]).reshape(1, LANES)
    pos_b = jnp.broadcast_to(pos_flat[:, None], (n, LANES))
    spec = pl.BlockSpec((tm, LANES), lambda i: (i, 0))
    shp = jax.ShapeDtypeStruct((n, LANES), F32)
    return pl.pallas_call(
        _rope_tab_kernel,
        grid=(n // tm,),
        in_specs=[spec, pl.BlockSpec((1, LANES), lambda i: (0, 0))],
        out_specs=[spec, spec, spec],
        out_shape=[shp, shp, shp],
        compiler_params=_params(("parallel",)),
        name="rope_tables",
    )(pos_b, inv_row)


def _head_norm(xh, g):
    ms = jnp.mean(xh * xh, axis=-1, keepdims=True)
    return xh * lax.rsqrt(ms + NORM_EPS) * g


def _rope(xh, c, sa, sb):
    return (xh * c + pltpu.roll(xh, LANES - ROPE_HALF, 1) * sa
            + pltpu.roll(xh, ROPE_HALF, 1) * sb)


def _norm_rope_kernel(*refs, heads, do_norm, do_rope, scale):
    x_ref = refs[0]
    pos = 1
    g = None
    if do_norm:
        g = refs[pos][...]
        pos += 1
    if do_rope:
        c, sa, sb = refs[pos][...], refs[pos + 1][...], refs[pos + 2][...]
        pos += 3
    o_ref = refs[pos]
    for h in range(heads):
        xh = x_ref[:, h * HEAD_DIM:(h + 1) * HEAD_DIM]
        if do_norm:
            xh = _head_norm(xh, g)
        if do_rope:
            xh = _rope(xh, c, sa, sb)
        if scale != 1.0:
            xh = xh * scale
        o_ref[:, h * HEAD_DIM:(h + 1) * HEAD_DIM] = xh.astype(o_ref.dtype)


def norm_rope(x2d, col_start, heads, gain, tabs, scale=1.0, tm=256, name="norm_rope"):
    m = x2d.shape[0]
    hb = heads
    while col_start % (hb * HEAD_DIM):
        hb //= 2
    w = hb * HEAD_DIM
    off = col_start // w
    tm = min(tm, m)
    in_specs = [pl.BlockSpec((tm, w), lambda i, j: (i, off + j))]
    args = [x2d]
    if gain is not None:
        in_specs.append(pl.BlockSpec((1, HEAD_DIM), lambda i, j: (0, 0)))
        args.append(gain.reshape(1, HEAD_DIM))
    if tabs is not None:
        in_specs += [pl.BlockSpec((tm, LANES), lambda i, j: (i, 0))] * 3
        args += list(tabs)
    return pl.pallas_call(
        functools.partial(_norm_rope_kernel, heads=hb, do_norm=gain is not None,
                          do_rope=tabs is not None, scale=scale),
        grid=(m // tm, heads // hb),
        in_specs=in_specs,
        out_specs=pl.BlockSpec((tm, w), lambda i, j: (i, j)),
        out_shape=jax.ShapeDtypeStruct((m, heads * HEAD_DIM), BF16),
        compiler_params=_params(("parallel", "parallel")),
        name=name,
    )(*args)


def _compress_kernel(*refs, is_key):
    if is_key:
        x_ref, w1_ref, w2_ref, pe_ref, g_ref, c_ref, sa_ref, sb_ref, o_ref = refs
    else:
        x_ref, w1_ref, w2_ref, pe_ref, o_ref = refs
    half = CMP_STRIDE * HEAD_DIM
    x = x_ref[0, 0]
    nc = x.shape[0]
    top = _dot(x, w1_ref[:half, :], preferred_element_type=F32)
    bot = _dot(x, w1_ref[half:, :], preferred_element_type=F32)
    pe_term = _dot(pe_ref[...], w1_ref[...], preferred_element_type=F32)[0:1, :]
    pre = top + pltpu.roll(bot, nc - 1, 0) + pe_term
    hid = jax.nn.gelu(pre)
    out = _dot(hid.astype(BF16), w2_ref[...], preferred_element_type=F32)
    if is_key:
        out = _rope(_head_norm(out, g_ref[...]), c_ref[0], sa_ref[0], sb_ref[0])
    o_ref[0, 0] = out.astype(o_ref.dtype)


def compress(xblk, w1, w2, pe, key_extras=None, name="compress"):
    b, h, nc, wdt = xblk.shape
    pe8 = jnp.broadcast_to(pe.reshape(1, CMP_BLOCK * HEAD_DIM), (8, CMP_BLOCK * HEAD_DIM)).astype(BF16)
    in_specs = [pl.BlockSpec((1, 1, nc, wdt), lambda i, j: (i, j, 0, 0)),
                pl.BlockSpec((CMP_BLOCK * HEAD_DIM, HEAD_DIM), lambda i, j: (0, 0)),
                pl.BlockSpec((HEAD_DIM, HEAD_DIM), lambda i, j: (0, 0)),
                pl.BlockSpec((8, CMP_BLOCK * HEAD_DIM), lambda i, j: (0, 0))]
    args = [xblk, w1.astype(BF16), w2.astype(BF16), pe8]
    if key_extras is not None:
        gain, tabs = key_extras
        in_specs.append(pl.BlockSpec((1, HEAD_DIM), lambda i, j: (0, 0)))
        in_specs += [pl.BlockSpec((1, nc, LANES), lambda i, j: (i, 0, 0))] * 3
        args += [gain.reshape(1, HEAD_DIM)] + [t.reshape(b, nc, LANES) for t in tabs]
    out_dtype = BF16
    return pl.pallas_call(
        functools.partial(_compress_kernel, is_key=key_extras is not None),
        grid=(b, h),
        in_specs=in_specs,
        out_specs=pl.BlockSpec((1, 1, nc, HEAD_DIM), lambda i, j: (i, j, 0, 0)),
        out_shape=jax.ShapeDtypeStruct((b, h, nc, HEAD_DIM), out_dtype),
        compiler_params=_params(("parallel", "parallel")),
        name=name,
    )(*args)


def _stack_heads(q_ref, group, tq):
    return jnp.concatenate([q_ref[0, :, g * HEAD_DIM:(g + 1) * HEAD_DIM] for g in range(group)], axis=0)


def _unstack_heads(o_ref, o_t, group, tq):
    o = o_t.T
    for g in range(group):
        o_ref[0, :, g * HEAD_DIM:(g + 1) * HEAD_DIM] = o[g * tq:(g + 1) * tq, :].astype(o_ref.dtype)


def _cmp_kernel(q_ref, kc_ref, vct_ref, mt_ref, o_ref, sel_ref, *, tq, n_sel, topn):
    group = NSA_GROUP
    t0 = pl.program_id(2) * tq
    q = _stack_heads(q_ref, group, tq)
    kc = kc_ref[0, 0]
    nc = kc.shape[0]
    s = _nt_dot(kc, q)
    c_idx = lax.broadcasted_iota(jnp.int32, (nc, tq), 0)
    t_idx = t0 + lax.broadcasted_iota(jnp.int32, (nc, tq), 1)
    ok1 = (CMP_STRIDE * c_idx + CMP_BLOCK - 1) <= t_idx
    bias1 = jnp.where(ok1, 0.0, NEG_INF)
    okf1 = jnp.where(ok1, 1.0, 0.0)
    bias = jnp.concatenate([bias1] * group, axis=1)
    okf = jnp.concatenate([okf1] * group, axis=1)
    s = s + bias
    m = jnp.max(s, axis=0, keepdims=True)
    e = jnp.exp(s - m) * okf
    l = jnp.sum(e, axis=0, keepdims=True)
    p = e * jnp.where(l > 0.0, 1.0 / l, 0.0)
    o_t = _dot(vct_ref[0, 0], p.astype(BF16), preferred_element_type=F32)
    _unstack_heads(o_ref, o_t, group, tq)

    psum = p[:, 0:tq]
    for g in range(1, group):
        psum = psum + p[:, g * tq:(g + 1) * tq]
    imp = _dot(mt_ref[...], psum, preferred_element_type=F32,
                  precision=lax.Precision.HIGHEST)
    j_idx = lax.broadcasted_iota(jnp.int32, (n_sel, tq), 0)
    jt = (t0 + lax.broadcasted_iota(jnp.int32, (n_sel, tq), 1)) // SEL_BLOCK
    forced = (j_idx == 0) | (j_idx == jt) | (j_idx == jt - 1)
    imp = jnp.where(forced, FORCED_SCORE, imp)
    imp = jnp.where(j_idx <= jt, imp, -jnp.inf)
    rank = jnp.zeros((n_sel, tq), F32)
    for jp in range(n_sel):
        row = imp[jp:jp + 1, :]
        before = (row > imp) | ((row == imp) & (j_idx > jp))
        rank = rank + jnp.where(before, 1.0, 0.0)
    bias_t = jnp.where(rank < topn, 0.0, -MASK_BIG)
    bias_t = jnp.concatenate([bias_t, jnp.zeros((LANES - n_sel, tq), F32)], axis=0)
    sel_ref[0, 0] = bias_t.T.astype(sel_ref.dtype)


def nsa_compressed(qn, kc, vct, b, t, tq=256):
    nc = kc.shape[2]
    n_sel = t // SEL_BLOCK
    assert n_sel <= LANES
    topn = min(SEL_TOPN, n_sel)
    tq = min(tq, t)
    cs = CMP_STRIDE * np.arange(nc)[:, None]
    ss = SEL_BLOCK * np.arange(n_sel)[None, :]
    ov = np.clip(np.minimum(cs + CMP_BLOCK, ss + SEL_BLOCK) - np.maximum(cs, ss), 0, None) / CMP_BLOCK
    ov[nc - 1, :] = 0.0
    mt = jnp.asarray(ov.T, dtype=F32)
    gw = NSA_GROUP * HEAD_DIM
    return pl.pallas_call(
        functools.partial(_cmp_kernel, tq=tq, n_sel=n_sel, topn=topn),
        grid=(b, NSA_KV_HEADS, t // tq),
        in_specs=[pl.BlockSpec((1, tq, gw), lambda i, h, j: (i, j, h)),
                  pl.BlockSpec((1, 1, nc, HEAD_DIM), lambda i, h, j: (i, h, 0, 0)),
                  pl.BlockSpec((1, 1, HEAD_DIM, nc), lambda i, h, j: (i, h, 0, 0)),
                  pl.BlockSpec((n_sel, nc), lambda i, h, j: (0, 0))],
        out_specs=[pl.BlockSpec((1, tq, gw), lambda i, h, j: (i, j, h)),
                   pl.BlockSpec((1, 1, tq, LANES), lambda i, h, j: (i, h, j, 0))],
        out_shape=[jax.ShapeDtypeStruct((b, t, NSA_HEADS * HEAD_DIM), F32),
                   jax.ShapeDtypeStruct((b, NSA_KV_HEADS, t, LANES), BF16)],
        compiler_params=_params(("parallel", "parallel", "parallel")),
        name="nsa_compressed",
    )(qn, kc, vct, mt)


def _flash_kernel(*refs, mode, group, tq, tk):
    if mode == "window":
        q_ref, k_ref, vt_ref, o_ref, q_s, m_s, l_s, acc_s = refs
        xq = None
    else:
        q_ref, k_ref, vt_ref, xq_ref, xk_ref, o_ref, q_s, m_s, l_s, acc_s = refs
        xq = xq_ref[0, 0] if mode == "sel" else xq_ref[...]
    qi = pl.program_id(2)
    t0 = qi * tq
    q = _stack_heads(q_ref, group, tq)
    if xq is not None:
        q = jnp.concatenate([q, jnp.concatenate([xq] * group, axis=0)], axis=1)
    q_s[...] = q
    m_s[...] = jnp.full(m_s.shape, NEG_INF, F32)
    l_s[...] = jnp.zeros(l_s.shape, F32)
    acc_s[...] = jnp.zeros(acc_s.shape, F32)

    def tile(j, keep):
        k0 = pl.multiple_of(j * tk, tk)
        kt = k_ref[0, pl.ds(k0, tk), :]
        if mode == "sel":
            kt = jnp.concatenate([kt, xk_ref[pl.ds(k0, tk), :]], axis=1)
        elif mode == "dsa":
            kt = jnp.concatenate([kt, xk_ref[0, pl.ds(k0, tk), :]], axis=1)
        s = _nt_dot(kt, q_s[...])
        if keep is not None:
            kpos = k0 + lax.broadcasted_iota(jnp.int32, (tk, tq), 0)
            tpos = t0 + lax.broadcasted_iota(jnp.int32, (tk, tq), 1)
            ok = keep(kpos, tpos)
            s = jnp.concatenate([jnp.where(ok, s[:, g * tq:(g + 1) * tq], NEG_INF)
                                 for g in range(group)], axis=1)
        m_old = m_s[...]
        m_new = jnp.maximum(m_old, jnp.max(s, axis=0, keepdims=True))
        alpha = jnp.exp(m_old - m_new)
        p = jnp.exp(s - m_new)
        l_s[...] = alpha * l_s[...] + jnp.sum(p, axis=0, keepdims=True)
        acc_s[...] = acc_s[...] * alpha + _dot(vt_ref[0, 0, j], p.astype(BF16))
        m_s[...] = m_new

    causal = lambda kpos, tpos: kpos <= tpos

    def plain(j, carry):
        tile(j, None)
        return carry

    if mode == "dsa":
        lax.fori_loop(0, (t0 + tq - 1) // tk + 1, plain, 0)
    elif mode == "sel":
        jd = (t0 + tq - 1) // tk
        lax.fori_loop(0, jd, plain, 0)
        tile(jd, causal)
    else:
        nw = WINDOW // tk

        @pl.when(qi >= nw)
        def _():
            tile(qi - nw, lambda kpos, tpos: tpos - kpos < WINDOW)

        for dist in range(nw - 1, 0, -1):
            @pl.when(qi >= dist)
            def _():
                tile(qi - dist, None)

        tile(qi, causal)

    _unstack_heads(o_ref, acc_s[...] * (1.0 / l_s[...]), group, tq)


def flash_masked(qn, kn, vt, *, mode, group, b, t, tq, tk, extra=None, name="flash"):
    kvh = kn.shape[2] // HEAD_DIM
    gw = group * HEAD_DIM
    tq = min(tq, t)
    tk = min(tk, t)
    ntk = t // tk
    vt5 = vt.reshape(b, kvh, HEAD_DIM, ntk, tk).transpose(0, 1, 3, 2, 4)
    in_specs = [pl.BlockSpec((1, tq, gw), lambda i, h, qi: (i, qi, h)),
                pl.BlockSpec((1, t, HEAD_DIM), lambda i, h, qi: (i, 0, h)),
                pl.BlockSpec((1, 1, ntk, HEAD_DIM, tk), lambda i, h, qi: (i, h, 0, 0, 0))]
    args = [qn, kn, vt5]
    kdim = HEAD_DIM
    if mode == "sel":
        assert tk % tq == 0
        et = np.zeros((t, LANES), np.float32)
        et[np.arange(t), np.arange(t) // SEL_BLOCK] = 1.0
        in_specs += [pl.BlockSpec((1, 1, tq, LANES), lambda i, h, qi: (i, h, qi, 0)),
                     pl.BlockSpec((t, LANES), lambda i, h, qi: (0, 0))]
        args += [extra, jnp.asarray(et, dtype=BF16)]
        kdim += LANES
    elif mode == "dsa":
        assert tq == LANES
        in_specs += [pl.BlockSpec((tq, LANES), lambda i, h, qi: (0, 0)),
                     pl.BlockSpec((1, t, tq), lambda i, h, qi: (i, 0, qi))]
        args += [jnp.eye(tq, dtype=BF16), extra]
        kdim += LANES
    else:
        assert tk == tq and WINDOW % tk == 0
    return pl.pallas_call(
        functools.partial(_flash_kernel, mode=mode, group=group, tq=tq, tk=tk),
        grid=(b, kvh, t // tq),
        in_specs=in_specs,
        out_specs=pl.BlockSpec((1, tq, gw), lambda i, h, qi: (i, qi, h)),
        out_shape=jax.ShapeDtypeStruct((b, t, kvh * gw), F32),
        scratch_shapes=[pltpu.VMEM((group * tq, kdim), BF16),
                        pltpu.VMEM((1, group * tq), F32),
                        pltpu.VMEM((1, group * tq), F32),
                        pltpu.VMEM((HEAD_DIM, group * tq), F32)],
        compiler_params=_params(("parallel", "parallel", "parallel")),
        name=name,
    )(*args)


def _nsa_merge_kernel(oc_ref, os_ref, ow_ref, g_ref, e_ref, o_ref):
    gate = _sigmoid(g_ref[...])
    out = None
    for j, src in enumerate((oc_ref, os_ref, ow_ref)):
        gj = _split_dot(gate, e_ref[j])
        term = gj * src[...]
        out = term if out is None else out + term
    o_ref[...] = out.astype(o_ref.dtype)


def nsa_merge(oc, os_, ow, pc, tm=256):
    m, w = oc.shape
    tm = min(tm, m)
    e = np.zeros((3, LANES, w), np.float32)
    for h in range(NSA_HEADS):
        for j in range(3):
            e[j, h * 3 + j, h * HEAD_DIM:(h + 1) * HEAD_DIM] = 1.0
    spec = pl.BlockSpec((tm, w), lambda i: (i, 0))
    return pl.pallas_call(
        _nsa_merge_kernel,
        grid=(m // tm,),
        in_specs=[spec, spec, spec,
                  pl.BlockSpec((tm, LANES), lambda i: (i, 0)),
                  pl.BlockSpec((3, LANES, w), lambda i: (0, 0, 0))],
        out_specs=spec,
        out_shape=jax.ShapeDtypeStruct((m, w), BF16),
        compiler_params=_params(("parallel",)),
        name="nsa_merge",
    )(oc, os_, ow, pc, jnp.asarray(e, dtype=BF16))


def _seg_sum(x, bd):
    outs = []
    for s in range(x.shape[1] // LANES):
        outs.append(_split_dot(x[:, s * LANES:(s + 1) * LANES], bd))
    return jnp.concatenate(outs, axis=1)


def _shifted(x, prev_row, first):
    rolled = pltpu.roll(x, 1, 0)
    row0 = jnp.where(first, 0.0, prev_row)
    ridx = lax.broadcasted_iota(jnp.int32, x.shape, 0)
    return jnp.where(ridx == 0, row0, rolled)


def _rwkv_prep_kernel(pb_ref, pbp_ref, pc_ref, pcp_ref, mub_ref, muc_ref, w0_ref, w2_ref, a0_ref, a2_ref,
                      g2_ref, kkg_ref, ka_ref, bd_ref,
                      r_ref, w_ref, k_ref, v_ref, kk_ref, b_ref, g_ref, *, tiles_per_seq):
    first = (pl.program_id(0) % tiles_per_seq) == 0
    w = RWKV_WIDTH
    xb = pb_ref[...]
    xb = xb + (_shifted(xb, pbp_ref[7:8, :], first) - xb) * mub_ref[...]
    xc = pc_ref[...]
    xc = xc + (_shifted(xc, pcp_ref[7:8, :], first) - xc) * muc_ref[...]
    r, k, v = xb[:, :w], xb[:, w:2 * w], xb[:, 2 * w:]
    wd, ad, gd = xc[:, LANES:2 * LANES], xc[:, 2 * LANES:3 * LANES], xc[:, 3 * LANES:]
    z = w0_ref[...] + _dot(jnp.tanh(wd).astype(BF16), w2_ref[...], preferred_element_type=F32)
    nz = -z
    softplus = jnp.maximum(nz, 0.0) + jnp.log(1.0 + jnp.exp(-jnp.abs(nz)))
    w_log = -softplus - 0.5
    a = _sigmoid(a0_ref[...] + _dot(ad.astype(BF16), a2_ref[...], preferred_element_type=F32))
    g = _dot(_sigmoid(gd).astype(BF16), g2_ref[...], preferred_element_type=F32)
    kkv = k * kkg_ref[...]
    norm = jnp.sqrt(_seg_sum(kkv * kkv, bd_ref[...]))
    kkv = kkv / jnp.maximum(norm, 1e-12)
    r_ref[...] = r
    w_ref[...] = jnp.exp(-jnp.exp(w_log))
    k_ref[...] = k * (1.0 + (a - 1.0) * ka_ref[...])
    v_ref[...] = v
    kk_ref[...] = kkv
    b_ref[...] = kkv * a
    g_ref[...] = g


def _pad_rows(wm, rows):
    return jnp.pad(wm, ((0, rows - wm.shape[0]), (0, 0)))


def _head_block_ones():
    bd = np.zeros((LANES, LANES), np.float32)
    bd[:RWKV_HEAD, :RWKV_HEAD] = 1.0
    bd[RWKV_HEAD:, RWKV_HEAD:] = 1.0
    return jnp.asarray(bd, dtype=BF16)


def rwkv_prep(pb, pc, t, mu, w0, w2, a0, a2, g2, kk_gain, k_a, tm=128):
    m = pb.shape[0]
    w = RWKV_WIDTH
    tm = min(tm, t)
    mu_r, mu_k, mu_v, mu_wd, mu_ad, mu_gd = jnp.split(
        mu, [int(x) for x in np.cumsum([w, w, w, LORA_DECAY, LORA_AAA])])
    mub = jnp.concatenate([mu_r, mu_k, mu_v]).reshape(1, 3 * w)
    pad = lambda z: jnp.pad(z, (0, LANES - z.shape[0]))
    muc = jnp.concatenate([jnp.zeros((LANES,), F32), pad(mu_wd), pad(mu_ad), mu_gd]).reshape(1, -1)
    cw = pc.shape[1]
    row = lambda z: z.reshape(1, w)
    full = lambda shape: pl.BlockSpec(shape, lambda i: (0,) * len(shape))
    tile = lambda width: pl.BlockSpec((tm, width), lambda i: (i, 0))
    prev = lambda width: pl.BlockSpec((8, width), lambda i: (jnp.maximum(i * (tm // 8) - 1, 0), 0))
    out_spec = tile(w)
    shp = jax.ShapeDtypeStruct((m, w), F32)
    return pl.pallas_call(
        functools.partial(_rwkv_prep_kernel, tiles_per_seq=t // tm),
        grid=(m // tm,),
        in_specs=[tile(3 * w), prev(3 * w), tile(cw), prev(cw), full((1, 3 * w)), full((1, cw)),
                  full((1, w)), full((LANES, w)), full((1, w)), full((LANES, w)), full((LORA_GATE, w)),
                  full((1, w)), full((1, w)), full((LANES, LANES))],
        out_specs=[out_spec] * 7,
        out_shape=[shp] * 7,
        compiler_params=_params(("parallel",), VMEM_BIG),
        name="rwkv_prep",
    )(pb, pb, pc, pc, mub, muc, row(w0), _pad_rows(w2, LANES).astype(BF16), row(a0),
      _pad_rows(a2, LANES).astype(BF16), g2.astype(BF16), row(kk_gain), row(k_a), _head_block_ones())


def _rwkv_scan_kernel(w_ref, kk_ref, b_ref, k_ref, r_ref, v_ref, y_ref, s_ref, *, tb, rows):
    ig = pl.program_id(1)

    @pl.when(pl.program_id(0) == 0)
    def _():
        s_ref[ig] = jnp.zeros(s_ref.shape[1:], F32)

    def step(t, s):
        kk = kk_ref[t]
        sa = -jnp.sum(s * kk[None], axis=1)
        vrow = v_ref[t, 0]
        s = (s * w_ref[t][None] + sa[:, None, :] * b_ref[t][None]
             + vrow[:, None, :] * k_ref[t][None])
        y_ref[t, 0] = jnp.sum(s * r_ref[t][None], axis=1)
        return s

    s_ref[ig] = lax.fori_loop(0, tb, step, s_ref[ig], unroll=2)


def rwkv_scan(w, kk, bb, k, r, v, tb=64, rows=4):
    t, n, lanes = w.shape
    groups = n // rows
    tb = min(tb, t)
    op = pl.BlockSpec((tb, n, lanes), lambda i, j: (i, 0, 0))
    vy = pl.BlockSpec((tb, 1, rows, lanes), lambda i, j: (i, j, 0, 0))
    return pl.pallas_call(
        functools.partial(_rwkv_scan_kernel, tb=tb, rows=rows),
        grid=(t // tb, groups),
        in_specs=[op, op, op, op, op, vy],
        out_specs=vy,
        out_shape=jax.ShapeDtypeStruct((t, groups, rows, lanes), F32),
        scratch_shapes=[pltpu.VMEM((groups, rows, n, lanes), F32)],
        compiler_params=_params(("arbitrary", "arbitrary"), VMEM_BIG),
        name="rwkv_scan",
    )(w, kk, bb, k, r, v)


def _rwkv_post_kernel(y_ref, r_ref, k_ref, v_ref, g_ref, rk_ref, gg_ref, gb_ref, bd_ref, o_ref):
    bd = bd_ref[...]
    y = y_ref[...]
    inv_n = 1.0 / RWKV_HEAD
    mean = _seg_sum(y, bd) * inv_n
    d = y - mean
    var = _seg_sum(d * d, bd) * inv_n
    yn = d * lax.rsqrt(var + GN_EPS) * gg_ref[...] + gb_ref[...]
    bonus = _seg_sum(r_ref[...] * k_ref[...] * rk_ref[...], bd) * v_ref[...]
    o_ref[...] = ((yn + bonus) * g_ref[...]).astype(o_ref.dtype)


def rwkv_post(y, r, k, v, g, r_k, gn_g, gn_b, tm=256):
    m, w = y.shape
    tm = min(tm, m)
    tile = pl.BlockSpec((tm, w), lambda i: (i, 0))
    rowspec = pl.BlockSpec((1, w), lambda i: (0, 0))
    return pl.pallas_call(
        _rwkv_post_kernel,
        grid=(m // tm,),
        in_specs=[tile] * 5 + [rowspec] * 3 + [pl.BlockSpec((LANES, LANES), lambda i: (0, 0))],
        out_specs=tile,
        out_shape=jax.ShapeDtypeStruct((m, w), BF16),
        compiler_params=_params(("parallel",)),
        name="rwkv_post",
    )(y, r, k, v, g, r_k.reshape(1, w), gn_g.reshape(1, w), gn_b.reshape(1, w), _head_block_ones())


def rwkv7(pb, pc, b, t, mu, w0, w2, a0, a2, g2, kk_gain, k_a, r_k, gn_g, gn_b):
    r, w, k, v, kk, bb, g = rwkv_prep(pb, pc, t, mu, w0, w2, a0, a2, g2, kk_gain, k_a)
    rows = 4

    def to_scan(z):
        return z.reshape(b, t, RWKV_HEADS, RWKV_HEAD).transpose(1, 3, 0, 2).reshape(t, RWKV_HEAD, b * RWKV_HEADS)

    v_s = to_scan(v).reshape(t, RWKV_HEAD // rows, rows, b * RWKV_HEADS)
    y = rwkv_scan(to_scan(w), to_scan(kk), to_scan(bb), to_scan(k), to_scan(r), v_s, rows=rows)
    y = y.reshape(t, RWKV_HEAD, b, RWKV_HEADS).transpose(2, 0, 3, 1).reshape(b * t, RWKV_WIDTH)
    return rwkv_post(y, r, k, v, g, r_k, gn_g, gn_b)


def _dsa_select_kernel(qi_ref, ki_ref, wt_ref, bias_ref, q_s, key_s, *, t, tq, kc, topk):
    t0 = pl.program_id(1) * tq
    nchunk = (t0 + tq - 1) // kc + 1
    for h in range(IDX_HEADS):
        q_s[h * tq:(h + 1) * tq, :] = qi_ref[0, :, h * IDX_DIM:(h + 1) * IDX_DIM]
    wt = wt_ref[0]

    def chunk(c, carry):
        k0 = pl.multiple_of(c * kc, kc)
        lg = _nt_dot(ki_ref[0, pl.ds(k0, kc), :], q_s[...])
        acc = jnp.zeros((kc, tq), F32)
        for h in range(IDX_HEADS):
            acc = acc + jnp.maximum(lg[:, h * tq:(h + 1) * tq], 0.0) * wt[h:h + 1, :]
        spos = k0 + lax.broadcasted_iota(jnp.int32, (kc, tq), 0)
        tpos = t0 + lax.broadcasted_iota(jnp.int32, (kc, tq), 1)
        score = jnp.where(spos <= tpos, acc + 0.0, -jnp.inf)
        bits = pltpu.bitcast(score, jnp.int32)
        key_s[pl.ds(k0, kc), :] = jnp.where(bits < 0, bits ^ jnp.int32(0x7FFFFFFF), bits)
        return carry

    lax.fori_loop(0, nchunk, chunk, 0)
    kf = jnp.float32(topk)
    part = 64

    def count(pred):
        def body(c, acc):
            k0 = pl.multiple_of(c * kc, kc)
            ind = jnp.where(pred(key_s[pl.ds(k0, kc), :], k0), 1.0, 0.0)
            return acc + jnp.sum(ind.reshape(kc // part, part, tq), axis=0)
        acc = lax.fori_loop(0, nchunk, body, jnp.zeros((part, tq), F32))
        return jnp.sum(acc, axis=0, keepdims=True)

    def value_bit(it, tau_u):
        cand_u = tau_u | lax.shift_left(jnp.int32(1), 31 - it)
        cand_s = cand_u ^ jnp.int32(INT_MIN)
        return jnp.where(count(lambda blk, k0: blk >= cand_s) >= kf, cand_u, tau_u)

    tau_u = lax.fori_loop(0, 32, value_bit, jnp.zeros((1, tq), jnp.int32))
    tau = tau_u ^ jnp.int32(INT_MIN)
    need = kf - count(lambda blk, k0: blk > tau)
    n_eq = count(lambda blk, k0: blk == tau)
    nbits = int(np.log2(t))

    def tie_search():
        def index_bit(it, jj):
            cand = jj | lax.shift_left(jnp.int32(1), nbits - 1 - it)

            def pred(blk, k0):
                spos = k0 + lax.broadcasted_iota(jnp.int32, (kc, tq), 0)
                return (blk == tau) & (spos < cand)
            return jnp.where(count(pred) < need, cand, jj)
        return lax.fori_loop(0, nbits, index_bit, jnp.zeros((1, tq), jnp.int32))

    jj = lax.cond(jnp.max(n_eq - need) > 0.0, tie_search, lambda: jnp.full((1, tq), t, jnp.int32))

    def emit(c, carry):
        k0 = pl.multiple_of(c * kc, kc)
        blk = key_s[pl.ds(k0, kc), :]
        spos = k0 + lax.broadcasted_iota(jnp.int32, (kc, tq), 0)
        tpos = t0 + lax.broadcasted_iota(jnp.int32, (kc, tq), 1)
        keep = ((blk > tau) | ((blk == tau) & (spos <= jj))) & (spos <= tpos)
        bias_ref[0, pl.ds(k0, kc), :] = jnp.where(keep, 0.0, -MASK_BIG).astype(bias_ref.dtype)
        return carry

    lax.fori_loop(0, nchunk, emit, 0)

    def fill(c, carry):
        k0 = pl.multiple_of(c * kc, kc)
        bias_ref[0, pl.ds(k0, kc), :] = jnp.full((kc, tq), -MASK_BIG, bias_ref.dtype)
        return carry

    lax.fori_loop(nchunk, t // kc, fill, 0)


def dsa_select(qi, ki, wt, b, t, tq=128, kc=512):
    topk = min(DSA_TOPK_MAX, t // 4)
    kc = min(kc, t)
    assert topk <= kc and t % kc == 0
    return pl.pallas_call(
        functools.partial(_dsa_select_kernel, t=t, tq=tq, kc=kc, topk=topk),
        grid=(b, t // tq),
        in_specs=[pl.BlockSpec((1, tq, IDX_HEADS * IDX_DIM), lambda i, j: (i, j, 0)),
                  pl.BlockSpec((1, t, IDX_DIM), lambda i, j: (i, 0, 0)),
                  pl.BlockSpec((1, IDX_HEADS, tq), lambda i, j: (i, 0, j))],
        out_specs=pl.BlockSpec((1, t, tq), lambda i, j: (i, 0, j)),
        out_shape=jax.ShapeDtypeStruct((b, t, t), BF16),
        scratch_shapes=[pltpu.VMEM((IDX_HEADS * tq, IDX_DIM), BF16),
                        pltpu.VMEM((t, tq), jnp.int32)],
        compiler_params=_params(("parallel", "parallel"), VMEM_BIG),
        name="dsa_select",
    )(qi, ki, wt)


def _xattn_kernel(q_ref, g_ref, mk_ref, mv_ref, o_ref):
    scale = HEAD_DIM ** -0.5
    for h in range(MEM_HEADS):
        sl = slice(h * HEAD_DIM, (h + 1) * HEAD_DIM)
        qh = (_head_norm(q_ref[:, sl], g_ref[...]) * scale).astype(BF16)
        s = _nt_dot(qh, mk_ref[0, :, sl])
        m = jnp.max(s, axis=-1, keepdims=True)
        e = jnp.exp(s - m)
        p = e / jnp.sum(e, axis=-1, keepdims=True)
        o_ref[:, sl] = _dot(p.astype(BF16), mv_ref[0, :, sl],
                               preferred_element_type=F32).astype(o_ref.dtype)


def mem_xattn(qm, gain, mk, mv, b, t, tq=256):
    tq = min(tq, t)
    nt = t // tq
    mtok = mk.shape[1]
    return pl.pallas_call(
        _xattn_kernel,
        grid=(b, nt),
        in_specs=[pl.BlockSpec((tq, MEM_WIDTH), lambda i, j: (i * nt + j, 0)),
                  pl.BlockSpec((1, HEAD_DIM), lambda i, j: (0, 0)),
                  pl.BlockSpec((1, mtok, MEM_WIDTH), lambda i, j: (i, 0, 0)),
                  pl.BlockSpec((1, mtok, MEM_WIDTH), lambda i, j: (i, 0, 0))],
        out_specs=pl.BlockSpec((tq, MEM_WIDTH), lambda i, j: (i * nt + j, 0)),
        out_shape=jax.ShapeDtypeStruct((b * t, MEM_WIDTH), BF16),
        compiler_params=_params(("parallel", "parallel")),
        name="mem_xattn",
    )(qm, gain.reshape(1, HEAD_DIM), mk, mv)


def _kv_transposed(x2d, lo, b, t, heads):
    v = x2d[:, lo:lo + heads * HEAD_DIM].astype(BF16)
    return v.reshape(b, t, heads, HEAD_DIM).transpose(0, 2, 3, 1)


def nsa_mixer(pa, pc, b, t, tabs, cmp_tabs, q_norm, kc_norm, ks_norm, kw_norm, pe_k, pe_v,
              ck_w1, ck_w2, cv_w1, cv_w2):
    kvw = NSA_KV_HEADS * HEAD_DIM
    scale = HEAD_DIM ** -0.5
    qn = norm_rope(pa, 0, NSA_HEADS, q_norm, tabs, scale=scale, name="nsa_q_prep")
    ksn = norm_rope(pa, 6 * kvw, NSA_KV_HEADS, ks_norm, tabs, name="nsa_ks_prep").reshape(b, t, kvw)
    kwn = norm_rope(pa, 8 * kvw, NSA_KV_HEADS, kw_norm, tabs, name="nsa_kw_prep").reshape(b, t, kvw)
    vst = _kv_transposed(pa, 7 * kvw, b, t, NSA_KV_HEADS)
    vwt = _kv_transposed(pa, 9 * kvw, b, t, NSA_KV_HEADS)
    qn = qn.reshape(b, t, NSA_HEADS * HEAD_DIM)

    nc = t // CMP_STRIDE

    def chunked(lo):
        z = pa[:, lo:lo + kvw].astype(BF16).reshape(b, nc, CMP_STRIDE, NSA_KV_HEADS, HEAD_DIM)
        return z.transpose(0, 3, 1, 2, 4).reshape(b, NSA_KV_HEADS, nc, CMP_STRIDE * HEAD_DIM)

    kc = compress(chunked(4 * kvw), ck_w1, ck_w2, pe_k, key_extras=(kc_norm, cmp_tabs), name="compress_k")
    vc = compress(chunked(5 * kvw), cv_w1, cv_w2, pe_v, name="compress_v")
    vct = vc.transpose(0, 1, 3, 2)
    o_c, sel_bias = nsa_compressed(qn, kc, vct, b, t)
    o_s = flash_masked(qn, ksn, vst, mode="sel", group=NSA_GROUP, b=b, t=t, tq=256, tk=256,
                       extra=sel_bias, name="nsa_selected")
    o_w = flash_masked(qn, kwn, vwt, mode="window", group=NSA_GROUP, b=b, t=t, tq=256, tk=256,
                       name="nsa_window")
    m = b * t
    w = NSA_HEADS * HEAD_DIM
    return nsa_merge(o_c.reshape(m, w), o_s.reshape(m, w), o_w.reshape(m, w), pc)


def dsa_mixer_core(p1, p2, b, t, tabs, q_norm, k_norm, ki_norm):
    scale = HEAD_DIM ** -0.5
    qw = DSA_HEADS * HEAD_DIM
    kvw = DSA_KV_HEADS * HEAD_DIM
    qn = norm_rope(p1, 0, DSA_HEADS, q_norm, tabs, scale=scale, name="dsa_q_prep").reshape(b, t, qw)
    kn = norm_rope(p1, qw, DSA_KV_HEADS, k_norm, tabs, name="dsa_k_prep").reshape(b, t, kvw)
    vt = _kv_transposed(p1, qw + kvw, b, t, DSA_KV_HEADS)
    qi = norm_rope(p1, qw + 2 * kvw, IDX_HEADS, None, tabs, name="dsa_qi_prep")
    qi = qi.reshape(b, t, IDX_HEADS * IDX_DIM)
    ki = norm_rope(p2, 0, 1, ki_norm, tabs, name="dsa_ki_prep").reshape(b, t, IDX_DIM)
    wscale = IDX_HEADS ** -0.5 * IDX_DIM ** -0.5
    wt = (p2[:, IDX_DIM:IDX_DIM + IDX_HEADS] * wscale).reshape(b, t, IDX_HEADS).transpose(0, 2, 1)
    key_bias = dsa_select(qi, ki, wt, b, t)
    o = flash_masked(qn, kn, vt, mode="dsa", group=DSA_GROUP, b=b, t=t, tq=128, tk=512,
                     extra=key_bias, name="dsa_attention")
    return o.reshape(b * t, qw)


def kernel(x, mem, positions, mem_norm, mem_w_kv, mem_k_norm, l0_mix_norm, l0_w_in, l0_w_out, nsa_q_norm, nsa_kc_norm, nsa_ks_norm, nsa_kw_norm, nsa_pe_k, nsa_pe_v, nsa_ck_w1, nsa_ck_w2, nsa_cv_w1, nsa_cv_w2, rwkv_mu, rwkv_w0, rwkv_w2, rwkv_a0, rwkv_a2, rwkv_g2, rwkv_kk, rwkv_ka, rwkv_rk, rwkv_gn_g, rwkv_gn_b, l0_xattn_norm, l0_mem_wq, l0_mem_q_norm, l0_mem_wo, l0_ffn_norm, l0_w1, l0_w3, l0_w2, l1_mix_norm, l1_w_in, l1_w_out, dsa_q_norm, dsa_k_norm, dsa_ki_norm, l1_xattn_norm, l1_mem_wq, l1_mem_q_norm, l1_mem_wo, l1_ffn_norm, l1_w1, l1_w3, l1_w2):
    b, t, d = x.shape
    m = b * t
    bf = lambda z: z.astype(BF16)
    x2 = x.reshape(m, d)

    tabs = rope_tables(positions.reshape(m))
    nc = t // CMP_STRIDE
    cmp_pos = positions[:, CMP_BLOCK - 1::CMP_STRIDE]
    cmp_pos = jnp.concatenate([cmp_pos, cmp_pos[:, -1:]], axis=1)
    cmp_tabs = rope_tables(cmp_pos.reshape(b * nc))

    mtok = mem.shape[1]
    memn = rmsnorm(mem.reshape(b * mtok, d), mem_norm)
    mkv = matmul(memn, bf(mem_w_kv), tm=512, tn=512, name="mem_kv_proj")
    mk = norm_rope(mkv, 0, MEM_HEADS, mem_k_norm, None, name="mem_k_norm").reshape(b, mtok, MEM_WIDTH)
    mv = bf(mkv[:, MEM_WIDTH:]).reshape(b, mtok, MEM_WIDTH)

    def tail(x2, xn, wq, qn, wo, fn, w1, w3, w2):
        h = rmsnorm(x2, xn)
        qm = matmul(h, bf(wq), tm=1024, tn=512, name="xattn_q_proj")
        o = mem_xattn(qm, qn, mk, mv, b, t)
        x2 = matmul(o, bf(wo), tm=1024, tn=512, res=x2, name="xattn_out_proj")
        h = rmsnorm(x2, fn)
        u = matmul(h, [bf(w1), bf(w3)], tm=1024, tn=256, out_dtype=BF16, name="ffn_up")
        return matmul(u, bf(w2), tm=512, tn=512, res=x2, name="ffn_down")

    nsa_cols = NSA_HEADS * HEAD_DIM + 6 * NSA_KV_HEADS * HEAD_DIM
    gate_cols = 3 * NSA_HEADS
    rw0 = nsa_cols + gate_cols
    rw1 = rw0 + 3 * RWKV_WIDTH
    pad_cols = lambda wm: jnp.pad(wm, ((0, 0), (0, LANES - wm.shape[1])))
    w_c = jnp.concatenate([
        pad_cols(l0_w_in[:, nsa_cols:rw0]),
        pad_cols(l0_w_in[:, rw1:rw1 + LORA_DECAY]),
        pad_cols(l0_w_in[:, rw1 + LORA_DECAY:rw1 + LORA_DECAY + LORA_AAA]),
        l0_w_in[:, rw1 + LORA_DECAY + LORA_AAA:]], axis=1)
    h = rmsnorm(x2, l0_mix_norm)
    pa = matmul(h, bf(l0_w_in[:, :nsa_cols]), tm=1024, tn=512, name="l0_proj_nsa")
    pb = matmul(h, bf(l0_w_in[:, rw0:rw1]), tm=1024, tn=512, name="l0_proj_rwkv")
    pc = matmul(h, bf(w_c), tm=1024, tn=w_c.shape[1], name="l0_proj_small")
    o_a = nsa_mixer(pa, pc, b, t, tabs, cmp_tabs, nsa_q_norm, nsa_kc_norm, nsa_ks_norm, nsa_kw_norm,
                    nsa_pe_k, nsa_pe_v, nsa_ck_w1, nsa_ck_w2, nsa_cv_w1, nsa_cv_w2)
    o_b = rwkv7(pb, pc, b, t, rwkv_mu, rwkv_w0, rwkv_w2, rwkv_a0, rwkv_a2, rwkv_g2, rwkv_kk, rwkv_ka,
                rwkv_rk, rwkv_gn_g, rwkv_gn_b)
    x2 = matmul(jnp.concatenate([o_a, o_b], axis=1), bf(l0_w_out), tm=1024, tn=512, res=x2,
                name="l0_out_proj")
    x2 = tail(x2, l0_xattn_norm, l0_mem_wq, l0_mem_q_norm, l0_mem_wo, l0_ffn_norm, l0_w1, l0_w3, l0_w2)

    main_cols = DSA_HEADS * HEAD_DIM + 2 * DSA_KV_HEADS * HEAD_DIM + IDX_HEADS * IDX_DIM
    w_s = jnp.concatenate([l1_w_in[:, main_cols:main_cols + IDX_DIM],
                           pad_cols(l1_w_in[:, main_cols + IDX_DIM:])], axis=1)
    h = rmsnorm(x2, l1_mix_norm)
    p1 = matmul(h, bf(l1_w_in[:, :main_cols]), tm=1024, tn=512, name="l1_proj_main")
    p2 = matmul(h, bf(w_s), tm=1024, tn=w_s.shape[1], name="l1_proj_small")
    o = dsa_mixer_core(p1, p2, b, t, tabs, dsa_q_norm, dsa_k_norm, dsa_ki_norm)
    x2 = matmul(bf(o), bf(l1_w_out), tm=1024, tn=512, res=x2, name="l1_out_proj")
    x2 = tail(x2, l1_xattn_norm, l1_mem_wq, l1_mem_q_norm, l1_mem_wo, l1_ffn_norm, l1_w1, l1_w3, l1_w2)
    return x2.reshape(b, t, d)
```

```python
import functools

import numpy as np
import jax
import jax.numpy as jnp
from jax import lax
from jax.experimental import pallas as pl
from jax.experimental.pallas import tpu as pltpu

F32 = jnp.float32
BF16 = jnp.bfloat16

HEAD_DIM = 128
ROPE_DIM = HEAD_DIM // 4
ROPE_HALF = ROPE_DIM // 2
ROPE_THETA = 500000.0
NORM_EPS = 1e-6
NEG_INF = -1e30

NSA_HEADS = 16
NSA_KV_HEADS = 4
NSA_GROUP = NSA_HEADS // NSA_KV_HEADS
CMP_BLOCK = 32
CMP_STRIDE = 16
SEL_BLOCK = 64
SEL_TOPN = 16
WINDOW = 512
FORCED_SCORE = 1e4

RWKV_WIDTH = 2048
RWKV_HEAD = 64
RWKV_HEADS = RWKV_WIDTH // RWKV_HEAD
LORA_DECAY = 96
LORA_AAA = 96
LORA_GATE = 256
GN_EPS = 64e-5

DSA_HEADS = 32
DSA_KV_HEADS = 4
DSA_GROUP = DSA_HEADS // DSA_KV_HEADS
IDX_HEADS = 32
IDX_DIM = 128
DSA_TOPK_MAX = 256

MEM_HEADS = 4
MEM_WIDTH = MEM_HEADS * HEAD_DIM

LANES = 128
VMEM_BIG = 56 * 1024 * 1024
VMEM_MID = 40 * 1024 * 1024
INT_MIN = -2 ** 31
MASK_BIG = 1e30
FLASH_TK = 512


def _params(sem, vmem=VMEM_MID):
    return pltpu.CompilerParams(dimension_semantics=sem, vmem_limit_bytes=vmem)


def _sigmoid(x):
    return 1.0 / (1.0 + jnp.exp(-x))


def _dot(a, b, preferred_element_type=F32, precision=None):
    return lax.dot_general(a, b, (((1,), (0,)), ((), ())), precision=precision,
                           preferred_element_type=preferred_element_type)


def _nt_dot(a, b):
    return lax.dot_general(a, b, (((1,), (1,)), ((), ())), preferred_element_type=F32)


def _split_dot(x, w_bf16):
    hi = x.astype(BF16)
    lo = (x - hi.astype(F32)).astype(BF16)
    return (_dot(hi, w_bf16, preferred_element_type=F32)
            + _dot(lo, w_bf16, preferred_element_type=F32))


def _rmsnorm_kernel(x_ref, g_ref, o_ref):
    x = x_ref[...]
    ms = jnp.mean(x * x, axis=-1, keepdims=True)
    o_ref[...] = (x * lax.rsqrt(ms + NORM_EPS) * g_ref[...]).astype(o_ref.dtype)


def rmsnorm(x2d, g, tm=256):
    m, d = x2d.shape
    return pl.pallas_call(
        _rmsnorm_kernel,
        grid=(m // tm,),
        in_specs=[pl.BlockSpec((tm, d), lambda i: (i, 0)),
                  pl.BlockSpec((1, d), lambda i: (0, 0))],
        out_specs=pl.BlockSpec((tm, d), lambda i: (i, 0)),
        out_shape=jax.ShapeDtypeStruct((m, d), BF16),
        compiler_params=_params(("parallel",)),
        name="rmsnorm",
    )(x2d, g.reshape(1, d))


def _mm_kernel(*refs, n_a, swiglu, has_res):
    a_refs = refs[:n_a]
    n_b = 2 if swiglu else n_a
    b_refs = refs[n_a:n_a + n_b]
    res_ref = refs[n_a + n_b] if has_res else None
    o_ref = refs[-1]
    if swiglu:
        a = a_refs[0][...]
        gate = _dot(a, b_refs[0][...].astype(BF16))
        y = gate * _sigmoid(gate) * _dot(a, b_refs[1][...].astype(BF16))
    else:
        y = _dot(a_refs[0][...], b_refs[0][...].astype(BF16))
        for a_ref, b_ref in zip(a_refs[1:], b_refs[1:]):
            y = y + _dot(a_ref[...], b_ref[...].astype(BF16))
    if has_res:
        y = y + res_ref[...]
    o_ref[...] = y.astype(o_ref.dtype)


def matmul(a, b, *, tm, tn, swiglu=False, col0=0, ncols=None, res=None, out_dtype=F32, name="matmul"):
    a_list = list(a) if isinstance(a, (list, tuple)) else [a]
    b_list = list(b) if isinstance(b, (list, tuple)) else [b]
    m = a_list[0].shape[0]
    n = b_list[0].shape[1] - col0 if ncols is None else ncols
    tm = min(tm, m)
    tn = min(tn, n)
    assert m % tm == 0 and n % tn == 0 and col0 % tn == 0
    off = col0 // tn
    in_specs = [pl.BlockSpec((tm, ai.shape[1]), lambda i, j: (i, 0)) for ai in a_list]
    in_specs += [pl.BlockSpec((bi.shape[0], tn), lambda i, j: (0, off + j)) for bi in b_list]
    args = a_list + b_list
    if res is not None:
        in_specs.append(pl.BlockSpec((tm, tn), lambda i, j: (i, j)))
        args.append(res)
    return pl.pallas_call(
        functools.partial(_mm_kernel, n_a=len(a_list), swiglu=swiglu, has_res=res is not None),
        grid=(m // tm, n // tn),
        in_specs=in_specs,
        out_specs=pl.BlockSpec((tm, tn), lambda i, j: (i, j)),
        out_shape=jax.ShapeDtypeStruct((m, n), out_dtype),
        compiler_params=_params(("parallel", "parallel"), VMEM_BIG),
        name=name,
    )(*args)


def _rope_tab_kernel(pos_ref, inv_ref, c_ref, sa_ref, sb_ref):
    ang = pos_ref[...].astype(F32) * inv_ref[...]
    c = jnp.cos(ang)
    s = jnp.sin(ang)
    lane = lax.broadcasted_iota(jnp.int32, ang.shape, 1)
    c_ref[...] = jnp.where(lane < ROPE_DIM, c, 1.0)
    sa_ref[...] = jnp.where(lane < ROPE_HALF, -s, 0.0)
    sb_ref[...] = jnp.where((lane >= ROPE_HALF) & (lane < ROPE_DIM), s, 0.0)


def rope_tables(pos_flat, tm=256):
    n = pos_flat.shape[0]
    tm = min(tm, n)
    inv = ROPE_THETA ** (-jnp.arange(0, ROPE_DIM, 2, dtype=F32) / ROPE_DIM)
    inv_row = jnp.concatenate([inv, inv, jnp.zeros((LANES - ROPE_DIM,), F32)]).reshape(1, LANES)
    pos_b = jnp.broadcast_to(pos_flat[:, None], (n, LANES))
    spec = pl.BlockSpec((tm, LANES), lambda i: (i, 0))
    shp = jax.ShapeDtypeStruct((n, LANES), F32)
    return pl.pallas_call(
        _rope_tab_kernel,
        grid=(n // tm,),
        in_specs=[spec, pl.BlockSpec((1, LANES), lambda i: (0, 0))],
        out_specs=[spec, spec, spec],
        out_shape=[shp, shp, shp],
        compiler_params=_params(("parallel",)),
        name="rope_tables",
    )(pos_b, inv_row)


def _head_norm(xh, g):
    ms = jnp.mean(xh * xh, axis=-1, keepdims=True)
    return xh * lax.rsqrt(ms + NORM_EPS) * g


def _rope(xh, c, sa, sb):
    return (xh * c + pltpu.roll(xh, LANES - ROPE_HALF, 1) * sa
            + pltpu.roll(xh, ROPE_HALF, 1) * sb)


def _norm_rope_kernel(*refs, heads, do_norm, do_rope, scale):
    x_ref = refs[0]
    pos = 1
    g = None
    if do_norm:
        g = refs[pos][...]
        pos += 1
    if do_rope:
        c, sa, sb = refs[pos][...], refs[pos + 1][...], refs[pos + 2][...]
        pos += 3
    o_ref = refs[pos]
    for h in range(heads):
        xh = x_ref[:, h * HEAD_DIM:(h + 1) * HEAD_DIM]
        if do_norm:
            xh = _head_norm(xh, g)
        if do_rope:
            xh = _rope(xh, c, sa, sb)
        if scale != 1.0:
            xh = xh * scale
        o_ref[:, h * HEAD_DIM:(h + 1) * HEAD_DIM] = xh.astype(o_ref.dtype)


def norm_rope(x2d, col_start, heads, gain, tabs, scale=1.0, tm=256, name="norm_rope"):
    m = x2d.shape[0]
    hb = heads
    while col_start % (hb * HEAD_DIM):
        hb //= 2
    w = hb * HEAD_DIM
    off = col_start // w
    tm = min(tm, m)
    in_specs = [pl.BlockSpec((tm, w), lambda i, j: (i, off + j))]
    args = [x2d]
    if gain is not None:
        in_specs.append(pl.BlockSpec((1, HEAD_DIM), lambda i, j: (0, 0)))
        args.append(gain.reshape(1, HEAD_DIM))
    if tabs is not None:
        in_specs += [pl.BlockSpec((tm, LANES), lambda i, j: (i, 0))] * 3
        args += list(tabs)
    return pl.pallas_call(
        functools.partial(_norm_rope_kernel, heads=hb, do_norm=gain is not None,
                          do_rope=tabs is not None, scale=scale),
        grid=(m // tm, heads // hb),
        in_specs=in_specs,
        out_specs=pl.BlockSpec((tm, w), lambda i, j: (i, j)),
        out_shape=jax.ShapeDtypeStruct((m, heads * HEAD_DIM), BF16),
        compiler_params=_params(("parallel", "parallel")),
        name=name,
    )(*args)


def _compress_kernel(*refs, is_key):
    if is_key:
        x_ref, w1_ref, w2_ref, pe_ref, g_ref, c_ref, sa_ref, sb_ref, o_ref = refs
    else:
        x_ref, w1_ref, w2_ref, pe_ref, o_ref = refs
    half = CMP_STRIDE * HEAD_DIM
    x = x_ref[0, 0]
    nc = x.shape[0]
    top = _dot(x, w1_ref[:half, :], preferred_element_type=F32)
    bot = _dot(x, w1_ref[half:, :], preferred_element_type=F32)
    pe_term = _dot(pe_ref[...], w1_ref[...], preferred_element_type=F32)[0:1, :]
    pre = top + pltpu.roll(bot, nc - 1, 0) + pe_term
    hid = jax.nn.gelu(pre)
    out = _dot(hid.astype(BF16), w2_ref[...], preferred_element_type=F32)
    if is_key:
        out = _rope(_head_norm(out, g_ref[...]), c_ref[0], sa_ref[0], sb_ref[0])
    o_ref[0, 0] = out.astype(o_ref.dtype)


def compress(xblk, w1, w2, pe, key_extras=None, name="compress"):
    b, h, nc, wdt = xblk.shape
    pe8 = jnp.broadcast_to(pe.reshape(1, CMP_BLOCK * HEAD_DIM), (8, CMP_BLOCK * HEAD_DIM)).astype(BF16)
    in_specs = [pl.BlockSpec((1, 1, nc, wdt), lambda i, j: (i, j, 0, 0)),
                pl.BlockSpec((CMP_BLOCK * HEAD_DIM, HEAD_DIM), lambda i, j: (0, 0)),
                pl.BlockSpec((HEAD_DIM, HEAD_DIM), lambda i, j: (0, 0)),
                pl.BlockSpec((8, CMP_BLOCK * HEAD_DIM), lambda i, j: (0, 0))]
    args = [xblk, w1.astype(BF16), w2.astype(BF16), pe8]
    if key_extras is not None:
        gain, tabs = key_extras
        in_specs.append(pl.BlockSpec((1, HEAD_DIM), lambda i, j: (0, 0)))
        in_specs += [pl.BlockSpec((1, nc, LANES), lambda i, j: (i, 0, 0))] * 3
        args += [gain.reshape(1, HEAD_DIM)] + [t.reshape(b, nc, LANES) for t in tabs]
    out_dtype = BF16
    return pl.pallas_call(
        functools.partial(_compress_kernel, is_key=key_extras is not None),
        grid=(b, h),
        in_specs=in_specs,
        out_specs=pl.BlockSpec((1, 1, nc, HEAD_DIM), lambda i, j: (i, j, 0, 0)),
        out_shape=jax.ShapeDtypeStruct((b, h, nc, HEAD_DIM), out_dtype),
        compiler_params=_params(("parallel", "parallel")),
        name=name,
    )(*args)


def _stack_heads(q_ref, group, tq):
    return jnp.concatenate([q_ref[0, :, g * HEAD_DIM:(g + 1) * HEAD_DIM] for g in range(group)], axis=0)


def _unstack_heads(o_ref, o_t, group, tq):
    o = o_t.T
    for g in range(group):
        o_ref[0, :, g * HEAD_DIM:(g + 1) * HEAD_DIM] = o[g * tq:(g + 1) * tq, :].astype(o_ref.dtype)


def _cmp_kernel(q_ref, kc_ref, vct_ref, mt_ref, o_ref, sel_ref, *, tq, n_sel, topn):
    group = NSA_GROUP
    t0 = pl.program_id(2) * tq
    q = _stack_heads(q_ref, group, tq)
    kc = kc_ref[0, 0]
    nc = kc.shape[0]
    s = _nt_dot(kc, q)
    c_idx = lax.broadcasted_iota(jnp.int32, (nc, tq), 0)
    t_idx = t0 + lax.broadcasted_iota(jnp.int32, (nc, tq), 1)
    ok1 = (CMP_STRIDE * c_idx + CMP_BLOCK - 1) <= t_idx
    bias1 = jnp.where(ok1, 0.0, NEG_INF)
    okf1 = jnp.where(ok1, 1.0, 0.0)
    bias = jnp.concatenate([bias1] * group, axis=1)
    okf = jnp.concatenate([okf1] * group, axis=1)
    s = s + bias
    m = jnp.max(s, axis=0, keepdims=True)
    e = jnp.exp(s - m) * okf
    l = jnp.sum(e, axis=0, keepdims=True)
    p = e * jnp.where(l > 0.0, 1.0 / l, 0.0)
    o_t = _dot(vct_ref[0, 0], p.astype(BF16), preferred_element_type=F32)
    _unstack_heads(o_ref, o_t, group, tq)

    psum = p[:, 0:tq]
    for g in range(1, group):
        psum = psum + p[:, g * tq:(g + 1) * tq]
    imp = _dot(mt_ref[...], psum, preferred_element_type=F32,
                  precision=lax.Precision.HIGHEST)
    j_idx = lax.broadcasted_iota(jnp.int32, (n_sel, tq), 0)
    jt = (t0 + lax.broadcasted_iota(jnp.int32, (n_sel, tq), 1)) // SEL_BLOCK
    forced = (j_idx == 0) | (j_idx == jt) | (j_idx == jt - 1)
    imp = jnp.where(forced, FORCED_SCORE, imp)
    imp = jnp.where(j_idx <= jt, imp, -jnp.inf)
    rank = jnp.zeros((n_sel, tq), F32)
    for jp in range(n_sel):
        row = imp[jp:jp + 1, :]
        before = (row > imp) | ((row == imp) & (j_idx > jp))
        rank = rank + jnp.where(before, 1.0, 0.0)
    bias_t = jnp.where(rank < topn, 0.0, -MASK_BIG)
    bias_t = jnp.concatenate([bias_t, jnp.zeros((LANES - n_sel, tq), F32)], axis=0)
    sel_ref[0, 0] = bias_t.T.astype(sel_ref.dtype)


def nsa_compressed(qn, kc, vct, b, t, tq=256):
    nc = kc.shape[2]
    n_sel = t // SEL_BLOCK
    assert n_sel <= LANES
    topn = min(SEL_TOPN, n_sel)
    tq = min(tq, t)
    cs = CMP_STRIDE * np.arange(nc)[:, None]
    ss = SEL_BLOCK * np.arange(n_sel)[None, :]
    ov = np.clip(np.minimum(cs + CMP_BLOCK, ss + SEL_BLOCK) - np.maximum(cs, ss), 0, None) / CMP_BLOCK
    ov[nc - 1, :] = 0.0
    mt = jnp.asarray(ov.T, dtype=F32)
    gw = NSA_GROUP * HEAD_DIM
    return pl.pallas_call(
        functools.partial(_cmp_kernel, tq=tq, n_sel=n_sel, topn=topn),
        grid=(b, NSA_KV_HEADS, t // tq),
        in_specs=[pl.BlockSpec((1, tq, gw), lambda i, h, j: (i, j, h)),
                  pl.BlockSpec((1, 1, nc, HEAD_DIM), lambda i, h, j: (i, h, 0, 0)),
                  pl.BlockSpec((1, 1, HEAD_DIM, nc), lambda i, h, j: (i, h, 0, 0)),
                  pl.BlockSpec((n_sel, nc), lambda i, h, j: (0, 0))],
        out_specs=[pl.BlockSpec((1, tq, gw), lambda i, h, j: (i, j, h)),
                   pl.BlockSpec((1, 1, tq, LANES), lambda i, h, j: (i, h, j, 0))],
        out_shape=[jax.ShapeDtypeStruct((b, t, NSA_HEADS * HEAD_DIM), F32),
                   jax.ShapeDtypeStruct((b, NSA_KV_HEADS, t, LANES), BF16)],
        compiler_params=_params(("parallel", "parallel", "parallel")),
        name="nsa_compressed",
    )(qn, kc, vct, mt)


def _flash_kernel(*refs, mode, group, slab, tq, tk):
    if mode == "window":
        q_ref, k_ref, vt_ref, o_ref, q_s, m_s, l_s, acc_s = refs
        xq = None
    else:
        q_ref, k_ref, vt_ref, xq_ref, xk_ref, o_ref, q_s, m_s, l_s, acc_s = refs
        xq = xq_ref[0, 0] if mode == "sel" else xq_ref[...]
    qi = pl.program_id(2)
    t0 = qi * tq
    q = _stack_heads(q_ref, group, tq)
    if xq is not None:
        q = jnp.concatenate([q, jnp.concatenate([xq] * group, axis=0)], axis=1)
    q_s[...] = q
    m_s[...] = jnp.full(m_s.shape, NEG_INF, F32)
    l_s[...] = jnp.zeros(l_s.shape, F32)
    acc_s[...] = jnp.zeros(acc_s.shape, F32)

    def tile(j, keep):
        k0 = pl.multiple_of(j * tk, tk)
        kt = k_ref[0, pl.ds(k0, tk), :]
        if mode == "sel":
            kt = jnp.concatenate([kt, xk_ref[pl.ds(k0, tk), :]], axis=1)
        elif mode == "dsa":
            kt = jnp.concatenate([kt, xk_ref[0, pl.ds(k0, tk), :]], axis=1)
        vt = vt_ref[0, 0, j]
        if keep is not None:
            kpos = k0 + lax.broadcasted_iota(jnp.int32, (tk, tq), 0)
            tpos = t0 + lax.broadcasted_iota(jnp.int32, (tk, tq), 1)
            ok = keep(kpos, tpos)
        for lo in range(0, group, slab):
            cols = slice(lo * tq, (lo + slab) * tq)
            s = _nt_dot(kt, q_s[cols, :])
            if keep is not None:
                s = jnp.concatenate([jnp.where(ok, s[:, g * tq:(g + 1) * tq], NEG_INF)
                                     for g in range(slab)], axis=1)
            m_old = m_s[:, cols]
            m_new = jnp.maximum(m_old, jnp.max(s, axis=0, keepdims=True))
            alpha = jnp.exp(m_old - m_new)
            p = jnp.exp(s - m_new)
            l_s[:, cols] = alpha * l_s[:, cols] + jnp.sum(p, axis=0, keepdims=True)
            acc_s[:, cols] = acc_s[:, cols] * alpha + _dot(vt, p.astype(BF16))
            m_s[:, cols] = m_new

    causal = lambda kpos, tpos: kpos <= tpos

    def plain(j, carry):
        tile(j, None)
        return carry

    if mode == "dsa":
        lax.fori_loop(0, (t0 + tq - 1) // tk + 1, plain, 0)
    elif mode == "sel":
        jd = (t0 + tq - 1) // tk
        lax.fori_loop(0, jd, plain, 0)
        tile(jd, causal)
    else:
        nw = WINDOW // tk

        @pl.when(qi >= nw)
        def _():
            tile(qi - nw, lambda kpos, tpos: tpos - kpos < WINDOW)

        for dist in range(nw - 1, 0, -1):
            @pl.when(qi >= dist)
            def _():
                tile(qi - dist, None)

        tile(qi, causal)

    _unstack_heads(o_ref, acc_s[...] * (1.0 / l_s[...]), group, tq)


def flash_masked(qn, kn, vt5, *, mode, group, b, t, tq, slab=None, extra=None, out_dtype=F32, name="flash"):
    kvh = kn.shape[2] // HEAD_DIM
    gw = group * HEAD_DIM
    tq = min(tq, t)
    ntk, tk = vt5.shape[2], vt5.shape[4]
    slab = group if slab is None else slab
    in_specs = [pl.BlockSpec((1, tq, gw), lambda i, h, qi: (i, qi, h)),
                pl.BlockSpec((1, t, HEAD_DIM), lambda i, h, qi: (i, 0, h)),
                pl.BlockSpec((1, 1, ntk, HEAD_DIM, tk), lambda i, h, qi: (i, h, 0, 0, 0))]
    args = [qn, kn, vt5]
    kdim = HEAD_DIM
    if mode == "sel":
        assert tk % tq == 0
        et = np.zeros((t, LANES), np.float32)
        et[np.arange(t), np.arange(t) // SEL_BLOCK] = 1.0
        in_specs += [pl.BlockSpec((1, 1, tq, LANES), lambda i, h, qi: (i, h, qi, 0)),
                     pl.BlockSpec((t, LANES), lambda i, h, qi: (0, 0))]
        args += [extra, jnp.asarray(et, dtype=BF16)]
        kdim += LANES
    elif mode == "dsa":
        assert tq == LANES
        in_specs += [pl.BlockSpec((tq, LANES), lambda i, h, qi: (0, 0)),
                     pl.BlockSpec((1, t, tq), lambda i, h, qi: (i, 0, qi))]
        args += [jnp.eye(tq, dtype=BF16), extra]
        kdim += LANES
    else:
        assert tk == tq and WINDOW % tk == 0
    return pl.pallas_call(
        functools.partial(_flash_kernel, mode=mode, group=group, slab=slab, tq=tq, tk=tk),
        grid=(b, kvh, t // tq),
        in_specs=in_specs,
        out_specs=pl.BlockSpec((1, tq, gw), lambda i, h, qi: (i, qi, h)),
        out_shape=jax.ShapeDtypeStruct((b, t, kvh * gw), out_dtype),
        scratch_shapes=[pltpu.VMEM((group * tq, kdim), BF16),
                        pltpu.VMEM((1, group * tq), F32),
                        pltpu.VMEM((1, group * tq), F32),
                        pltpu.VMEM((HEAD_DIM, group * tq), F32)],
        compiler_params=_params(("parallel", "parallel", "parallel")),
        name=name,
    )(*args)


def _nsa_merge_kernel(oc_ref, os_ref, ow_ref, g_ref, e_ref, o_ref):
    gate = _sigmoid(g_ref[...])
    out = None
    for j, src in enumerate((oc_ref, os_ref, ow_ref)):
        gj = _split_dot(gate, e_ref[j])
        term = gj * src[...]
        out = term if out is None else out + term
    o_ref[...] = out.astype(o_ref.dtype)


def nsa_merge(oc, os_, ow, pc, tm=256):
    m, w = oc.shape
    tm = min(tm, m)
    e = np.zeros((3, LANES, w), np.float32)
    for h in range(NSA_HEADS):
        for j in range(3):
            e[j, h * 3 + j, h * HEAD_DIM:(h + 1) * HEAD_DIM] = 1.0
    spec = pl.BlockSpec((tm, w), lambda i: (i, 0))
    return pl.pallas_call(
        _nsa_merge_kernel,
        grid=(m // tm,),
        in_specs=[spec, spec, spec,
                  pl.BlockSpec((tm, LANES), lambda i: (i, 0)),
                  pl.BlockSpec((3, LANES, w), lambda i: (0, 0, 0))],
        out_specs=spec,
        out_shape=jax.ShapeDtypeStruct((m, w), BF16),
        compiler_params=_params(("parallel",)),
        name="nsa_merge",
    )(oc, os_, ow, pc, jnp.asarray(e, dtype=BF16))


def _seg_sum(x, bd):
    outs = []
    for s in range(x.shape[1] // LANES):
        outs.append(_split_dot(x[:, s * LANES:(s + 1) * LANES], bd))
    return jnp.concatenate(outs, axis=1)


def _shifted(x, prev_row, first):
    rolled = pltpu.roll(x, 1, 0)
    row0 = jnp.where(first, 0.0, prev_row)
    ridx = lax.broadcasted_iota(jnp.int32, x.shape, 0)
    return jnp.where(ridx == 0, row0, rolled)


def _rwkv_prep_kernel(pb_ref, pbp_ref, pc_ref, pcp_ref, mub_ref, muc_ref, w0_ref, w2_ref, a0_ref, a2_ref,
                      g2_ref, kkg_ref, ka_ref, bd_ref,
                      r_ref, w_ref, k_ref, v_ref, kk_ref, b_ref, g_ref, *, tiles_per_seq):
    first = (pl.program_id(0) % tiles_per_seq) == 0
    w = RWKV_WIDTH
    xb = pb_ref[...]
    xb = xb + (_shifted(xb, pbp_ref[7:8, :], first) - xb) * mub_ref[...]
    xc = pc_ref[...]
    xc = xc + (_shifted(xc, pcp_ref[7:8, :], first) - xc) * muc_ref[...]
    r, k, v = xb[:, :w], xb[:, w:2 * w], xb[:, 2 * w:]
    wd, ad, gd = xc[:, LANES:2 * LANES], xc[:, 2 * LANES:3 * LANES], xc[:, 3 * LANES:]
    z = w0_ref[...] + _dot(jnp.tanh(wd).astype(BF16), w2_ref[...], preferred_element_type=F32)
    nz = -z
    softplus = jnp.maximum(nz, 0.0) + jnp.log(1.0 + jnp.exp(-jnp.abs(nz)))
    w_log = -softplus - 0.5
    a = _sigmoid(a0_ref[...] + _dot(ad.astype(BF16), a2_ref[...], preferred_element_type=F32))
    g = _dot(_sigmoid(gd).astype(BF16), g2_ref[...], preferred_element_type=F32)
    kkv = k * kkg_ref[...]
    norm = jnp.sqrt(_seg_sum(kkv * kkv, bd_ref[...]))
    kkv = kkv / jnp.maximum(norm, 1e-12)
    r_ref[...] = r
    w_ref[...] = jnp.exp(-jnp.exp(w_log))
    k_ref[...] = k * (1.0 + (a - 1.0) * ka_ref[...])
    v_ref[...] = v
    kk_ref[...] = kkv
    b_ref[...] = kkv * a
    g_ref[...] = g


def _pad_rows(wm, rows):
    return jnp.pad(wm, ((0, rows - wm.shape[0]), (0, 0)))


def _head_block_ones():
    bd = np.zeros((LANES, LANES), np.float32)
    bd[:RWKV_HEAD, :RWKV_HEAD] = 1.0
    bd[RWKV_HEAD:, RWKV_HEAD:] = 1.0
    return jnp.asarray(bd, dtype=BF16)


def rwkv_prep(pb, pc, t, mu, w0, w2, a0, a2, g2, kk_gain, k_a, tm=128):
    m = pb.shape[0]
    w = RWKV_WIDTH
    tm = min(tm, t)
    mu_r, mu_k, mu_v, mu_wd, mu_ad, mu_gd = jnp.split(
        mu, [int(x) for x in np.cumsum([w, w, w, LORA_DECAY, LORA_AAA])])
    mub = jnp.concatenate([mu_r, mu_k, mu_v]).reshape(1, 3 * w)
    pad = lambda z: jnp.pad(z, (0, LANES - z.shape[0]))
    muc = jnp.concatenate([jnp.zeros((LANES,), F32), pad(mu_wd), pad(mu_ad), mu_gd]).reshape(1, -1)
    cw = pc.shape[1]
    row = lambda z: z.reshape(1, w)
    full = lambda shape: pl.BlockSpec(shape, lambda i: (0,) * len(shape))
    tile = lambda width: pl.BlockSpec((tm, width), lambda i: (i, 0))
    prev = lambda width: pl.BlockSpec((8, width), lambda i: (jnp.maximum(i * (tm // 8) - 1, 0), 0))
    out_spec = tile(w)
    shp = jax.ShapeDtypeStruct((m, w), F32)
    return pl.pallas_call(
        functools.partial(_rwkv_prep_kernel, tiles_per_seq=t // tm),
        grid=(m // tm,),
        in_specs=[tile(3 * w), prev(3 * w), tile(cw), prev(cw), full((1, 3 * w)), full((1, cw)),
                  full((1, w)), full((LANES, w)), full((1, w)), full((LANES, w)), full((LORA_GATE, w)),
                  full((1, w)), full((1, w)), full((LANES, LANES))],
        out_specs=[out_spec] * 7,
        out_shape=[shp] * 7,
        compiler_params=_params(("parallel",), VMEM_BIG),
        name="rwkv_prep",
    )(pb, pb, pc, pc, mub, muc, row(w0), _pad_rows(w2, LANES).astype(BF16), row(a0),
      _pad_rows(a2, LANES).astype(BF16), g2.astype(BF16), row(kk_gain), row(k_a), _head_block_ones())


def _rwkv_scan_kernel(w_ref, kk_ref, b_ref, k_ref, r_ref, v_ref, y_ref, s_ref, *, tb, rows):
    ig = pl.program_id(1)

    @pl.when(pl.program_id(0) == 0)
    def _():
        s_ref[ig] = jnp.zeros(s_ref.shape[1:], F32)

    def step(t, s):
        kk = kk_ref[t]
        sa = -jnp.sum(s * kk[None], axis=1)
        vrow = v_ref[t, 0]
        s = (s * w_ref[t][None] + sa[:, None, :] * b_ref[t][None]
             + vrow[:, None, :] * k_ref[t][None])
        y_ref[t, 0] = jnp.sum(s * r_ref[t][None], axis=1)
        return s

    s_ref[ig] = lax.fori_loop(0, tb, step, s_ref[ig], unroll=8)


def rwkv_scan(w, kk, bb, k, r, v, tb=64, rows=4):
    t, n, lanes = w.shape
    groups = n // rows
    tb = min(tb, t)
    op = pl.BlockSpec((tb, n, lanes), lambda i, j: (i, 0, 0))
    vy = pl.BlockSpec((tb, 1, rows, lanes), lambda i, j: (i, j, 0, 0))
    return pl.pallas_call(
        functools.partial(_rwkv_scan_kernel, tb=tb, rows=rows),
        grid=(t // tb, groups),
        in_specs=[op, op, op, op, op, vy],
        out_specs=vy,
        out_shape=jax.ShapeDtypeStruct((t, groups, rows, lanes), F32),
        scratch_shapes=[pltpu.VMEM((groups, rows, n, lanes), F32)],
        compiler_params=_params(("arbitrary", "arbitrary"), VMEM_BIG),
        name="rwkv_scan",
    )(w, kk, bb, k, r, v)


def _rwkv_post_kernel(y_ref, r_ref, k_ref, v_ref, g_ref, rk_ref, gg_ref, gb_ref, bd_ref, o_ref):
    bd = bd_ref[...]
    y = y_ref[...]
    inv_n = 1.0 / RWKV_HEAD
    mean = _seg_sum(y, bd) * inv_n
    d = y - mean
    var = _seg_sum(d * d, bd) * inv_n
    yn = d * lax.rsqrt(var + GN_EPS) * gg_ref[...] + gb_ref[...]
    bonus = _seg_sum(r_ref[...] * k_ref[...] * rk_ref[...], bd) * v_ref[...]
    o_ref[...] = ((yn + bonus) * g_ref[...]).astype(o_ref.dtype)


def rwkv_post(y, r, k, v, g, r_k, gn_g, gn_b, tm=256):
    m, w = y.shape
    tm = min(tm, m)
    tile = pl.BlockSpec((tm, w), lambda i: (i, 0))
    rowspec = pl.BlockSpec((1, w), lambda i: (0, 0))
    return pl.pallas_call(
        _rwkv_post_kernel,
        grid=(m // tm,),
        in_specs=[tile] * 5 + [rowspec] * 3 + [pl.BlockSpec((LANES, LANES), lambda i: (0, 0))],
        out_specs=tile,
        out_shape=jax.ShapeDtypeStruct((m, w), BF16),
        compiler_params=_params(("parallel",)),
        name="rwkv_post",
    )(y, r, k, v, g, r_k.reshape(1, w), gn_g.reshape(1, w), gn_b.reshape(1, w), _head_block_ones())


def rwkv7(pb, pc, b, t, mu, w0, w2, a0, a2, g2, kk_gain, k_a, r_k, gn_g, gn_b):
    r, w, k, v, kk, bb, g = rwkv_prep(pb, pc, t, mu, w0, w2, a0, a2, g2, kk_gain, k_a)
    rows = 4

    def to_scan(z):
        return z.reshape(b, t, RWKV_HEADS, RWKV_HEAD).transpose(1, 3, 0, 2).reshape(t, RWKV_HEAD, b * RWKV_HEADS)

    v_s = to_scan(v).reshape(t, RWKV_HEAD // rows, rows, b * RWKV_HEADS)
    y = rwkv_scan(to_scan(w), to_scan(kk), to_scan(bb), to_scan(k), to_scan(r), v_s, rows=rows)
    y = y.reshape(t, RWKV_HEAD, b, RWKV_HEADS).transpose(2, 0, 3, 1).reshape(b * t, RWKV_WIDTH)
    return rwkv_post(y, r, k, v, g, r_k, gn_g, gn_b)


def _dsa_select_kernel(qi_ref, ki_ref, wt_ref, bias_ref, q_s, key_s, *, t, tq, kc, topk):
    t0 = pl.program_id(1) * tq
    nchunk = (t0 + tq - 1) // kc + 1
    for h in range(IDX_HEADS):
        q_s[h * tq:(h + 1) * tq, :] = qi_ref[0, :, h * IDX_DIM:(h + 1) * IDX_DIM]
    wt = wt_ref[0]

    def chunk(c, carry):
        k0 = pl.multiple_of(c * kc, kc)
        lg = _nt_dot(ki_ref[0, pl.ds(k0, kc), :], q_s[...])
        acc = jnp.zeros((kc, tq), F32)
        for h in range(IDX_HEADS):
            acc = acc + jnp.maximum(lg[:, h * tq:(h + 1) * tq], 0.0) * wt[h:h + 1, :]
        spos = k0 + lax.broadcasted_iota(jnp.int32, (kc, tq), 0)
        tpos = t0 + lax.broadcasted_iota(jnp.int32, (kc, tq), 1)
        score = jnp.where(spos <= tpos, acc + 0.0, -jnp.inf)
        bits = pltpu.bitcast(score, jnp.int32)
        key_s[pl.ds(k0, kc), :] = jnp.where(bits < 0, bits ^ jnp.int32(0x7FFFFFFF), bits)
        return carry

    lax.fori_loop(0, nchunk, chunk, 0)
    kf = jnp.float32(topk)
    part = 64

    def count(pred):
        def body(c, acc):
            k0 = pl.multiple_of(c * kc, kc)
            ind = jnp.where(pred(key_s[pl.ds(k0, kc), :], k0), 1.0, 0.0)
            return acc + jnp.sum(ind.reshape(kc // part, part, tq), axis=0)
        acc = lax.fori_loop(0, nchunk, body, jnp.zeros((part, tq), F32))
        return jnp.sum(acc, axis=0, keepdims=True)

    def value_bit(it, tau_u):
        cand_u = tau_u | lax.shift_left(jnp.int32(1), 31 - it)
        cand_s = cand_u ^ jnp.int32(INT_MIN)
        return jnp.where(count(lambda blk, k0: blk >= cand_s) >= kf, cand_u, tau_u)

    tau_u = lax.fori_loop(0, 32, value_bit, jnp.zeros((1, tq), jnp.int32))
    tau = tau_u ^ jnp.int32(INT_MIN)
    need = kf - count(lambda blk, k0: blk > tau)
    n_eq = count(lambda blk, k0: blk == tau)
    nbits = int(np.log2(t))

    def tie_search():
        def index_bit(it, jj):
            cand = jj | lax.shift_left(jnp.int32(1), nbits - 1 - it)

            def pred(blk, k0):
                spos = k0 + lax.broadcasted_iota(jnp.int32, (kc, tq), 0)
                return (blk == tau) & (spos < cand)
            return jnp.where(count(pred) < need, cand, jj)
        return lax.fori_loop(0, nbits, index_bit, jnp.zeros((1, tq), jnp.int32))

    jj = lax.cond(jnp.max(n_eq - need) > 0.0, tie_search, lambda: jnp.full((1, tq), t, jnp.int32))

    def emit(c, carry):
        k0 = pl.multiple_of(c * kc, kc)
        blk = key_s[pl.ds(k0, kc), :]
        spos = k0 + lax.broadcasted_iota(jnp.int32, (kc, tq), 0)
        tpos = t0 + lax.broadcasted_iota(jnp.int32, (kc, tq), 1)
        keep = ((blk > tau) | ((blk == tau) & (spos <= jj))) & (spos <= tpos)
        bias_ref[0, pl.ds(k0, kc), :] = jnp.where(keep, 0.0, -MASK_BIG).astype(bias_ref.dtype)
        return carry

    lax.fori_loop(0, nchunk, emit, 0)

    def fill(c, carry):
        k0 = pl.multiple_of(c * kc, kc)
        bias_ref[0, pl.ds(k0, kc), :] = jnp.full((kc, tq), -MASK_BIG, bias_ref.dtype)
        return carry

    lax.fori_loop(nchunk, t // kc, fill, 0)


def dsa_select(qi, ki, wt, b, t, tq=128, kc=512):
    topk = min(DSA_TOPK_MAX, t // 4)
    kc = min(kc, t)
    assert topk <= kc and t % kc == 0
    return pl.pallas_call(
        functools.partial(_dsa_select_kernel, t=t, tq=tq, kc=kc, topk=topk),
        grid=(b, t // tq),
        in_specs=[pl.BlockSpec((1, tq, IDX_HEADS * IDX_DIM), lambda i, j: (i, j, 0)),
                  pl.BlockSpec((1, t, IDX_DIM), lambda i, j: (i, 0, 0)),
                  pl.BlockSpec((1, IDX_HEADS, tq), lambda i, j: (i, 0, j))],
        out_specs=pl.BlockSpec((1, t, tq), lambda i, j: (i, 0, j)),
        out_shape=jax.ShapeDtypeStruct((b, t, t), BF16),
        scratch_shapes=[pltpu.VMEM((IDX_HEADS * tq, IDX_DIM), BF16),
                        pltpu.VMEM((t, tq), jnp.int32)],
        compiler_params=_params(("parallel", "parallel"), VMEM_BIG),
        name="dsa_select",
    )(qi, ki, wt)


def _xattn_kernel(q_ref, g_ref, mk_ref, mv_ref, o_ref):
    scale = HEAD_DIM ** -0.5
    for h in range(MEM_HEADS):
        sl = slice(h * HEAD_DIM, (h + 1) * HEAD_DIM)
        qh = (_head_norm(q_ref[:, sl], g_ref[...]) * scale).astype(BF16)
        s = _nt_dot(qh, mk_ref[0, :, sl])
        m = jnp.max(s, axis=-1, keepdims=True)
        e = jnp.exp(s - m)
        p = e / jnp.sum(e, axis=-1, keepdims=True)
        o_ref[:, sl] = _dot(p.astype(BF16), mv_ref[0, :, sl],
                               preferred_element_type=F32).astype(o_ref.dtype)


def mem_xattn(qm, gain, mk, mv, b, t, tq=256):
    tq = min(tq, t)
    nt = t // tq
    mtok = mk.shape[1]
    return pl.pallas_call(
        _xattn_kernel,
        grid=(b, nt),
        in_specs=[pl.BlockSpec((tq, MEM_WIDTH), lambda i, j: (i * nt + j, 0)),
                  pl.BlockSpec((1, HEAD_DIM), lambda i, j: (0, 0)),
                  pl.BlockSpec((1, mtok, MEM_WIDTH), lambda i, j: (i, 0, 0)),
                  pl.BlockSpec((1, mtok, MEM_WIDTH), lambda i, j: (i, 0, 0))],
        out_specs=pl.BlockSpec((tq, MEM_WIDTH), lambda i, j: (i * nt + j, 0)),
        out_shape=jax.ShapeDtypeStruct((b * t, MEM_WIDTH), BF16),
        compiler_params=_params(("parallel", "parallel")),
        name="mem_xattn",
    )(qm, gain.reshape(1, HEAD_DIM), mk, mv)


def _kv_transposed(x2d, lo, b, t, heads, tk=FLASH_TK):
    tk = min(tk, t)
    v = x2d[:, lo:lo + heads * HEAD_DIM].astype(BF16)
    return v.reshape(b, t // tk, tk, heads, HEAD_DIM).transpose(0, 3, 1, 4, 2)


def nsa_mixer(pa, pc, b, t, tabs, cmp_tabs, q_norm, kc_norm, ks_norm, kw_norm, pe_k, pe_v,
              ck_w1, ck_w2, cv_w1, cv_w2):
    kvw = NSA_KV_HEADS * HEAD_DIM
    scale = HEAD_DIM ** -0.5
    qn = norm_rope(pa, 0, NSA_HEADS, q_norm, tabs, scale=scale, name="nsa_q_prep")
    ksn = norm_rope(pa, 6 * kvw, NSA_KV_HEADS, ks_norm, tabs, name="nsa_ks_prep").reshape(b, t, kvw)
    kwn = norm_rope(pa, 8 * kvw, NSA_KV_HEADS, kw_norm, tabs, name="nsa_kw_prep").reshape(b, t, kvw)
    vst = _kv_transposed(pa, 7 * kvw, b, t, NSA_KV_HEADS)
    vwt = _kv_transposed(pa, 9 * kvw, b, t, NSA_KV_HEADS)
    qn = qn.reshape(b, t, NSA_HEADS * HEAD_DIM)

    nc = t // CMP_STRIDE

    def chunked(lo):
        z = pa[:, lo:lo + kvw].astype(BF16).reshape(b, nc, CMP_STRIDE, NSA_KV_HEADS, HEAD_DIM)
        return z.transpose(0, 3, 1, 2, 4).reshape(b, NSA_KV_HEADS, nc, CMP_STRIDE * HEAD_DIM)

    kc = compress(chunked(4 * kvw), ck_w1, ck_w2, pe_k, key_extras=(kc_norm, cmp_tabs), name="compress_k")
    vc = compress(chunked(5 * kvw), cv_w1, cv_w2, pe_v, name="compress_v")
    vct = vc.transpose(0, 1, 3, 2)
    o_c, sel_bias = nsa_compressed(qn, kc, vct, b, t)
    o_s = flash_masked(qn, ksn, vst, mode="sel", group=NSA_GROUP, b=b, t=t, tq=256,
                       extra=sel_bias, name="nsa_selected")
    o_w = flash_masked(qn, kwn, vwt, mode="window", group=NSA_GROUP, b=b, t=t, tq=FLASH_TK,
                       name="nsa_window")
    m = b * t
    w = NSA_HEADS * HEAD_DIM
    return nsa_merge(o_c.reshape(m, w), o_s.reshape(m, w), o_w.reshape(m, w), pc)


def dsa_mixer_core(p1, p2, b, t, tabs, q_norm, k_norm, ki_norm):
    scale = HEAD_DIM ** -0.5
    qw = DSA_HEADS * HEAD_DIM
    kvw = DSA_KV_HEADS * HEAD_DIM
    qn = norm_rope(p1, 0, DSA_HEADS, q_norm, tabs, scale=scale, name="dsa_q_prep").reshape(b, t, qw)
    kn = norm_rope(p1, qw, DSA_KV_HEADS, k_norm, tabs, name="dsa_k_prep").reshape(b, t, kvw)
    vt = _kv_transposed(p1, qw + kvw, b, t, DSA_KV_HEADS)
    qi = norm_rope(p1, qw + 2 * kvw, IDX_HEADS, None, tabs, name="dsa_qi_prep")
    qi = qi.reshape(b, t, IDX_HEADS * IDX_DIM)
    ki = norm_rope(p2, 0, 1, ki_norm, tabs, name="dsa_ki_prep").reshape(b, t, IDX_DIM)
    wscale = IDX_HEADS ** -0.5 * IDX_DIM ** -0.5
    wt = (p2[:, IDX_DIM:IDX_DIM + IDX_HEADS] * wscale).reshape(b, t, IDX_HEADS).transpose(0, 2, 1)
    key_bias = dsa_select(qi, ki, wt, b, t)
    o = flash_masked(qn, kn, vt, mode="dsa", group=DSA_GROUP, b=b, t=t, tq=LANES,
                     extra=key_bias, out_dtype=BF16, name="dsa_attention")
    return o.reshape(b * t, qw)


def kernel(x, mem, positions, mem_norm, mem_w_kv, mem_k_norm, l0_mix_norm, l0_w_in, l0_w_out, nsa_q_norm, nsa_kc_norm, nsa_ks_norm, nsa_kw_norm, nsa_pe_k, nsa_pe_v, nsa_ck_w1, nsa_ck_w2, nsa_cv_w1, nsa_cv_w2, rwkv_mu, rwkv_w0, rwkv_w2, rwkv_a0, rwkv_a2, rwkv_g2, rwkv_kk, rwkv_ka, rwkv_rk, rwkv_gn_g, rwkv_gn_b, l0_xattn_norm, l0_mem_wq, l0_mem_q_norm, l0_mem_wo, l0_ffn_norm, l0_w1, l0_w3, l0_w2, l1_mix_norm, l1_w_in, l1_w_out, dsa_q_norm, dsa_k_norm, dsa_ki_norm, l1_xattn_norm, l1_mem_wq, l1_mem_q_norm, l1_mem_wo, l1_ffn_norm, l1_w1, l1_w3, l1_w2):
    b, t, d = x.shape
    m = b * t
    bf = lambda z: z.astype(BF16)
    x2 = x.reshape(m, d)

    tabs = rope_tables(positions.reshape(m))
    nc = t // CMP_STRIDE
    cmp_pos = positions[:, CMP_BLOCK - 1::CMP_STRIDE]
    cmp_pos = jnp.concatenate([cmp_pos, cmp_pos[:, -1:]], axis=1)
    cmp_tabs = rope_tables(cmp_pos.reshape(b * nc))

    mtok = mem.shape[1]
    memn = rmsnorm(mem.reshape(b * mtok, d), mem_norm)
    mkv = matmul(memn, mem_w_kv, tm=512, tn=512, name="mem_kv_proj")
    mk = norm_rope(mkv, 0, MEM_HEADS, mem_k_norm, None, name="mem_k_norm").reshape(b, mtok, MEM_WIDTH)
    mv = bf(mkv[:, MEM_WIDTH:]).reshape(b, mtok, MEM_WIDTH)

    def tail(x2, xn, wq, qn, wo, fn, w1, w3, w2):
        h = rmsnorm(x2, xn)
        qm = matmul(h, wq, tm=1024, tn=512, name="xattn_q_proj")
        o = mem_xattn(qm, qn, mk, mv, b, t)
        x2 = matmul(o, wo, tm=1024, tn=512, res=x2, name="xattn_out_proj")
        h = rmsnorm(x2, fn)
        u = matmul(h, [w1, w3], swiglu=True, tm=1024, tn=256, out_dtype=BF16, name="ffn_up")
        return matmul(u, bf(w2), tm=512, tn=512, res=x2, name="ffn_down")

    nsa_cols = NSA_HEADS * HEAD_DIM + 6 * NSA_KV_HEADS * HEAD_DIM
    gate_cols = 3 * NSA_HEADS
    rw0 = nsa_cols + gate_cols
    rw1 = rw0 + 3 * RWKV_WIDTH
    pad_cols = lambda wm: jnp.pad(wm, ((0, 0), (0, LANES - wm.shape[1])))
    w_c = jnp.concatenate([
        pad_cols(l0_w_in[:, nsa_cols:rw0]),
        pad_cols(l0_w_in[:, rw1:rw1 + LORA_DECAY]),
        pad_cols(l0_w_in[:, rw1 + LORA_DECAY:rw1 + LORA_DECAY + LORA_AAA]),
        l0_w_in[:, rw1 + LORA_DECAY + LORA_AAA:]], axis=1)
    h = rmsnorm(x2, l0_mix_norm)
    pa = matmul(h, l0_w_in, ncols=nsa_cols, tm=1024, tn=512, name="l0_proj_nsa")
    pb = matmul(h, bf(l0_w_in[:, rw0:rw1]), tm=1024, tn=512, name="l0_proj_rwkv")
    pc = matmul(h, bf(w_c), tm=1024, tn=w_c.shape[1], name="l0_proj_small")
    o_a = nsa_mixer(pa, pc, b, t, tabs, cmp_tabs, nsa_q_norm, nsa_kc_norm, nsa_ks_norm, nsa_kw_norm,
                    nsa_pe_k, nsa_pe_v, nsa_ck_w1, nsa_ck_w2, nsa_cv_w1, nsa_cv_w2)
    o_b = rwkv7(pb, pc, b, t, rwkv_mu, rwkv_w0, rwkv_w2, rwkv_a0, rwkv_a2, rwkv_g2, rwkv_kk, rwkv_ka,
                rwkv_rk, rwkv_gn_g, rwkv_gn_b)
    nsa_w = NSA_HEADS * HEAD_DIM
    x2 = matmul([o_a, o_b], [l0_w_out[:nsa_w], l0_w_out[nsa_w:]], tm=1024, tn=512, res=x2,
                name="l0_out_proj")
    x2 = tail(x2, l0_xattn_norm, l0_mem_wq, l0_mem_q_norm, l0_mem_wo, l0_ffn_norm, l0_w1, l0_w3, l0_w2)

    main_cols = DSA_HEADS * HEAD_DIM + 2 * DSA_KV_HEADS * HEAD_DIM + IDX_HEADS * IDX_DIM
    w_s = jnp.concatenate([l1_w_in[:, main_cols:main_cols + IDX_DIM],
                           pad_cols(l1_w_in[:, main_cols + IDX_DIM:])], axis=1)
    h = rmsnorm(x2, l1_mix_norm)
    p1 = matmul(h, l1_w_in, ncols=main_cols, tm=1024, tn=512, name="l1_proj_main")
    p2 = matmul(h, bf(w_s), tm=1024, tn=w_s.shape[1], name="l1_proj_small")
    o = dsa_mixer_core(p1, p2, b, t, tabs, dsa_q_norm, dsa_k_norm, dsa_ki_norm)
    x2 = matmul(o, l1_w_out, tm=1024, tn=512, res=x2, name="l1_out_proj")
    x2 = tail(x2, l1_xattn_norm, l1_mem_wq, l1_mem_q_norm, l1_mem_wo, l1_ffn_norm, l1_w1, l1_w3, l1_w2)
    return x2.reshape(b, t, d)
```

```python
import functools

import numpy as np
import jax
import jax.numpy as jnp
from jax import lax
from jax.experimental import pallas as pl
from jax.experimental.pallas import tpu as pltpu

F32 = jnp.float32
BF16 = jnp.bfloat16

HEAD_DIM = 128
ROPE_DIM = HEAD_DIM // 4
ROPE_HALF = ROPE_DIM // 2
ROPE_THETA = 500000.0
NORM_EPS = 1e-6
NEG_INF = -1e30

NSA_HEADS = 16
NSA_KV_HEADS = 4
NSA_GROUP = NSA_HEADS // NSA_KV_HEADS
CMP_BLOCK = 32
CMP_STRIDE = 16
SEL_BLOCK = 64
SEL_TOPN = 16
WINDOW = 512
FORCED_SCORE = 1e4

RWKV_WIDTH = 2048
RWKV_HEAD = 64
RWKV_HEADS = RWKV_WIDTH // RWKV_HEAD
LORA_DECAY = 96
LORA_AAA = 96
LORA_GATE = 256
GN_EPS = 64e-5

DSA_HEADS = 32
DSA_KV_HEADS = 4
DSA_GROUP = DSA_HEADS // DSA_KV_HEADS
IDX_HEADS = 32
IDX_DIM = 128
DSA_TOPK_MAX = 256

MEM_HEADS = 4
MEM_WIDTH = MEM_HEADS * HEAD_DIM

LANES = 128
VMEM_BIG = 56 * 1024 * 1024
VMEM_MID = 40 * 1024 * 1024
INT_MIN = -2 ** 31
MASK_BIG = 1e30
FLASH_TK = 512


def _params(sem, vmem=VMEM_MID):
    return pltpu.CompilerParams(dimension_semantics=sem, vmem_limit_bytes=vmem)


def _sigmoid(x):
    return 1.0 / (1.0 + jnp.exp(-x))


def _dot(a, b, preferred_element_type=F32, precision=None):
    return lax.dot_general(a, b, (((1,), (0,)), ((), ())), precision=precision,
                           preferred_element_type=preferred_element_type)


def _nt_dot(a, b):
    return lax.dot_general(a, b, (((1,), (1,)), ((), ())), preferred_element_type=F32)


def _split_dot(x, w_bf16):
    hi = x.astype(BF16)
    lo = (x - hi.astype(F32)).astype(BF16)
    return (_dot(hi, w_bf16, preferred_element_type=F32)
            + _dot(lo, w_bf16, preferred_element_type=F32))


def _rmsnorm_kernel(x_ref, g_ref, o_ref):
    x = x_ref[...]
    ms = jnp.mean(x * x, axis=-1, keepdims=True)
    o_ref[...] = (x * lax.rsqrt(ms + NORM_EPS) * g_ref[...]).astype(o_ref.dtype)


def rmsnorm(x2d, g, tm=256):
    m, d = x2d.shape
    return pl.pallas_call(
        _rmsnorm_kernel,
        grid=(m // tm,),
        in_specs=[pl.BlockSpec((tm, d), lambda i: (i, 0)),
                  pl.BlockSpec((1, d), lambda i: (0, 0))],
        out_specs=pl.BlockSpec((tm, d), lambda i: (i, 0)),
        out_shape=jax.ShapeDtypeStruct((m, d), BF16),
        compiler_params=_params(("parallel",)),
        name="rmsnorm",
    )(x2d, g.reshape(1, d))


def _mm_kernel(*refs, n_a, swiglu, has_res):
    a_refs = refs[:n_a]
    n_b = 2 if swiglu else n_a
    b_refs = refs[n_a:n_a + n_b]
    res_ref = refs[n_a + n_b] if has_res else None
    o_ref = refs[-1]
    if swiglu:
        a = a_refs[0][...]
        gate = _dot(a, b_refs[0][...].astype(BF16))
        y = gate * _sigmoid(gate) * _dot(a, b_refs[1][...].astype(BF16))
    else:
        y = _dot(a_refs[0][...], b_refs[0][...].astype(BF16))
        for a_ref, b_ref in zip(a_refs[1:], b_refs[1:]):
            y = y + _dot(a_ref[...], b_ref[...].astype(BF16))
    if has_res:
        y = y + res_ref[...]
    o_ref[...] = y.astype(o_ref.dtype)


def matmul(a, b, *, tm, tn, swiglu=False, col0=0, ncols=None, res=None, out_dtype=F32, name="matmul"):
    a_list = list(a) if isinstance(a, (list, tuple)) else [a]
    b_list = list(b) if isinstance(b, (list, tuple)) else [b]
    m = a_list[0].shape[0]
    n = b_list[0].shape[1] - col0 if ncols is None else ncols
    tm = min(tm, m)
    tn = min(tn, n)
    assert m % tm == 0 and n % tn == 0 and col0 % tn == 0
    off = col0 // tn
    in_specs = [pl.BlockSpec((tm, ai.shape[1]), lambda i, j: (i, 0)) for ai in a_list]
    in_specs += [pl.BlockSpec((bi.shape[0], tn), lambda i, j: (0, off + j)) for bi in b_list]
    args = a_list + b_list
    if res is not None:
        in_specs.append(pl.BlockSpec((tm, tn), lambda i, j: (i, j)))
        args.append(res)
    return pl.pallas_call(
        functools.partial(_mm_kernel, n_a=len(a_list), swiglu=swiglu, has_res=res is not None),
        grid=(m // tm, n // tn),
        in_specs=in_specs,
        out_specs=pl.BlockSpec((tm, tn), lambda i, j: (i, j)),
        out_shape=jax.ShapeDtypeStruct((m, n), out_dtype),
        compiler_params=_params(("parallel", "parallel"), VMEM_BIG),
        name=name,
    )(*args)


def _rope_tab_kernel(pos_ref, inv_ref, c_ref, sa_ref, sb_ref):
    ang = pos_ref[...].astype(F32) * inv_ref[...]
    c = jnp.cos(ang)
    s = jnp.sin(ang)
    lane = lax.broadcasted_iota(jnp.int32, ang.shape, 1)
    c_ref[...] = jnp.where(lane < ROPE_DIM, c, 1.0)
    sa_ref[...] = jnp.where(lane < ROPE_HALF, -s, 0.0)
    sb_ref[...] = jnp.where((lane >= ROPE_HALF) & (lane < ROPE_DIM), s, 0.0)


def rope_tables(pos_flat, tm=256):
    n = pos_flat.shape[0]
    tm = min(tm, n)
    inv = ROPE_THETA ** (-jnp.arange(0, ROPE_DIM, 2, dtype=F32) / ROPE_DIM)
    inv_row = jnp.concatenate([inv, inv, jnp.zeros((LANES - ROPE_DIM,), F32)]).reshape(1, LANES)
    pos_b = jnp.broadcast_to(pos_flat[:, None], (n, LANES))
    spec = pl.BlockSpec((tm, LANES), lambda i: (i, 0))
    shp = jax.ShapeDtypeStruct((n, LANES), F32)
    return pl.pallas_call(
        _rope_tab_kernel,
        grid=(n // tm,),
        in_specs=[spec, pl.BlockSpec((1, LANES), lambda i: (0, 0))],
        out_specs=[spec, spec, spec],
        out_shape=[shp, shp, shp],
        compiler_params=_params(("parallel",)),
        name="rope_tables",
    )(pos_b, inv_row)


def _head_norm(xh, g):
    ms = jnp.mean(xh * xh, axis=-1, keepdims=True)
    return xh * lax.rsqrt(ms + NORM_EPS) * g


def _rope(xh, c, sa, sb):
    return (xh * c + pltpu.roll(xh, LANES - ROPE_HALF, 1) * sa
            + pltpu.roll(xh, ROPE_HALF, 1) * sb)


def _norm_rope_kernel(*refs, heads, do_norm, do_rope, scale):
    x_ref = refs[0]
    pos = 1
    g = None
    if do_norm:
        g = refs[pos][...]
        pos += 1
    if do_rope:
        c, sa, sb = refs[pos][...], refs[pos + 1][...], refs[pos + 2][...]
        pos += 3
    o_ref = refs[pos]
    for h in range(heads):
        xh = x_ref[:, h * HEAD_DIM:(h + 1) * HEAD_DIM]
        if do_norm:
            xh = _head_norm(xh, g)
        if do_rope:
            xh = _rope(xh, c, sa, sb)
        if scale != 1.0:
            xh = xh * scale
        o_ref[:, h * HEAD_DIM:(h + 1) * HEAD_DIM] = xh.astype(o_ref.dtype)


def norm_rope(x2d, col_start, heads, gain, tabs, scale=1.0, tm=256, name="norm_rope"):
    m = x2d.shape[0]
    hb = heads
    while col_start % (hb * HEAD_DIM):
        hb //= 2
    w = hb * HEAD_DIM
    off = col_start // w
    tm = min(tm, m)
    in_specs = [pl.BlockSpec((tm, w), lambda i, j: (i, off + j))]
    args = [x2d]
    if gain is not None:
        in_specs.append(pl.BlockSpec((1, HEAD_DIM), lambda i, j: (0, 0)))
        args.append(gain.reshape(1, HEAD_DIM))
    if tabs is not None:
        in_specs += [pl.BlockSpec((tm, LANES), lambda i, j: (i, 0))] * 3
        args += list(tabs)
    return pl.pallas_call(
        functools.partial(_norm_rope_kernel, heads=hb, do_norm=gain is not None,
                          do_rope=tabs is not None, scale=scale),
        grid=(m // tm, heads // hb),
        in_specs=in_specs,
        out_specs=pl.BlockSpec((tm, w), lambda i, j: (i, j)),
        out_shape=jax.ShapeDtypeStruct((m, heads * HEAD_DIM), BF16),
        compiler_params=_params(("parallel", "parallel")),
        name=name,
    )(*args)


def _compress_kernel(*refs, is_key):
    if is_key:
        x_ref, w1_ref, w2_ref, pe_ref, g_ref, c_ref, sa_ref, sb_ref, o_ref = refs
    else:
        x_ref, w1_ref, w2_ref, pe_ref, o_ref = refs
    half = CMP_STRIDE * HEAD_DIM
    x = x_ref[0, 0]
    nc = x.shape[0]
    top = _dot(x, w1_ref[:half, :], preferred_element_type=F32)
    bot = _dot(x, w1_ref[half:, :], preferred_element_type=F32)
    pe_term = _dot(pe_ref[...], w1_ref[...], preferred_element_type=F32)[0:1, :]
    pre = top + pltpu.roll(bot, nc - 1, 0) + pe_term
    hid = jax.nn.gelu(pre)
    out = _dot(hid.astype(BF16), w2_ref[...], preferred_element_type=F32)
    if is_key:
        out = _rope(_head_norm(out, g_ref[...]), c_ref[0], sa_ref[0], sb_ref[0])
    o_ref[0, 0] = out.astype(o_ref.dtype)


def compress(xblk, w1, w2, pe, key_extras=None, name="compress"):
    b, h, nc, wdt = xblk.shape
    pe8 = jnp.broadcast_to(pe.reshape(1, CMP_BLOCK * HEAD_DIM), (8, CMP_BLOCK * HEAD_DIM)).astype(BF16)
    in_specs = [pl.BlockSpec((1, 1, nc, wdt), lambda i, j: (i, j, 0, 0)),
                pl.BlockSpec((CMP_BLOCK * HEAD_DIM, HEAD_DIM), lambda i, j: (0, 0)),
                pl.BlockSpec((HEAD_DIM, HEAD_DIM), lambda i, j: (0, 0)),
                pl.BlockSpec((8, CMP_BLOCK * HEAD_DIM), lambda i, j: (0, 0))]
    args = [xblk, w1.astype(BF16), w2.astype(BF16), pe8]
    if key_extras is not None:
        gain, tabs = key_extras
        in_specs.append(pl.BlockSpec((1, HEAD_DIM), lambda i, j: (0, 0)))
        in_specs += [pl.BlockSpec((1, nc, LANES), lambda i, j: (i, 0, 0))] * 3
        args += [gain.reshape(1, HEAD_DIM)] + [t.reshape(b, nc, LANES) for t in tabs]
    out_dtype = BF16
    return pl.pallas_call(
        functools.partial(_compress_kernel, is_key=key_extras is not None),
        grid=(b, h),
        in_specs=in_specs,
        out_specs=pl.BlockSpec((1, 1, nc, HEAD_DIM), lambda i, j: (i, j, 0, 0)),
        out_shape=jax.ShapeDtypeStruct((b, h, nc, HEAD_DIM), out_dtype),
        compiler_params=_params(("parallel", "parallel")),
        name=name,
    )(*args)


def _stack_heads(q_ref, group, tq):
    return jnp.concatenate([q_ref[0, :, g * HEAD_DIM:(g + 1) * HEAD_DIM] for g in range(group)], axis=0)


def _unstack_heads(o_ref, o_t, group, tq):
    o = o_t.T
    for g in range(group):
        o_ref[0, :, g * HEAD_DIM:(g + 1) * HEAD_DIM] = o[g * tq:(g + 1) * tq, :].astype(o_ref.dtype)


def _cmp_kernel(q_ref, kc_ref, vct_ref, mt_ref, o_ref, sel_ref, *, tq, n_sel, topn):
    group = NSA_GROUP
    t0 = pl.program_id(2) * tq
    q = _stack_heads(q_ref, group, tq)
    kc = kc_ref[0, 0]
    nc = kc.shape[0]
    s = _nt_dot(kc, q)
    c_idx = lax.broadcasted_iota(jnp.int32, (nc, tq), 0)
    t_idx = t0 + lax.broadcasted_iota(jnp.int32, (nc, tq), 1)
    ok1 = (CMP_STRIDE * c_idx + CMP_BLOCK - 1) <= t_idx
    bias1 = jnp.where(ok1, 0.0, NEG_INF)
    okf1 = jnp.where(ok1, 1.0, 0.0)
    bias = jnp.concatenate([bias1] * group, axis=1)
    okf = jnp.concatenate([okf1] * group, axis=1)
    s = s + bias
    m = jnp.max(s, axis=0, keepdims=True)
    e = jnp.exp(s - m) * okf
    l = jnp.sum(e, axis=0, keepdims=True)
    p = e * jnp.where(l > 0.0, 1.0 / l, 0.0)
    o_t = _dot(vct_ref[0, 0], p.astype(BF16), preferred_element_type=F32)
    _unstack_heads(o_ref, o_t, group, tq)

    psum = p[:, 0:tq]
    for g in range(1, group):
        psum = psum + p[:, g * tq:(g + 1) * tq]
    imp = _dot(mt_ref[...], psum, preferred_element_type=F32,
                  precision=lax.Precision.HIGHEST)
    j_idx = lax.broadcasted_iota(jnp.int32, (n_sel, tq), 0)
    jt = (t0 + lax.broadcasted_iota(jnp.int32, (n_sel, tq), 1)) // SEL_BLOCK
    forced = (j_idx == 0) | (j_idx == jt) | (j_idx == jt - 1)
    imp = jnp.where(forced, FORCED_SCORE, imp)
    imp = jnp.where(j_idx <= jt, imp, -jnp.inf)
    rank = jnp.zeros((n_sel, tq), F32)
    for jp in range(n_sel):
        row = imp[jp:jp + 1, :]
        before = (row > imp) | ((row == imp) & (j_idx > jp))
        rank = rank + jnp.where(before, 1.0, 0.0)
    bias_t = jnp.where(rank < topn, 0.0, -MASK_BIG)
    bias_t = jnp.concatenate([bias_t, jnp.zeros((LANES - n_sel, tq), F32)], axis=0)
    sel_ref[0, 0] = bias_t.T.astype(sel_ref.dtype)


def nsa_compressed(qn, kc, vct, b, t, tq=256):
    nc = kc.shape[2]
    n_sel = t // SEL_BLOCK
    assert n_sel <= LANES
    topn = min(SEL_TOPN, n_sel)
    tq = min(tq, t)
    cs = CMP_STRIDE * np.arange(nc)[:, None]
    ss = SEL_BLOCK * np.arange(n_sel)[None, :]
    ov = np.clip(np.minimum(cs + CMP_BLOCK, ss + SEL_BLOCK) - np.maximum(cs, ss), 0, None) / CMP_BLOCK
    ov[nc - 1, :] = 0.0
    mt = jnp.asarray(ov.T, dtype=F32)
    gw = NSA_GROUP * HEAD_DIM
    return pl.pallas_call(
        functools.partial(_cmp_kernel, tq=tq, n_sel=n_sel, topn=topn),
        grid=(b, NSA_KV_HEADS, t // tq),
        in_specs=[pl.BlockSpec((1, tq, gw), lambda i, h, j: (i, j, h)),
                  pl.BlockSpec((1, 1, nc, HEAD_DIM), lambda i, h, j: (i, h, 0, 0)),
                  pl.BlockSpec((1, 1, HEAD_DIM, nc), lambda i, h, j: (i, h, 0, 0)),
                  pl.BlockSpec((n_sel, nc), lambda i, h, j: (0, 0))],
        out_specs=[pl.BlockSpec((1, tq, gw), lambda i, h, j: (i, j, h)),
                   pl.BlockSpec((1, 1, tq, LANES), lambda i, h, j: (i, h, j, 0))],
        out_shape=[jax.ShapeDtypeStruct((b, t, NSA_HEADS * HEAD_DIM), F32),
                   jax.ShapeDtypeStruct((b, NSA_KV_HEADS, t, LANES), BF16)],
        compiler_params=_params(("parallel", "parallel", "parallel")),
        name="nsa_compressed",
    )(qn, kc, vct, mt)


def _flash_kernel(*refs, mode, group, slab, tq, tk):
    if mode == "window":
        q_ref, k_ref, vt_ref, o_ref, q_s, m_s, l_s, acc_s = refs
        xq = None
    else:
        q_ref, k_ref, vt_ref, xq_ref, xk_ref, o_ref, q_s, m_s, l_s, acc_s = refs
        xq = xq_ref[0, 0] if mode == "sel" else xq_ref[...]
    qi = pl.program_id(2)
    t0 = qi * tq
    q = _stack_heads(q_ref, group, tq)
    if xq is not None:
        q = jnp.concatenate([q, jnp.concatenate([xq] * group, axis=0)], axis=1)
    q_s[...] = q
    m_s[...] = jnp.full(m_s.shape, NEG_INF, F32)
    l_s[...] = jnp.zeros(l_s.shape, F32)
    acc_s[...] = jnp.zeros(acc_s.shape, F32)

    def tile(j, keep):
        k0 = pl.multiple_of(j * tk, tk)
        kt = k_ref[0, pl.ds(k0, tk), :]
        if mode == "sel":
            kt = jnp.concatenate([kt, xk_ref[pl.ds(k0, tk), :]], axis=1)
        elif mode == "dsa":
            kt = jnp.concatenate([kt, xk_ref[0, pl.ds(k0, tk), :]], axis=1)
        vt = vt_ref[0, 0, j]
        if keep is not None:
            kpos = k0 + lax.broadcasted_iota(jnp.int32, (tk, tq), 0)
            tpos = t0 + lax.broadcasted_iota(jnp.int32, (tk, tq), 1)
            ok = keep(kpos, tpos)
        for lo in range(0, group, slab):
            cols = slice(lo * tq, (lo + slab) * tq)
            s = _nt_dot(kt, q_s[cols, :])
            if keep is not None:
                s = jnp.concatenate([jnp.where(ok, s[:, g * tq:(g + 1) * tq], NEG_INF)
                                     for g in range(slab)], axis=1)
            m_old = m_s[:, cols]
            m_new = jnp.maximum(m_old, jnp.max(s, axis=0, keepdims=True))
            alpha = jnp.exp(m_old - m_new)
            p = jnp.exp(s - m_new)
            l_s[:, cols] = alpha * l_s[:, cols] + jnp.sum(p, axis=0, keepdims=True)
            acc_s[:, cols] = acc_s[:, cols] * alpha + _dot(vt, p.astype(BF16))
            m_s[:, cols] = m_new

    causal = lambda kpos, tpos: kpos <= tpos

    def plain(j, carry):
        tile(j, None)
        return carry

    if mode == "dsa":
        lax.fori_loop(0, (t0 + tq - 1) // tk + 1, plain, 0)
    elif mode == "sel":
        jd = (t0 + tq - 1) // tk
        lax.fori_loop(0, jd, plain, 0)
        tile(jd, causal)
    else:
        nw = WINDOW // tk

        @pl.when(qi >= nw)
        def _():
            tile(qi - nw, lambda kpos, tpos: tpos - kpos < WINDOW)

        for dist in range(nw - 1, 0, -1):
            @pl.when(qi >= dist)
            def _():
                tile(qi - dist, None)

        tile(qi, causal)

    _unstack_heads(o_ref, acc_s[...] * (1.0 / l_s[...]), group, tq)


def flash_masked(qn, kn, vt5, *, mode, group, b, t, tq, slab=None, extra=None, out_dtype=F32, name="flash"):
    kvh = kn.shape[2] // HEAD_DIM
    gw = group * HEAD_DIM
    tq = min(tq, t)
    ntk, tk = vt5.shape[2], vt5.shape[4]
    slab = group if slab is None else slab
    in_specs = [pl.BlockSpec((1, tq, gw), lambda i, h, qi: (i, qi, h)),
                pl.BlockSpec((1, t, HEAD_DIM), lambda i, h, qi: (i, 0, h)),
                pl.BlockSpec((1, 1, ntk, HEAD_DIM, tk), lambda i, h, qi: (i, h, 0, 0, 0))]
    args = [qn, kn, vt5]
    kdim = HEAD_DIM
    if mode == "sel":
        assert tk % tq == 0
        et = np.zeros((t, LANES), np.float32)
        et[np.arange(t), np.arange(t) // SEL_BLOCK] = 1.0
        in_specs += [pl.BlockSpec((1, 1, tq, LANES), lambda i, h, qi: (i, h, qi, 0)),
                     pl.BlockSpec((t, LANES), lambda i, h, qi: (0, 0))]
        args += [extra, jnp.asarray(et, dtype=BF16)]
        kdim += LANES
    elif mode == "dsa":
        assert tq == LANES
        in_specs += [pl.BlockSpec((tq, LANES), lambda i, h, qi: (0, 0)),
                     pl.BlockSpec((1, t, tq), lambda i, h, qi: (i, 0, qi))]
        args += [jnp.eye(tq, dtype=BF16), extra]
        kdim += LANES
    else:
        assert tk == tq and WINDOW % tk == 0
    return pl.pallas_call(
        functools.partial(_flash_kernel, mode=mode, group=group, slab=slab, tq=tq, tk=tk),
        grid=(b, kvh, t // tq),
        in_specs=in_specs,
        out_specs=pl.BlockSpec((1, tq, gw), lambda i, h, qi: (i, qi, h)),
        out_shape=jax.ShapeDtypeStruct((b, t, kvh * gw), out_dtype),
        scratch_shapes=[pltpu.VMEM((group * tq, kdim), BF16),
                        pltpu.VMEM((1, group * tq), F32),
                        pltpu.VMEM((1, group * tq), F32),
                        pltpu.VMEM((HEAD_DIM, group * tq), F32)],
        compiler_params=_params(("parallel", "parallel", "parallel")),
        name=name,
    )(*args)


def _nsa_merge_kernel(oc_ref, os_ref, ow_ref, g_ref, e_ref, o_ref):
    gate = _sigmoid(g_ref[...])
    out = None
    for j, src in enumerate((oc_ref, os_ref, ow_ref)):
        gj = _split_dot(gate, e_ref[j])
        term = gj * src[...]
        out = term if out is None else out + term
    o_ref[...] = out.astype(o_ref.dtype)


def nsa_merge(oc, os_, ow, pc, tm=256):
    m, w = oc.shape
    tm = min(tm, m)
    e = np.zeros((3, LANES, w), np.float32)
    for h in range(NSA_HEADS):
        for j in range(3):
            e[j, h * 3 + j, h * HEAD_DIM:(h + 1) * HEAD_DIM] = 1.0
    spec = pl.BlockSpec((tm, w), lambda i: (i, 0))
    return pl.pallas_call(
        _nsa_merge_kernel,
        grid=(m // tm,),
        in_specs=[spec, spec, spec,
                  pl.BlockSpec((tm, LANES), lambda i: (i, 0)),
                  pl.BlockSpec((3, LANES, w), lambda i: (0, 0, 0))],
        out_specs=spec,
        out_shape=jax.ShapeDtypeStruct((m, w), BF16),
        compiler_params=_params(("parallel",)),
        name="nsa_merge",
    )(oc, os_, ow, pc, jnp.asarray(e, dtype=BF16))


SUB = 8
HEADS_PER_SLAB = LANES // RWKV_HEADS
N_SLABS = RWKV_WIDTH // LANES


def _to_jh(p):
    lead = p.shape[:-1]
    return p.reshape(*lead, RWKV_HEADS, RWKV_HEAD).swapaxes(-1, -2).reshape(*lead, RWKV_WIDTH)


def _lane_table(p, nb):
    tab = p.reshape(RWKV_HEADS, RWKV_HEAD).T
    return jnp.tile(tab, (1, nb)).reshape(RWKV_HEAD, 1, LANES)


def _head_sum_jh(x, bd):
    acc = x[:, 0:LANES]
    for s in range(1, N_SLABS):
        acc = acc + x[:, s * LANES:(s + 1) * LANES]
    tot = _split_dot(acc, bd)
    return jnp.concatenate([tot] * N_SLABS, axis=1)


def _interleave(xs, out_ref):
    nb = len(xs)
    tt = xs[0].shape[0]
    group = (lax.broadcasted_iota(jnp.int32, xs[0].shape, 1) % LANES) // RWKV_HEADS
    moved = {}
    for delta in range(1 - nb, nb):
        if delta == 0:
            continue
        src = None
        for g in range(max(0, -delta), min(nb, nb - delta)):
            src = xs[g + delta] if src is None else jnp.where(group == g, xs[g + delta], src)
        moved[delta] = pltpu.roll(src, (delta * RWKV_HEADS) % RWKV_WIDTH, 1)
    for jj in range(HEADS_PER_SLAB):
        z = xs[jj]
        for bi in range(nb):
            if bi != jj:
                z = jnp.where(group == bi, moved[bi - jj], z)
        for tb in range(tt // SUB):
            for s in range(N_SLABS):
                out_ref[tb, HEADS_PER_SLAB * s + jj] = z[tb * SUB:(tb + 1) * SUB, s * LANES:(s + 1) * LANES]


def _deinterleave(val, o_ref):
    nb = o_ref.shape[0]
    tt = o_ref.shape[1]
    zs = []
    for jj in range(HEADS_PER_SLAB):
        z = jnp.concatenate([val[:, HEADS_PER_SLAB * s + jj] for s in range(N_SLABS)], axis=-1)
        zs.append(z.reshape(tt, RWKV_WIDTH))
    lane = lax.broadcasted_iota(jnp.int32, (tt, RWKV_WIDTH), 1) % LANES
    for bi in range(nb):
        out = None
        for jj in range(HEADS_PER_SLAB):
            shift = ((jj - bi) * RWKV_HEADS) % RWKV_WIDTH
            zj = zs[jj] if shift == 0 else pltpu.roll(zs[jj], shift, 1)
            out = zj if out is None else jnp.where(lane // RWKV_HEADS == jj, zj, out)
        o_ref[bi] = out.astype(o_ref.dtype)


def _shifted(x, prev_row, first):
    rolled = pltpu.roll(x, 1, 0)
    row0 = jnp.where(first, 0.0, prev_row)
    ridx = lax.broadcasted_iota(jnp.int32, x.shape, 0)
    return jnp.where(ridx == 0, row0, rolled)


def _rwkv_prep_kernel(pb_ref, pbp_ref, pc_ref, pcp_ref, mub_ref, muc_ref, w0_ref, w2_ref, a0_ref, a2_ref,
                      g2_ref, kkg_ref, ka_ref, bd_ref,
                      r_ref, w_ref, k_ref, v_ref, kk_ref, b_ref, g_ref):
    first = pl.program_id(0) == 0
    w = RWKV_WIDTH
    nb = pb_ref.shape[0]
    outs = [[] for _ in range(7)]
    for bi in range(nb):
        xb = pb_ref[bi]
        xb = xb + (_shifted(xb, pbp_ref[bi, SUB - 1:SUB, :], first) - xb) * mub_ref[...]
        xc = pc_ref[bi]
        xc = xc + (_shifted(xc, pcp_ref[bi, SUB - 1:SUB, :], first) - xc) * muc_ref[...]
        r, k, v = xb[:, :w], xb[:, w:2 * w], xb[:, 2 * w:]
        wd, ad, gd = xc[:, LANES:2 * LANES], xc[:, 2 * LANES:3 * LANES], xc[:, 3 * LANES:]
        z = w0_ref[...] + _dot(jnp.tanh(wd).astype(BF16), w2_ref[...])
        nz = -z
        softplus = jnp.maximum(nz, 0.0) + jnp.log(1.0 + jnp.exp(-jnp.abs(nz)))
        w_log = -softplus - 0.5
        a = _sigmoid(a0_ref[...] + _dot(ad.astype(BF16), a2_ref[...]))
        g = _dot(_sigmoid(gd).astype(BF16), g2_ref[...])
        kkv = k * kkg_ref[...]
        norm = jnp.sqrt(_head_sum_jh(kkv * kkv, bd_ref[...]))
        kkv = kkv / jnp.maximum(norm, 1e-12)
        vals = (r, jnp.exp(-jnp.exp(w_log)), k * (1.0 + (a - 1.0) * ka_ref[...]), v, kkv, kkv * a, g)
        for lst, val in zip(outs, vals):
            lst.append(val)
    for lst, ref in zip(outs, (r_ref, w_ref, k_ref, v_ref, kk_ref, b_ref, g_ref)):
        _interleave(lst, ref)


def _pad_rows(wm, rows):
    return jnp.pad(wm, ((0, rows - wm.shape[0]), (0, 0)))


def _same_head_ones():
    idx = np.arange(LANES) % RWKV_HEADS
    return jnp.asarray(idx[:, None] == idx[None, :], dtype=BF16)


def rwkv_prep(pb, pc, b, t, mu, w0, w2, a0, a2, g2, kk_gain, k_a, tt=32):
    w = RWKV_WIDTH
    assert b * RWKV_HEADS == LANES and t % tt == 0 and tt % SUB == 0
    mu_r, mu_k, mu_v, mu_wd, mu_ad, mu_gd = jnp.split(
        mu, [int(x) for x in np.cumsum([w, w, w, LORA_DECAY, LORA_AAA])])
    mub = jnp.concatenate([_to_jh(mu_r), _to_jh(mu_k), _to_jh(mu_v)]).reshape(1, 3 * w)
    pad = lambda z: jnp.pad(z, (0, LANES - z.shape[0]))
    muc = jnp.concatenate([jnp.zeros((LANES,), F32), pad(mu_wd), pad(mu_ad), mu_gd]).reshape(1, -1)
    cw = pc.shape[1]
    row = lambda z: _to_jh(z).reshape(1, w)
    lora = lambda wm, rows: _to_jh(_pad_rows(wm, rows)).astype(BF16)
    full = lambda shape: pl.BlockSpec(shape, lambda i: (0,) * len(shape))
    tile = lambda width: pl.BlockSpec((b, tt, width), lambda i: (0, i, 0))
    prev = lambda width: pl.BlockSpec((b, SUB, width), lambda i: (0, jnp.maximum(i * (tt // SUB) - 1, 0), 0))
    out_spec = pl.BlockSpec((tt // SUB, RWKV_HEAD, SUB, LANES), lambda i: (i, 0, 0, 0))
    shp = jax.ShapeDtypeStruct((t // SUB, RWKV_HEAD, SUB, LANES), F32)
    pb3 = pb.reshape(b, t, 3 * w)
    pc3 = pc.reshape(b, t, cw)
    return pl.pallas_call(
        _rwkv_prep_kernel,
        grid=(t // tt,),
        in_specs=[tile(3 * w), prev(3 * w), tile(cw), prev(cw), full((1, 3 * w)), full((1, cw)),
                  full((1, w)), full((LANES, w)), full((1, w)), full((LANES, w)), full((LORA_GATE, w)),
                  full((1, w)), full((1, w)), full((LANES, LANES))],
        out_specs=[out_spec] * 7,
        out_shape=[shp] * 7,
        compiler_params=_params(("parallel",), VMEM_BIG),
        name="rwkv_prep",
    )(pb3, pb3, pc3, pc3, mub, muc, row(w0), lora(w2, LANES), row(a0), lora(a2, LANES), lora(g2, LORA_GATE),
      row(kk_gain), row(k_a), _same_head_ones())


def _rwkv_scan_kernel(w_ref, kk_ref, b_ref, k_ref, r_ref, v_ref, y_ref, s_ref, op_s, *, tb, rows):
    ig = pl.program_id(1)
    blk = SUB * RWKV_HEAD

    @pl.when(pl.program_id(0) == 0)
    def _():
        s_ref[ig] = jnp.zeros(s_ref.shape[1:], F32)

    @pl.when(ig == 0)
    def _():
        def gather(u, carry):
            base = pl.multiple_of(u * blk, blk)
            for q in range(SUB):
                key = pl.ds(base + q, RWKV_HEAD, stride=SUB)
                for n, ref in enumerate((w_ref, kk_ref, b_ref, k_ref, r_ref)):
                    op_s[n, u * SUB + q] = ref[key, :]
            return carry
        lax.fori_loop(0, tb // SUB, gather, 0)

    def steps(u, s):
        base = pl.multiple_of(u * blk, blk) + ig * (rows * SUB)
        for q in range(SUB):
            t = u * SUB + q
            val = pl.ds(base + q, rows, stride=SUB)
            sa = -jnp.sum(s * op_s[1, t][None], axis=1)
            s = (s * op_s[0, t][None] + sa[:, None, :] * op_s[2, t][None]
                 + v_ref[val, :][:, None, :] * op_s[3, t][None])
            y_ref[val, :] = jnp.sum(s * op_s[4, t][None], axis=1)
        return s

    s_ref[ig] = lax.fori_loop(0, tb // SUB, steps, s_ref[ig])


def rwkv_scan(w, kk, bb, k, r, v, tb=64, rows=4):
    n = RWKV_HEAD
    t = w.shape[0] // n
    groups = n // rows
    tb = min(tb, t)
    op = pl.BlockSpec((tb * n, LANES), lambda i, j: (i, 0))
    return pl.pallas_call(
        functools.partial(_rwkv_scan_kernel, tb=tb, rows=rows),
        grid=(t // tb, groups),
        in_specs=[op] * 6,
        out_specs=op,
        out_shape=jax.ShapeDtypeStruct((t * n, LANES), F32),
        scratch_shapes=[pltpu.VMEM((groups, rows, n, LANES), F32),
                        pltpu.VMEM((5, tb, n, LANES), F32)],
        compiler_params=_params(("arbitrary", "arbitrary"), VMEM_BIG),
        name="rwkv_scan",
    )(w, kk, bb, k, r, v)


def _rwkv_post_kernel(y_ref, r_ref, k_ref, v_ref, g_ref, rk_ref, gg_ref, gb_ref, o_ref):
    y = y_ref[...]
    inv_n = 1.0 / RWKV_HEAD
    mean = jnp.sum(y, axis=1, keepdims=True) * inv_n
    d = y - mean
    var = jnp.sum(d * d, axis=1, keepdims=True) * inv_n
    yn = d * lax.rsqrt(var + GN_EPS) * gg_ref[...][None] + gb_ref[...][None]
    bonus = jnp.sum(r_ref[...] * k_ref[...] * rk_ref[...][None], axis=1, keepdims=True) * v_ref[...]
    _deinterleave((yn + bonus) * g_ref[...], o_ref)


def rwkv_post(y, r, k, v, g, b, t, r_k, gn_g, gn_b, tt=32):
    tile = pl.BlockSpec((tt // SUB, RWKV_HEAD, SUB, LANES), lambda i: (i, 0, 0, 0))
    tab = pl.BlockSpec((RWKV_HEAD, 1, LANES), lambda i: (0, 0, 0))
    return pl.pallas_call(
        _rwkv_post_kernel,
        grid=(t // tt,),
        in_specs=[tile] * 5 + [tab] * 3,
        out_specs=pl.BlockSpec((b, tt, RWKV_WIDTH), lambda i: (0, i, 0)),
        out_shape=jax.ShapeDtypeStruct((b, t, RWKV_WIDTH), BF16),
        compiler_params=_params(("parallel",)),
        name="rwkv_post",
    )(y, r, k, v, g, _lane_table(r_k.reshape(-1), b), _lane_table(gn_g, b), _lane_table(gn_b, b))


def rwkv7(pb, pc, b, t, mu, w0, w2, a0, a2, g2, kk_gain, k_a, r_k, gn_g, gn_b):
    r, w, k, v, kk, bb, g = rwkv_prep(pb, pc, b, t, mu, w0, w2, a0, a2, g2, kk_gain, k_a)
    rows2d = lambda z: z.reshape(t * RWKV_HEAD, LANES)
    y = rwkv_scan(rows2d(w), rows2d(kk), rows2d(bb), rows2d(k), rows2d(r), rows2d(v))
    y = y.reshape(t // SUB, RWKV_HEAD, SUB, LANES)
    return rwkv_post(y, r, k, v, g, b, t, r_k, gn_g, gn_b).reshape(b * t, RWKV_WIDTH)


def _dsa_select_kernel(qi_ref, ki_ref, wt_ref, bias_ref, q_s, key_s, *, t, tq, kc, topk):
    t0 = pl.program_id(1) * tq
    nchunk = (t0 + tq - 1) // kc + 1
    for h in range(IDX_HEADS):
        q_s[h * tq:(h + 1) * tq, :] = qi_ref[0, :, h * IDX_DIM:(h + 1) * IDX_DIM]
    wt = wt_ref[0]

    def chunk(c, carry):
        k0 = pl.multiple_of(c * kc, kc)
        lg = _nt_dot(ki_ref[0, pl.ds(k0, kc), :], q_s[...])
        acc = jnp.zeros((kc, tq), F32)
        for h in range(IDX_HEADS):
            acc = acc + jnp.maximum(lg[:, h * tq:(h + 1) * tq], 0.0) * wt[h:h + 1, :]
        spos = k0 + lax.broadcasted_iota(jnp.int32, (kc, tq), 0)
        tpos = t0 + lax.broadcasted_iota(jnp.int32, (kc, tq), 1)
        score = jnp.where(spos <= tpos, acc + 0.0, -jnp.inf)
        bits = pltpu.bitcast(score, jnp.int32)
        key_s[pl.ds(k0, kc), :] = jnp.where(bits < 0, bits ^ jnp.int32(0x7FFFFFFF), bits)
        return carry

    lax.fori_loop(0, nchunk, chunk, 0)
    kf = jnp.float32(topk)
    part = 64

    def count(pred):
        def body(c, acc):
            k0 = pl.multiple_of(c * kc, kc)
            ind = jnp.where(pred(key_s[pl.ds(k0, kc), :], k0), 1.0, 0.0)
            return acc + jnp.sum(ind.reshape(kc // part, part, tq), axis=0)
        acc = lax.fori_loop(0, nchunk, body, jnp.zeros((part, tq), F32))
        return jnp.sum(acc, axis=0, keepdims=True)

    def value_bit(it, tau_u):
        cand_u = tau_u | lax.shift_left(jnp.int32(1), 31 - it)
        cand_s = cand_u ^ jnp.int32(INT_MIN)
        return jnp.where(count(lambda blk, k0: blk >= cand_s) >= kf, cand_u, tau_u)

    tau_u = lax.fori_loop(0, 32, value_bit, jnp.zeros((1, tq), jnp.int32))
    tau = tau_u ^ jnp.int32(INT_MIN)
    need = kf - count(lambda blk, k0: blk > tau)
    n_eq = count(lambda blk, k0: blk == tau)
    nbits = int(np.log2(t))

    def tie_search():
        def index_bit(it, jj):
            cand = jj | lax.shift_left(jnp.int32(1), nbits - 1 - it)

            def pred(blk, k0):
                spos = k0 + lax.broadcasted_iota(jnp.int32, (kc, tq), 0)
                return (blk == tau) & (spos < cand)
            return jnp.where(count(pred) < need, cand, jj)
        return lax.fori_loop(0, nbits, index_bit, jnp.zeros((1, tq), jnp.int32))

    jj = lax.cond(jnp.max(n_eq - need) > 0.0, tie_search, lambda: jnp.full((1, tq), t, jnp.int32))

    def emit(c, carry):
        k0 = pl.multiple_of(c * kc, kc)
        blk = key_s[pl.ds(k0, kc), :]
        spos = k0 + lax.broadcasted_iota(jnp.int32, (kc, tq), 0)
        tpos = t0 + lax.broadcasted_iota(jnp.int32, (kc, tq), 1)
        keep = ((blk > tau) | ((blk == tau) & (spos <= jj))) & (spos <= tpos)
        bias_ref[0, pl.ds(k0, kc), :] = jnp.where(keep, 0.0, -MASK_BIG).astype(bias_ref.dtype)
        return carry

    lax.fori_loop(0, nchunk, emit, 0)

    def fill(c, carry):
        k0 = pl.multiple_of(c * kc, kc)
        bias_ref[0, pl.ds(k0, kc), :] = jnp.full((kc, tq), -MASK_BIG, bias_ref.dtype)
        return carry

    lax.fori_loop(nchunk, t // kc, fill, 0)


def dsa_select(qi, ki, wt, b, t, tq=128, kc=512):
    topk = min(DSA_TOPK_MAX, t // 4)
    kc = min(kc, t)
    assert topk <= kc and t % kc == 0
    return pl.pallas_call(
        functools.partial(_dsa_select_kernel, t=t, tq=tq, kc=kc, topk=topk),
        grid=(b, t // tq),
        in_specs=[pl.BlockSpec((1, tq, IDX_HEADS * IDX_DIM), lambda i, j: (i, j, 0)),
                  pl.BlockSpec((1, t, IDX_DIM), lambda i, j: (i, 0, 0)),
                  pl.BlockSpec((1, IDX_HEADS, tq), lambda i, j: (i, 0, j))],
        out_specs=pl.BlockSpec((1, t, tq), lambda i, j: (i, 0, j)),
        out_shape=jax.ShapeDtypeStruct((b, t, t), BF16),
        scratch_shapes=[pltpu.VMEM((IDX_HEADS * tq, IDX_DIM), BF16),
                        pltpu.VMEM((t, tq), jnp.int32)],
        compiler_params=_params(("parallel", "parallel"), VMEM_BIG),
        name="dsa_select",
    )(qi, ki, wt)


def _xattn_kernel(q_ref, g_ref, mk_ref, mv_ref, o_ref):
    scale = HEAD_DIM ** -0.5
    for h in range(MEM_HEADS):
        sl = slice(h * HEAD_DIM, (h + 1) * HEAD_DIM)
        qh = (_head_norm(q_ref[:, sl], g_ref[...]) * scale).astype(BF16)
        s = _nt_dot(qh, mk_ref[0, :, sl])
        m = jnp.max(s, axis=-1, keepdims=True)
        e = jnp.exp(s - m)
        p = e / jnp.sum(e, axis=-1, keepdims=True)
        o_ref[:, sl] = _dot(p.astype(BF16), mv_ref[0, :, sl],
                               preferred_element_type=F32).astype(o_ref.dtype)


def mem_xattn(qm, gain, mk, mv, b, t, tq=256):
    tq = min(tq, t)
    nt = t // tq
    mtok = mk.shape[1]
    return pl.pallas_call(
        _xattn_kernel,
        grid=(b, nt),
        in_specs=[pl.BlockSpec((tq, MEM_WIDTH), lambda i, j: (i * nt + j, 0)),
                  pl.BlockSpec((1, HEAD_DIM), lambda i, j: (0, 0)),
                  pl.BlockSpec((1, mtok, MEM_WIDTH), lambda i, j: (i, 0, 0)),
                  pl.BlockSpec((1, mtok, MEM_WIDTH), lambda i, j: (i, 0, 0))],
        out_specs=pl.BlockSpec((tq, MEM_WIDTH), lambda i, j: (i * nt + j, 0)),
        out_shape=jax.ShapeDtypeStruct((b * t, MEM_WIDTH), BF16),
        compiler_params=_params(("parallel", "parallel")),
        name="mem_xattn",
    )(qm, gain.reshape(1, HEAD_DIM), mk, mv)


def _kv_transposed(x2d, lo, b, t, heads, tk=FLASH_TK):
    tk = min(tk, t)
    v = x2d[:, lo:lo + heads * HEAD_DIM].astype(BF16)
    return v.reshape(b, t // tk, tk, heads, HEAD_DIM).transpose(0, 3, 1, 4, 2)


def nsa_mixer(pa, pc, b, t, tabs, cmp_tabs, q_norm, kc_norm, ks_norm, kw_norm, pe_k, pe_v,
              ck_w1, ck_w2, cv_w1, cv_w2):
    kvw = NSA_KV_HEADS * HEAD_DIM
    scale = HEAD_DIM ** -0.5
    qn = norm_rope(pa, 0, NSA_HEADS, q_norm, tabs, scale=scale, name="nsa_q_prep")
    ksn = norm_rope(pa, 6 * kvw, NSA_KV_HEADS, ks_norm, tabs, name="nsa_ks_prep").reshape(b, t, kvw)
    kwn = norm_rope(pa, 8 * kvw, NSA_KV_HEADS, kw_norm, tabs, name="nsa_kw_prep").reshape(b, t, kvw)
    vst = _kv_transposed(pa, 7 * kvw, b, t, NSA_KV_HEADS)
    vwt = _kv_transposed(pa, 9 * kvw, b, t, NSA_KV_HEADS)
    qn = qn.reshape(b, t, NSA_HEADS * HEAD_DIM)

    nc = t // CMP_STRIDE

    def chunked(lo):
        z = pa[:, lo:lo + kvw].astype(BF16).reshape(b, nc, CMP_STRIDE, NSA_KV_HEADS, HEAD_DIM)
        return z.transpose(0, 3, 1, 2, 4).reshape(b, NSA_KV_HEADS, nc, CMP_STRIDE * HEAD_DIM)

    kc = compress(chunked(4 * kvw), ck_w1, ck_w2, pe_k, key_extras=(kc_norm, cmp_tabs), name="compress_k")
    vc = compress(chunked(5 * kvw), cv_w1, cv_w2, pe_v, name="compress_v")
    vct = vc.transpose(0, 1, 3, 2)
    o_c, sel_bias = nsa_compressed(qn, kc, vct, b, t)
    o_s = flash_masked(qn, ksn, vst, mode="sel", group=NSA_GROUP, b=b, t=t, tq=256,
                       extra=sel_bias, name="nsa_selected")
    o_w = flash_masked(qn, kwn, vwt, mode="window", group=NSA_GROUP, b=b, t=t, tq=FLASH_TK,
                       name="nsa_window")
    m = b * t
    w = NSA_HEADS * HEAD_DIM
    return nsa_merge(o_c.reshape(m, w), o_s.reshape(m, w), o_w.reshape(m, w), pc)


def dsa_mixer_core(p1, p2, b, t, tabs, q_norm, k_norm, ki_norm):
    scale = HEAD_DIM ** -0.5
    qw = DSA_HEADS * HEAD_DIM
    kvw = DSA_KV_HEADS * HEAD_DIM
    qn = norm_rope(p1, 0, DSA_HEADS, q_norm, tabs, scale=scale, name="dsa_q_prep").reshape(b, t, qw)
    kn = norm_rope(p1, qw, DSA_KV_HEADS, k_norm, tabs, name="dsa_k_prep").reshape(b, t, kvw)
    vt = _kv_transposed(p1, qw + kvw, b, t, DSA_KV_HEADS)
    qi = norm_rope(p1, qw + 2 * kvw, IDX_HEADS, None, tabs, name="dsa_qi_prep")
    qi = qi.reshape(b, t, IDX_HEADS * IDX_DIM)
    ki = norm_rope(p2, 0, 1, ki_norm, tabs, name="dsa_ki_prep").reshape(b, t, IDX_DIM)
    wscale = IDX_HEADS ** -0.5 * IDX_DIM ** -0.5
    wt = (p2[:, IDX_DIM:IDX_DIM + IDX_HEADS] * wscale).reshape(b, t, IDX_HEADS).transpose(0, 2, 1)
    key_bias = dsa_select(qi, ki, wt, b, t)
    o = flash_masked(qn, kn, vt, mode="dsa", group=DSA_GROUP, b=b, t=t, tq=LANES,
                     extra=key_bias, out_dtype=BF16, name="dsa_attention")
    return o.reshape(b * t, qw)


def kernel(x, mem, positions, mem_norm, mem_w_kv, mem_k_norm, l0_mix_norm, l0_w_in, l0_w_out, nsa_q_norm, nsa_kc_norm, nsa_ks_norm, nsa_kw_norm, nsa_pe_k, nsa_pe_v, nsa_ck_w1, nsa_ck_w2, nsa_cv_w1, nsa_cv_w2, rwkv_mu, rwkv_w0, rwkv_w2, rwkv_a0, rwkv_a2, rwkv_g2, rwkv_kk, rwkv_ka, rwkv_rk, rwkv_gn_g, rwkv_gn_b, l0_xattn_norm, l0_mem_wq, l0_mem_q_norm, l0_mem_wo, l0_ffn_norm, l0_w1, l0_w3, l0_w2, l1_mix_norm, l1_w_in, l1_w_out, dsa_q_norm, dsa_k_norm, dsa_ki_norm, l1_xattn_norm, l1_mem_wq, l1_mem_q_norm, l1_mem_wo, l1_ffn_norm, l1_w1, l1_w3, l1_w2):
    b, t, d = x.shape
    m = b * t
    bf = lambda z: z.astype(BF16)
    x2 = x.reshape(m, d)

    tabs = rope_tables(positions.reshape(m))
    nc = t // CMP_STRIDE
    cmp_pos = positions[:, CMP_BLOCK - 1::CMP_STRIDE]
    cmp_pos = jnp.concatenate([cmp_pos, cmp_pos[:, -1:]], axis=1)
    cmp_tabs = rope_tables(cmp_pos.reshape(b * nc))

    mtok = mem.shape[1]
    memn = rmsnorm(mem.reshape(b * mtok, d), mem_norm)
    mkv = matmul(memn, mem_w_kv, tm=512, tn=512, name="mem_kv_proj")
    mk = norm_rope(mkv, 0, MEM_HEADS, mem_k_norm, None, name="mem_k_norm").reshape(b, mtok, MEM_WIDTH)
    mv = bf(mkv[:, MEM_WIDTH:]).reshape(b, mtok, MEM_WIDTH)

    def tail(x2, xn, wq, qn, wo, fn, w1, w3, w2):
        h = rmsnorm(x2, xn)
        qm = matmul(h, wq, tm=1024, tn=512, name="xattn_q_proj")
        o = mem_xattn(qm, qn, mk, mv, b, t)
        x2 = matmul(o, wo, tm=1024, tn=512, res=x2, name="xattn_out_proj")
        h = rmsnorm(x2, fn)
        u = matmul(h, [w1, w3], swiglu=True, tm=1024, tn=256, out_dtype=BF16, name="ffn_up")
        return matmul(u, bf(w2), tm=512, tn=512, res=x2, name="ffn_down")

    nsa_cols = NSA_HEADS * HEAD_DIM + 6 * NSA_KV_HEADS * HEAD_DIM
    gate_cols = 3 * NSA_HEADS
    rw0 = nsa_cols + gate_cols
    rw1 = rw0 + 3 * RWKV_WIDTH
    pad_cols = lambda wm: jnp.pad(wm, ((0, 0), (0, LANES - wm.shape[1])))
    w_c = jnp.concatenate([
        pad_cols(l0_w_in[:, nsa_cols:rw0]),
        pad_cols(l0_w_in[:, rw1:rw1 + LORA_DECAY]),
        pad_cols(l0_w_in[:, rw1 + LORA_DECAY:rw1 + LORA_DECAY + LORA_AAA]),
        l0_w_in[:, rw1 + LORA_DECAY + LORA_AAA:]], axis=1)
    h = rmsnorm(x2, l0_mix_norm)
    pa = matmul(h, l0_w_in, ncols=nsa_cols, tm=1024, tn=512, name="l0_proj_nsa")
    w_rkv = l0_w_in[:, rw0:rw1].reshape(d, 3, RWKV_WIDTH)
    pb = matmul(h, bf(_to_jh(w_rkv).reshape(d, 3 * RWKV_WIDTH)), tm=1024, tn=512, name="l0_proj_rwkv")
    pc = matmul(h, bf(w_c), tm=1024, tn=w_c.shape[1], name="l0_proj_small")
    o_a = nsa_mixer(pa, pc, b, t, tabs, cmp_tabs, nsa_q_norm, nsa_kc_norm, nsa_ks_norm, nsa_kw_norm,
                    nsa_pe_k, nsa_pe_v, nsa_ck_w1, nsa_ck_w2, nsa_cv_w1, nsa_cv_w2)
    o_b = rwkv7(pb, pc, b, t, rwkv_mu, rwkv_w0, rwkv_w2, rwkv_a0, rwkv_a2, rwkv_g2, rwkv_kk, rwkv_ka,
                rwkv_rk, rwkv_gn_g, rwkv_gn_b)
    nsa_w = NSA_HEADS * HEAD_DIM
    w_out_rwkv = l0_w_out[nsa_w:].reshape(RWKV_HEADS, RWKV_HEAD, d).swapaxes(0, 1).reshape(RWKV_WIDTH, d)
    x2 = matmul([o_a, o_b], [l0_w_out[:nsa_w], w_out_rwkv], tm=1024, tn=512, res=x2,
                name="l0_out_proj")
    x2 = tail(x2, l0_xattn_norm, l0_mem_wq, l0_mem_q_norm, l0_mem_wo, l0_ffn_norm, l0_w1, l0_w3, l0_w2)

    main_cols = DSA_HEADS * HEAD_DIM + 2 * DSA_KV_HEADS * HEAD_DIM + IDX_HEADS * IDX_DIM
    w_s = jnp.concatenate([l1_w_in[:, main_cols:main_cols + IDX_DIM],
                           pad_cols(l1_w_in[:, main_cols + IDX_DIM:])], axis=1)
    h = rmsnorm(x2, l1_mix_norm)
    p1 = matmul(h, l1_w_in, ncols=main_cols, tm=1024, tn=512, name="l1_proj_main")
    p2 = matmul(h, bf(w_s), tm=1024, tn=w_s.shape[1], name="l1_proj_small")
    o = dsa_mixer_core(p1, p2, b, t, tabs, dsa_q_norm, dsa_k_norm, dsa_ki_norm)
    x2 = matmul(o, l1_w_out, tm=1024, tn=512, res=x2, name="l1_out_proj")
    x2 = tail(x2, l1_xattn_norm, l1_mem_wq, l1_mem_q_norm, l1_mem_wo, l1_ffn_norm, l1_w1, l1_w3, l1_w2)
    return x2.reshape(b, t, d)
```

```python
import functools

import numpy as np
import jax
import jax.numpy as jnp
from jax import lax
from jax.experimental import pallas as pl
from jax.experimental.pallas import tpu as pltpu

F32 = jnp.float32
BF16 = jnp.bfloat16

HEAD_DIM = 128
ROPE_DIM = HEAD_DIM // 4
ROPE_HALF = ROPE_DIM // 2
ROPE_THETA = 500000.0
NORM_EPS = 1e-6
NEG_INF = -1e30

NSA_HEADS = 16
NSA_KV_HEADS = 4
NSA_GROUP = NSA_HEADS // NSA_KV_HEADS
CMP_BLOCK = 32
CMP_STRIDE = 16
SEL_BLOCK = 64
SEL_TOPN = 16
WINDOW = 512
FORCED_SCORE = 1e4

RWKV_WIDTH = 2048
RWKV_HEAD = 64
RWKV_HEADS = RWKV_WIDTH // RWKV_HEAD
LORA_DECAY = 96
LORA_AAA = 96
LORA_GATE = 256
GN_EPS = 64e-5

DSA_HEADS = 32
DSA_KV_HEADS = 4
DSA_GROUP = DSA_HEADS // DSA_KV_HEADS
IDX_HEADS = 32
IDX_DIM = 128
DSA_TOPK_MAX = 256

MEM_HEADS = 4
MEM_WIDTH = MEM_HEADS * HEAD_DIM

LANES = 128
VMEM_BIG = 56 * 1024 * 1024
VMEM_MID = 40 * 1024 * 1024
INT_MIN = -2 ** 31
MASK_BIG = 1e30
FLASH_TK = 512
QK_SCALE = HEAD_DIM ** -0.5 * float(np.log2(np.e))


def _params(sem, vmem=VMEM_MID):
    return pltpu.CompilerParams(dimension_semantics=sem, vmem_limit_bytes=vmem)


def _sigmoid(x):
    return 1.0 / (1.0 + jnp.exp(-x))


def _dot(a, b, preferred_element_type=F32, precision=None):
    return lax.dot_general(a, b, (((1,), (0,)), ((), ())), precision=precision,
                           preferred_element_type=preferred_element_type)


def _nt_dot(a, b):
    return lax.dot_general(a, b, (((1,), (1,)), ((), ())), preferred_element_type=F32)


def _split_dot(x, w_bf16):
    hi = x.astype(BF16)
    lo = (x - hi.astype(F32)).astype(BF16)
    return (_dot(hi, w_bf16, preferred_element_type=F32)
            + _dot(lo, w_bf16, preferred_element_type=F32))


def _rmsnorm_kernel(x_ref, g_ref, o_ref):
    x = x_ref[...]
    ms = jnp.mean(x * x, axis=-1, keepdims=True)
    o_ref[...] = (x * lax.rsqrt(ms + NORM_EPS) * g_ref[...]).astype(o_ref.dtype)


def rmsnorm(x2d, g, tm=256):
    m, d = x2d.shape
    return pl.pallas_call(
        _rmsnorm_kernel,
        grid=(m // tm,),
        in_specs=[pl.BlockSpec((tm, d), lambda i: (i, 0)),
                  pl.BlockSpec((1, d), lambda i: (0, 0))],
        out_specs=pl.BlockSpec((tm, d), lambda i: (i, 0)),
        out_shape=jax.ShapeDtypeStruct((m, d), BF16),
        compiler_params=_params(("parallel",)),
        name="rmsnorm",
    )(x2d, g.reshape(1, d))


def _mm_kernel(*refs, n_a, swiglu, has_res):
    a_refs = refs[:n_a]
    n_b = 2 if swiglu else n_a
    b_refs = refs[n_a:n_a + n_b]
    res_ref = refs[n_a + n_b] if has_res else None
    o_ref = refs[-1]
    if swiglu:
        a = a_refs[0][...]
        gate = _dot(a, b_refs[0][...].astype(BF16))
        y = gate * _sigmoid(gate) * _dot(a, b_refs[1][...].astype(BF16))
    else:
        y = _dot(a_refs[0][...], b_refs[0][...].astype(BF16))
        for a_ref, b_ref in zip(a_refs[1:], b_refs[1:]):
            y = y + _dot(a_ref[...], b_ref[...].astype(BF16))
    if has_res:
        y = y + res_ref[...]
    o_ref[...] = y.astype(o_ref.dtype)


def matmul(a, b, *, tm, tn, swiglu=False, col0=0, ncols=None, res=None, out_dtype=F32, name="matmul"):
    a_list = list(a) if isinstance(a, (list, tuple)) else [a]
    b_list = list(b) if isinstance(b, (list, tuple)) else [b]
    m = a_list[0].shape[0]
    n = b_list[0].shape[1] - col0 if ncols is None else ncols
    tm = min(tm, m)
    tn = min(tn, n)
    assert m % tm == 0 and n % tn == 0 and col0 % tn == 0
    off = col0 // tn
    in_specs = [pl.BlockSpec((tm, ai.shape[1]), lambda i, j: (i, 0)) for ai in a_list]
    in_specs += [pl.BlockSpec((bi.shape[0], tn), lambda i, j: (0, off + j)) for bi in b_list]
    args = a_list + b_list
    if res is not None:
        in_specs.append(pl.BlockSpec((tm, tn), lambda i, j: (i, j)))
        args.append(res)
    return pl.pallas_call(
        functools.partial(_mm_kernel, n_a=len(a_list), swiglu=swiglu, has_res=res is not None),
        grid=(m // tm, n // tn),
        in_specs=in_specs,
        out_specs=pl.BlockSpec((tm, tn), lambda i, j: (i, j)),
        out_shape=jax.ShapeDtypeStruct((m, n), out_dtype),
        compiler_params=_params(("parallel", "parallel"), VMEM_BIG),
        name=name,
    )(*args)


def _rope_tab_kernel(pos_ref, inv_ref, c_ref, sa_ref, sb_ref):
    ang = pos_ref[...].astype(F32) * inv_ref[...]
    c = jnp.cos(ang)
    s = jnp.sin(ang)
    lane = lax.broadcasted_iota(jnp.int32, ang.shape, 1)
    c_ref[...] = jnp.where(lane < ROPE_DIM, c, 1.0)
    sa_ref[...] = jnp.where(lane < ROPE_HALF, -s, 0.0)
    sb_ref[...] = jnp.where((lane >= ROPE_HALF) & (lane < ROPE_DIM), s, 0.0)


def rope_tables(pos_flat, tm=256):
    n = pos_flat.shape[0]
    tm = min(tm, n)
    inv = ROPE_THETA ** (-jnp.arange(0, ROPE_DIM, 2, dtype=F32) / ROPE_DIM)
    inv_row = jnp.concatenate([inv, inv, jnp.zeros((LANES - ROPE_DIM,), F32)]).reshape(1, LANES)
    pos_b = jnp.broadcast_to(pos_flat[:, None], (n, LANES))
    spec = pl.BlockSpec((tm, LANES), lambda i: (i, 0))
    shp = jax.ShapeDtypeStruct((n, LANES), F32)
    return pl.pallas_call(
        _rope_tab_kernel,
        grid=(n // tm,),
        in_specs=[spec, pl.BlockSpec((1, LANES), lambda i: (0, 0))],
        out_specs=[spec, spec, spec],
        out_shape=[shp, shp, shp],
        compiler_params=_params(("parallel",)),
        name="rope_tables",
    )(pos_b, inv_row)


def _head_norm(xh, g):
    ms = jnp.mean(xh * xh, axis=-1, keepdims=True)
    return xh * lax.rsqrt(ms + NORM_EPS) * g


def _rope(xh, c, sa, sb):
    return (xh * c + pltpu.roll(xh, LANES - ROPE_HALF, 1) * sa
            + pltpu.roll(xh, ROPE_HALF, 1) * sb)


def _norm_rope_kernel(*refs, heads, do_norm, do_rope, scale):
    x_ref = refs[0]
    pos = 1
    g = None
    if do_norm:
        g = refs[pos][...]
        pos += 1
    if do_rope:
        c, sa, sb = refs[pos][...], refs[pos + 1][...], refs[pos + 2][...]
        pos += 3
    o_ref = refs[pos]
    for h in range(heads):
        xh = x_ref[:, h * HEAD_DIM:(h + 1) * HEAD_DIM]
        if do_norm:
            xh = _head_norm(xh, g)
        if do_rope:
            xh = _rope(xh, c, sa, sb)
        if scale != 1.0:
            xh = xh * scale
        o_ref[:, h * HEAD_DIM:(h + 1) * HEAD_DIM] = xh.astype(o_ref.dtype)


def norm_rope(x2d, col_start, heads, gain, tabs, scale=1.0, tm=256, name="norm_rope"):
    m = x2d.shape[0]
    hb = heads
    while col_start % (hb * HEAD_DIM):
        hb //= 2
    w = hb * HEAD_DIM
    off = col_start // w
    tm = min(tm, m)
    in_specs = [pl.BlockSpec((tm, w), lambda i, j: (i, off + j))]
    args = [x2d]
    if gain is not None:
        in_specs.append(pl.BlockSpec((1, HEAD_DIM), lambda i, j: (0, 0)))
        args.append(gain.reshape(1, HEAD_DIM))
    if tabs is not None:
        in_specs += [pl.BlockSpec((tm, LANES), lambda i, j: (i, 0))] * 3
        args += list(tabs)
    return pl.pallas_call(
        functools.partial(_norm_rope_kernel, heads=hb, do_norm=gain is not None,
                          do_rope=tabs is not None, scale=scale),
        grid=(m // tm, heads // hb),
        in_specs=in_specs,
        out_specs=pl.BlockSpec((tm, w), lambda i, j: (i, j)),
        out_shape=jax.ShapeDtypeStruct((m, heads * HEAD_DIM), BF16),
        compiler_params=_params(("parallel", "parallel")),
        name=name,
    )(*args)


def _compress_kernel(*refs, is_key):
    if is_key:
        x_ref, w1_ref, w2_ref, pe_ref, g_ref, c_ref, sa_ref, sb_ref, o_ref = refs
    else:
        x_ref, w1_ref, w2_ref, pe_ref, o_ref = refs
    half = CMP_STRIDE * HEAD_DIM
    x = x_ref[0, 0]
    nc = x.shape[0]
    top = _dot(x, w1_ref[:half, :], preferred_element_type=F32)
    bot = _dot(x, w1_ref[half:, :], preferred_element_type=F32)
    pe_term = _dot(pe_ref[...], w1_ref[...], preferred_element_type=F32)[0:1, :]
    pre = top + pltpu.roll(bot, nc - 1, 0) + pe_term
    hid = jax.nn.gelu(pre)
    out = _dot(hid.astype(BF16), w2_ref[...], preferred_element_type=F32)
    if is_key:
        out = _rope(_head_norm(out, g_ref[...]), c_ref[0], sa_ref[0], sb_ref[0])
    o_ref[0, 0] = out.astype(o_ref.dtype)


def compress(xblk, w1, w2, pe, key_extras=None, name="compress"):
    b, h, nc, wdt = xblk.shape
    pe8 = jnp.broadcast_to(pe.reshape(1, CMP_BLOCK * HEAD_DIM), (8, CMP_BLOCK * HEAD_DIM)).astype(BF16)
    in_specs = [pl.BlockSpec((1, 1, nc, wdt), lambda i, j: (i, j, 0, 0)),
                pl.BlockSpec((CMP_BLOCK * HEAD_DIM, HEAD_DIM), lambda i, j: (0, 0)),
                pl.BlockSpec((HEAD_DIM, HEAD_DIM), lambda i, j: (0, 0)),
                pl.BlockSpec((8, CMP_BLOCK * HEAD_DIM), lambda i, j: (0, 0))]
    args = [xblk, w1.astype(BF16), w2.astype(BF16), pe8]
    if key_extras is not None:
        gain, tabs = key_extras
        in_specs.append(pl.BlockSpec((1, HEAD_DIM), lambda i, j: (0, 0)))
        in_specs += [pl.BlockSpec((1, nc, LANES), lambda i, j: (i, 0, 0))] * 3
        args += [gain.reshape(1, HEAD_DIM)] + [t.reshape(b, nc, LANES) for t in tabs]
    out_dtype = BF16
    return pl.pallas_call(
        functools.partial(_compress_kernel, is_key=key_extras is not None),
        grid=(b, h),
        in_specs=in_specs,
        out_specs=pl.BlockSpec((1, 1, nc, HEAD_DIM), lambda i, j: (i, j, 0, 0)),
        out_shape=jax.ShapeDtypeStruct((b, h, nc, HEAD_DIM), out_dtype),
        compiler_params=_params(("parallel", "parallel")),
        name=name,
    )(*args)


def _stack_heads(q_ref, group, tq):
    return jnp.concatenate([q_ref[0, :, g * HEAD_DIM:(g + 1) * HEAD_DIM] for g in range(group)], axis=0)


def _unstack_heads(o_ref, o_t, group, tq):
    o = o_t.T
    for g in range(group):
        o_ref[0, :, g * HEAD_DIM:(g + 1) * HEAD_DIM] = o[g * tq:(g + 1) * tq, :].astype(o_ref.dtype)


def _cmp_kernel(q_ref, kc_ref, vct_ref, mt_ref, o_ref, sel_ref, *, tq, n_sel, topn):
    group = NSA_GROUP
    t0 = pl.program_id(2) * tq
    q = _stack_heads(q_ref, group, tq)
    kc = kc_ref[0, 0]
    nc = kc.shape[0]
    s = _nt_dot(kc, q)
    c_idx = lax.broadcasted_iota(jnp.int32, (nc, tq), 0)
    t_idx = t0 + lax.broadcasted_iota(jnp.int32, (nc, tq), 1)
    ok1 = (CMP_STRIDE * c_idx + CMP_BLOCK - 1) <= t_idx
    bias1 = jnp.where(ok1, 0.0, NEG_INF)
    okf1 = jnp.where(ok1, 1.0, 0.0)
    bias = jnp.concatenate([bias1] * group, axis=1)
    okf = jnp.concatenate([okf1] * group, axis=1)
    s = s + bias
    m = jnp.max(s, axis=0, keepdims=True)
    e = jnp.exp2(s - m) * okf
    l = jnp.sum(e, axis=0, keepdims=True)
    p = e * jnp.where(l > 0.0, 1.0 / l, 0.0)
    o_t = _dot(vct_ref[0, 0], p.astype(BF16), preferred_element_type=F32)
    _unstack_heads(o_ref, o_t, group, tq)

    psum = p[:, 0:tq]
    for g in range(1, group):
        psum = psum + p[:, g * tq:(g + 1) * tq]
    imp = _dot(mt_ref[...], psum, preferred_element_type=F32,
                  precision=lax.Precision.HIGHEST)
    j_idx = lax.broadcasted_iota(jnp.int32, (n_sel, tq), 0)
    jt = (t0 + lax.broadcasted_iota(jnp.int32, (n_sel, tq), 1)) // SEL_BLOCK
    forced = (j_idx == 0) | (j_idx == jt) | (j_idx == jt - 1)
    imp = jnp.where(forced, FORCED_SCORE, imp)
    imp = jnp.where(j_idx <= jt, imp, -jnp.inf)
    rank = jnp.zeros((n_sel, tq), F32)
    for jp in range(n_sel):
        row = imp[jp:jp + 1, :]
        before = (row > imp) | ((row == imp) & (j_idx > jp))
        rank = rank + jnp.where(before, 1.0, 0.0)
    bias_t = jnp.where(rank < topn, 0.0, -MASK_BIG)
    bias_t = jnp.concatenate([bias_t, jnp.zeros((LANES - n_sel, tq), F32)], axis=0)
    sel_ref[0, 0] = bias_t.T.astype(sel_ref.dtype)


def nsa_compressed(qn, kc, vct, b, t, tq=256):
    nc = kc.shape[2]
    n_sel = t // SEL_BLOCK
    assert n_sel <= LANES
    topn = min(SEL_TOPN, n_sel)
    tq = min(tq, t)
    cs = CMP_STRIDE * np.arange(nc)[:, None]
    ss = SEL_BLOCK * np.arange(n_sel)[None, :]
    ov = np.clip(np.minimum(cs + CMP_BLOCK, ss + SEL_BLOCK) - np.maximum(cs, ss), 0, None) / CMP_BLOCK
    ov[nc - 1, :] = 0.0
    mt = jnp.asarray(ov.T, dtype=F32)
    gw = NSA_GROUP * HEAD_DIM
    return pl.pallas_call(
        functools.partial(_cmp_kernel, tq=tq, n_sel=n_sel, topn=topn),
        grid=(b, NSA_KV_HEADS, t // tq),
        in_specs=[pl.BlockSpec((1, tq, gw), lambda i, h, j: (i, j, h)),
                  pl.BlockSpec((1, 1, nc, HEAD_DIM), lambda i, h, j: (i, h, 0, 0)),
                  pl.BlockSpec((1, 1, HEAD_DIM, nc), lambda i, h, j: (i, h, 0, 0)),
                  pl.BlockSpec((n_sel, nc), lambda i, h, j: (0, 0))],
        out_specs=[pl.BlockSpec((1, tq, gw), lambda i, h, j: (i, j, h)),
                   pl.BlockSpec((1, 1, tq, LANES), lambda i, h, j: (i, h, j, 0))],
        out_shape=[jax.ShapeDtypeStruct((b, t, NSA_HEADS * HEAD_DIM), F32),
                   jax.ShapeDtypeStruct((b, NSA_KV_HEADS, t, LANES), BF16)],
        compiler_params=_params(("parallel", "parallel", "parallel")),
        name="nsa_compressed",
    )(qn, kc, vct, mt)


def _flash_kernel(*refs, mode, group, slab, tq, tk):
    if mode == "window":
        q_ref, k_ref, vt_ref, o_ref, q_s, m_s, l_s, acc_s, sa_s, sb_s, p_s = refs
        xq = None
    else:
        q_ref, k_ref, vt_ref, xq_ref, xk_ref, o_ref, q_s, m_s, l_s, acc_s, sa_s, sb_s, p_s = refs
        xq = xq_ref[0, 0] if mode == "sel" else xq_ref[...]
    qi = pl.program_id(2)
    t0 = qi * tq
    q = _stack_heads(q_ref, group, tq)
    if xq is not None:
        q = jnp.concatenate([q, jnp.concatenate([xq] * group, axis=0)], axis=1)
    q_s[...] = q
    m_s[...] = jnp.full(m_s.shape, NEG_INF, F32)
    l_s[...] = jnp.zeros(l_s.shape, F32)
    acc_s[...] = jnp.zeros(acc_s.shape, F32)

    def scores(j):
        k0 = pl.multiple_of(j * tk, tk)
        kt = k_ref[0, pl.ds(k0, tk), :]
        if mode == "sel":
            kt = jnp.concatenate([kt, xk_ref[pl.ds(k0, tk), :]], axis=1)
        elif mode == "dsa":
            kt = jnp.concatenate([kt, xk_ref[0, pl.ds(k0, tk), :]], axis=1)
        return _nt_dot(kt, q_s[...])

    def absorb(s_ref, j, keep):
        if keep is not None:
            kpos = j * tk + lax.broadcasted_iota(jnp.int32, (tk, tq), 0)
            tpos = t0 + lax.broadcasted_iota(jnp.int32, (tk, tq), 1)
            ok = keep(kpos, tpos)
            for g in range(group):
                cols = slice(g * tq, (g + 1) * tq)
                s_ref[:, cols] = jnp.where(ok, s_ref[:, cols], NEG_INF)
        m_old = m_s[...]
        m_new = jnp.maximum(m_old, jnp.max(s_ref[...], axis=0, keepdims=True))
        alpha = jnp.exp2(m_old - m_new)
        p = jnp.exp2(s_ref[...] - m_new)
        l_s[...] = alpha * l_s[...] + jnp.sum(p, axis=0, keepdims=True)
        p_s[...] = p.astype(BF16)
        acc_s[...] = acc_s[...] * alpha + _dot(vt_ref[0, 0, j], p_s[...])
        m_s[...] = m_new

    causal = lambda kpos, tpos: kpos <= tpos

    if mode == "window":
        nw = WINDOW // tk

        @pl.when(qi >= nw)
        def _():
            sa_s[...] = scores(qi - nw)
            absorb(sa_s, qi - nw, lambda kpos, tpos: tpos - kpos < WINDOW)

        for dist in range(nw - 1, 0, -1):
            @pl.when(qi >= dist)
            def _():
                sa_s[...] = scores(qi - dist)
                absorb(sa_s, qi - dist, None)

        sa_s[...] = scores(qi)
        absorb(sa_s, qi, causal)
    else:
        last = (t0 + tq - 1) // tk
        last_keep = causal if mode == "sel" else None
        sa_s[...] = scores(0)

        def pair(i, carry):
            j = 2 * i
            sb_s[...] = scores(j + 1)
            absorb(sa_s, j, None)
            sa_s[...] = scores(j + 2)
            absorb(sb_s, j + 1, None)
            return carry

        lax.fori_loop(0, last // 2, pair, 0)

        @pl.when(last % 2 == 1)
        def _():
            sb_s[...] = scores(last)
            absorb(sa_s, last - 1, None)
            absorb(sb_s, last, last_keep)

        @pl.when(last % 2 == 0)
        def _():
            absorb(sa_s, last, last_keep)

    _unstack_heads(o_ref, acc_s[...] * (1.0 / l_s[...]), group, tq)


def flash_masked(qn, kn, vt5, *, mode, group, b, t, tq, slab=None, extra=None, out_dtype=F32, name="flash"):
    kvh = kn.shape[2] // HEAD_DIM
    gw = group * HEAD_DIM
    tq = min(tq, t)
    ntk, tk = vt5.shape[2], vt5.shape[4]
    slab = group if slab is None else slab
    in_specs = [pl.BlockSpec((1, tq, gw), lambda i, h, qi: (i, qi, h)),
                pl.BlockSpec((1, t, HEAD_DIM), lambda i, h, qi: (i, 0, h)),
                pl.BlockSpec((1, 1, ntk, HEAD_DIM, tk), lambda i, h, qi: (i, h, 0, 0, 0))]
    args = [qn, kn, vt5]
    kdim = HEAD_DIM
    if mode == "sel":
        assert tk % tq == 0
        et = np.zeros((t, LANES), np.float32)
        et[np.arange(t), np.arange(t) // SEL_BLOCK] = 1.0
        in_specs += [pl.BlockSpec((1, 1, tq, LANES), lambda i, h, qi: (i, h, qi, 0)),
                     pl.BlockSpec((t, LANES), lambda i, h, qi: (0, 0))]
        args += [extra, jnp.asarray(et, dtype=BF16)]
        kdim += LANES
    elif mode == "dsa":
        assert tq == LANES
        in_specs += [pl.BlockSpec((tq, LANES), lambda i, h, qi: (0, 0)),
                     pl.BlockSpec((1, t, tq), lambda i, h, qi: (i, 0, qi))]
        args += [jnp.eye(tq, dtype=BF16), extra]
        kdim += LANES
    else:
        assert tk == tq and WINDOW % tk == 0
    return pl.pallas_call(
        functools.partial(_flash_kernel, mode=mode, group=group, slab=slab, tq=tq, tk=tk),
        grid=(b, kvh, t // tq),
        in_specs=in_specs,
        out_specs=pl.BlockSpec((1, tq, gw), lambda i, h, qi: (i, qi, h)),
        out_shape=jax.ShapeDtypeStruct((b, t, kvh * gw), out_dtype),
        scratch_shapes=[pltpu.VMEM((group * tq, kdim), BF16),
                        pltpu.VMEM((1, group * tq), F32),
                        pltpu.VMEM((1, group * tq), F32),
                        pltpu.VMEM((HEAD_DIM, group * tq), F32),
                        pltpu.VMEM((tk, group * tq), F32),
                        pltpu.VMEM((tk, group * tq), F32),
                        pltpu.VMEM((tk, group * tq), BF16)],
        compiler_params=_params(("parallel", "parallel", "parallel"), VMEM_BIG),
        name=name,
    )(*args)


def _nsa_merge_kernel(oc_ref, os_ref, ow_ref, g_ref, e_ref, o_ref):
    gate = _sigmoid(g_ref[...])
    out = None
    for j, src in enumerate((oc_ref, os_ref, ow_ref)):
        gj = _split_dot(gate, e_ref[j])
        term = gj * src[...]
        out = term if out is None else out + term
    o_ref[...] = out.astype(o_ref.dtype)


def nsa_merge(oc, os_, ow, pc, tm=256):
    m, w = oc.shape
    tm = min(tm, m)
    e = np.zeros((3, LANES, w), np.float32)
    for h in range(NSA_HEADS):
        for j in range(3):
            e[j, h * 3 + j, h * HEAD_DIM:(h + 1) * HEAD_DIM] = 1.0
    spec = pl.BlockSpec((tm, w), lambda i: (i, 0))
    return pl.pallas_call(
        _nsa_merge_kernel,
        grid=(m // tm,),
        in_specs=[spec, spec, spec,
                  pl.BlockSpec((tm, LANES), lambda i: (i, 0)),
                  pl.BlockSpec((3, LANES, w), lambda i: (0, 0, 0))],
        out_specs=spec,
        out_shape=jax.ShapeDtypeStruct((m, w), BF16),
        compiler_params=_params(("parallel",)),
        name="nsa_merge",
    )(oc, os_, ow, pc, jnp.asarray(e, dtype=BF16))


SUB = 8
HEADS_PER_SLAB = LANES // RWKV_HEADS
N_SLABS = RWKV_WIDTH // LANES


def _to_jh(p):
    lead = p.shape[:-1]
    return p.reshape(*lead, RWKV_HEADS, RWKV_HEAD).swapaxes(-1, -2).reshape(*lead, RWKV_WIDTH)


def _lane_table(p, nb):
    tab = p.reshape(RWKV_HEADS, RWKV_HEAD).T
    return jnp.tile(tab, (1, nb)).reshape(RWKV_HEAD, 1, LANES)


def _head_sum_jh(x, bd):
    acc = x[:, 0:LANES]
    for s in range(1, N_SLABS):
        acc = acc + x[:, s * LANES:(s + 1) * LANES]
    tot = _split_dot(acc, bd)
    return jnp.concatenate([tot] * N_SLABS, axis=1)


def _interleave(xs, out_ref):
    nb = len(xs)
    tt = xs[0].shape[0]
    group = (lax.broadcasted_iota(jnp.int32, xs[0].shape, 1) % LANES) // RWKV_HEADS
    moved = {}
    for delta in range(1 - nb, nb):
        if delta == 0:
            continue
        src = None
        for g in range(max(0, -delta), min(nb, nb - delta)):
            src = xs[g + delta] if src is None else jnp.where(group == g, xs[g + delta], src)
        moved[delta] = pltpu.roll(src, (delta * RWKV_HEADS) % RWKV_WIDTH, 1)
    for jj in range(HEADS_PER_SLAB):
        z = xs[jj]
        for bi in range(nb):
            if bi != jj:
                z = jnp.where(group == bi, moved[bi - jj], z)
        for tb in range(tt // SUB):
            for s in range(N_SLABS):
                out_ref[tb, HEADS_PER_SLAB * s + jj] = z[tb * SUB:(tb + 1) * SUB, s * LANES:(s + 1) * LANES]


def _deinterleave(val, o_ref):
    nb = o_ref.shape[0]
    tt = o_ref.shape[1]
    zs = []
    for jj in range(HEADS_PER_SLAB):
        z = jnp.concatenate([val[:, HEADS_PER_SLAB * s + jj] for s in range(N_SLABS)], axis=-1)
        zs.append(z.reshape(tt, RWKV_WIDTH))
    lane = lax.broadcasted_iota(jnp.int32, (tt, RWKV_WIDTH), 1) % LANES
    for bi in range(nb):
        out = None
        for jj in range(HEADS_PER_SLAB):
            shift = ((jj - bi) * RWKV_HEADS) % RWKV_WIDTH
            zj = zs[jj] if shift == 0 else pltpu.roll(zs[jj], shift, 1)
            out = zj if out is None else jnp.where(lane // RWKV_HEADS == jj, zj, out)
        o_ref[bi] = out.astype(o_ref.dtype)


def _shifted(x, prev_row, first):
    rolled = pltpu.roll(x, 1, 0)
    row0 = jnp.where(first, 0.0, prev_row)
    ridx = lax.broadcasted_iota(jnp.int32, x.shape, 0)
    return jnp.where(ridx == 0, row0, rolled)


def _rwkv_prep_kernel(pb_ref, pbp_ref, pc_ref, pcp_ref, mub_ref, muc_ref, w0_ref, w2_ref, a0_ref, a2_ref,
                      g2_ref, kkg_ref, ka_ref, bd_ref,
                      r_ref, w_ref, k_ref, v_ref, kk_ref, b_ref, g_ref):
    first = pl.program_id(0) == 0
    w = RWKV_WIDTH
    nb = pb_ref.shape[0]
    outs = [[] for _ in range(7)]
    for bi in range(nb):
        xb = pb_ref[bi]
        xb = xb + (_shifted(xb, pbp_ref[bi, SUB - 1:SUB, :], first) - xb) * mub_ref[...]
        xc = pc_ref[bi]
        xc = xc + (_shifted(xc, pcp_ref[bi, SUB - 1:SUB, :], first) - xc) * muc_ref[...]
        r, k, v = xb[:, :w], xb[:, w:2 * w], xb[:, 2 * w:]
        wd, ad, gd = xc[:, LANES:2 * LANES], xc[:, 2 * LANES:3 * LANES], xc[:, 3 * LANES:]
        z = w0_ref[...] + _dot(jnp.tanh(wd).astype(BF16), w2_ref[...])
        nz = -z
        softplus = jnp.maximum(nz, 0.0) + jnp.log(1.0 + jnp.exp(-jnp.abs(nz)))
        w_log = -softplus - 0.5
        a = _sigmoid(a0_ref[...] + _dot(ad.astype(BF16), a2_ref[...]))
        g = _dot(_sigmoid(gd).astype(BF16), g2_ref[...])
        kkv = k * kkg_ref[...]
        norm = jnp.sqrt(_head_sum_jh(kkv * kkv, bd_ref[...]))
        kkv = kkv / jnp.maximum(norm, 1e-12)
        vals = (r, jnp.exp(-jnp.exp(w_log)), k * (1.0 + (a - 1.0) * ka_ref[...]), v, kkv, kkv * a, g)
        for lst, val in zip(outs, vals):
            lst.append(val)
    for lst, ref in zip(outs, (r_ref, w_ref, k_ref, v_ref, kk_ref, b_ref, g_ref)):
        _interleave(lst, ref)


def _pad_rows(wm, rows):
    return jnp.pad(wm, ((0, rows - wm.shape[0]), (0, 0)))


def _same_head_ones():
    idx = np.arange(LANES) % RWKV_HEADS
    return jnp.asarray(idx[:, None] == idx[None, :], dtype=BF16)


def rwkv_prep(pb, pc, b, t, mu, w0, w2, a0, a2, g2, kk_gain, k_a, tt=32):
    w = RWKV_WIDTH
    assert b * RWKV_HEADS == LANES and t % tt == 0 and tt % SUB == 0
    mu_r, mu_k, mu_v, mu_wd, mu_ad, mu_gd = jnp.split(
        mu, [int(x) for x in np.cumsum([w, w, w, LORA_DECAY, LORA_AAA])])
    mub = jnp.concatenate([_to_jh(mu_r), _to_jh(mu_k), _to_jh(mu_v)]).reshape(1, 3 * w)
    pad = lambda z: jnp.pad(z, (0, LANES - z.shape[0]))
    muc = jnp.concatenate([jnp.zeros((LANES,), F32), pad(mu_wd), pad(mu_ad), mu_gd]).reshape(1, -1)
    cw = pc.shape[1]
    row = lambda z: _to_jh(z).reshape(1, w)
    lora = lambda wm, rows: _to_jh(_pad_rows(wm, rows)).astype(BF16)
    full = lambda shape: pl.BlockSpec(shape, lambda i: (0,) * len(shape))
    tile = lambda width: pl.BlockSpec((b, tt, width), lambda i: (0, i, 0))
    prev = lambda width: pl.BlockSpec((b, SUB, width), lambda i: (0, jnp.maximum(i * (tt // SUB) - 1, 0), 0))
    out_spec = pl.BlockSpec((tt // SUB, RWKV_HEAD, SUB, LANES), lambda i: (i, 0, 0, 0))
    shp = jax.ShapeDtypeStruct((t // SUB, RWKV_HEAD, SUB, LANES), F32)
    pb3 = pb.reshape(b, t, 3 * w)
    pc3 = pc.reshape(b, t, cw)
    return pl.pallas_call(
        _rwkv_prep_kernel,
        grid=(t // tt,),
        in_specs=[tile(3 * w), prev(3 * w), tile(cw), prev(cw), full((1, 3 * w)), full((1, cw)),
                  full((1, w)), full((LANES, w)), full((1, w)), full((LANES, w)), full((LORA_GATE, w)),
                  full((1, w)), full((1, w)), full((LANES, LANES))],
        out_specs=[out_spec] * 7,
        out_shape=[shp] * 7,
        compiler_params=_params(("parallel",), VMEM_BIG),
        name="rwkv_prep",
    )(pb3, pb3, pc3, pc3, mub, muc, row(w0), lora(w2, LANES), row(a0), lora(a2, LANES), lora(g2, LORA_GATE),
      row(kk_gain), row(k_a), _same_head_ones())


def _rwkv_scan_kernel(w_ref, kk_ref, b_ref, k_ref, r_ref, v_ref, y_ref, s_ref, op_s, *, tb, rows):
    ig = pl.program_id(1)
    blk = SUB * RWKV_HEAD

    @pl.when(pl.program_id(0) == 0)
    def _():
        s_ref[ig] = jnp.zeros(s_ref.shape[1:], F32)

    @pl.when(ig == 0)
    def _():
        def gather(u, carry):
            base = pl.multiple_of(u * blk, blk)
            for q in range(SUB):
                key = pl.ds(base + q, RWKV_HEAD, stride=SUB)
                for n, ref in enumerate((w_ref, kk_ref, b_ref, k_ref, r_ref)):
                    op_s[n, u * SUB + q] = ref[key, :]
            return carry
        lax.fori_loop(0, tb // SUB, gather, 0)

    def steps(u, s):
        base = pl.multiple_of(u * blk, blk) + ig * (rows * SUB)
        for q in range(SUB):
            t = u * SUB + q
            val = pl.ds(base + q, rows, stride=SUB)
            sa = -jnp.sum(s * op_s[1, t][None], axis=1)
            s = (s * op_s[0, t][None] + sa[:, None, :] * op_s[2, t][None]
                 + v_ref[val, :][:, None, :] * op_s[3, t][None])
            y_ref[val, :] = jnp.sum(s * op_s[4, t][None], axis=1)
        return s

    s_ref[ig] = lax.fori_loop(0, tb // SUB, steps, s_ref[ig])


def rwkv_scan(w, kk, bb, k, r, v, tb=64, rows=4):
    n = RWKV_HEAD
    t = w.shape[0] // n
    groups = n // rows
    tb = min(tb, t)
    op = pl.BlockSpec((tb * n, LANES), lambda i, j: (i, 0))
    return pl.pallas_call(
        functools.partial(_rwkv_scan_kernel, tb=tb, rows=rows),
        grid=(t // tb, groups),
        in_specs=[op] * 6,
        out_specs=op,
        out_shape=jax.ShapeDtypeStruct((t * n, LANES), F32),
        scratch_shapes=[pltpu.VMEM((groups, rows, n, LANES), F32),
                        pltpu.VMEM((5, tb, n, LANES), F32)],
        compiler_params=_params(("arbitrary", "arbitrary"), VMEM_BIG),
        name="rwkv_scan",
    )(w, kk, bb, k, r, v)


def _rwkv_post_kernel(y_ref, r_ref, k_ref, v_ref, g_ref, rk_ref, gg_ref, gb_ref, o_ref):
    y = y_ref[...]
    inv_n = 1.0 / RWKV_HEAD
    mean = jnp.sum(y, axis=1, keepdims=True) * inv_n
    d = y - mean
    var = jnp.sum(d * d, axis=1, keepdims=True) * inv_n
    yn = d * lax.rsqrt(var + GN_EPS) * gg_ref[...][None] + gb_ref[...][None]
    bonus = jnp.sum(r_ref[...] * k_ref[...] * rk_ref[...][None], axis=1, keepdims=True) * v_ref[...]
    _deinterleave((yn + bonus) * g_ref[...], o_ref)


def rwkv_post(y, r, k, v, g, b, t, r_k, gn_g, gn_b, tt=32):
    tile = pl.BlockSpec((tt // SUB, RWKV_HEAD, SUB, LANES), lambda i: (i, 0, 0, 0))
    tab = pl.BlockSpec((RWKV_HEAD, 1, LANES), lambda i: (0, 0, 0))
    return pl.pallas_call(
        _rwkv_post_kernel,
        grid=(t // tt,),
        in_specs=[tile] * 5 + [tab] * 3,
        out_specs=pl.BlockSpec((b, tt, RWKV_WIDTH), lambda i: (0, i, 0)),
        out_shape=jax.ShapeDtypeStruct((b, t, RWKV_WIDTH), BF16),
        compiler_params=_params(("parallel",)),
        name="rwkv_post",
    )(y, r, k, v, g, _lane_table(r_k.reshape(-1), b), _lane_table(gn_g, b), _lane_table(gn_b, b))


def rwkv7(pb, pc, b, t, mu, w0, w2, a0, a2, g2, kk_gain, k_a, r_k, gn_g, gn_b):
    r, w, k, v, kk, bb, g = rwkv_prep(pb, pc, b, t, mu, w0, w2, a0, a2, g2, kk_gain, k_a)
    rows2d = lambda z: z.reshape(t * RWKV_HEAD, LANES)
    y = rwkv_scan(rows2d(w), rows2d(kk), rows2d(bb), rows2d(k), rows2d(r), rows2d(v))
    y = y.reshape(t // SUB, RWKV_HEAD, SUB, LANES)
    return rwkv_post(y, r, k, v, g, b, t, r_k, gn_g, gn_b).reshape(b * t, RWKV_WIDTH)


def _dsa_select_kernel(qi_ref, ki_ref, wt_ref, bias_ref, q_s, key_s, *, t, tq, kc, topk):
    t0 = pl.program_id(1) * tq
    nchunk = (t0 + tq - 1) // kc + 1
    for h in range(IDX_HEADS):
        q_s[h * tq:(h + 1) * tq, :] = qi_ref[0, :, h * IDX_DIM:(h + 1) * IDX_DIM]
    wt = wt_ref[0]

    def chunk(c, carry):
        k0 = pl.multiple_of(c * kc, kc)
        lg = _nt_dot(ki_ref[0, pl.ds(k0, kc), :], q_s[...])
        acc = jnp.zeros((kc, tq), F32)
        for h in range(IDX_HEADS):
            acc = acc + jnp.maximum(lg[:, h * tq:(h + 1) * tq], 0.0) * wt[h:h + 1, :]
        spos = k0 + lax.broadcasted_iota(jnp.int32, (kc, tq), 0)
        tpos = t0 + lax.broadcasted_iota(jnp.int32, (kc, tq), 1)
        score = jnp.where(spos <= tpos, acc + 0.0, -jnp.inf)
        bits = pltpu.bitcast(score, jnp.int32)
        key_s[pl.ds(k0, kc), :] = jnp.where(bits < 0, bits ^ jnp.int32(0x7FFFFFFF), bits)
        return carry

    lax.fori_loop(0, nchunk, chunk, 0)
    kf = jnp.float32(topk)
    part = 64

    def count(pred):
        def body(c, acc):
            k0 = pl.multiple_of(c * kc, kc)
            ind = jnp.where(pred(key_s[pl.ds(k0, kc), :], k0), 1.0, 0.0)
            return acc + jnp.sum(ind.reshape(kc // part, part, tq), axis=0)
        acc = lax.fori_loop(0, nchunk, body, jnp.zeros((part, tq), F32))
        return jnp.sum(acc, axis=0, keepdims=True)

    def value_bit(it, tau_u):
        cand_u = tau_u | lax.shift_left(jnp.int32(1), 31 - it)
        cand_s = cand_u ^ jnp.int32(INT_MIN)
        return jnp.where(count(lambda blk, k0: blk >= cand_s) >= kf, cand_u, tau_u)

    tau_u = lax.fori_loop(0, 32, value_bit, jnp.zeros((1, tq), jnp.int32))
    tau = tau_u ^ jnp.int32(INT_MIN)
    need = kf - count(lambda blk, k0: blk > tau)
    n_eq = count(lambda blk, k0: blk == tau)
    nbits = int(np.log2(t))

    def tie_search():
        def index_bit(it, jj):
            cand = jj | lax.shift_left(jnp.int32(1), nbits - 1 - it)

            def pred(blk, k0):
                spos = k0 + lax.broadcasted_iota(jnp.int32, (kc, tq), 0)
                return (blk == tau) & (spos < cand)
            return jnp.where(count(pred) < need, cand, jj)
        return lax.fori_loop(0, nbits, index_bit, jnp.zeros((1, tq), jnp.int32))

    jj = lax.cond(jnp.max(n_eq - need) > 0.0, tie_search, lambda: jnp.full((1, tq), t, jnp.int32))

    def emit(c, carry):
        k0 = pl.multiple_of(c * kc, kc)
        blk = key_s[pl.ds(k0, kc), :]
        spos = k0 + lax.broadcasted_iota(jnp.int32, (kc, tq), 0)
        tpos = t0 + lax.broadcasted_iota(jnp.int32, (kc, tq), 1)
        keep = ((blk > tau) | ((blk == tau) & (spos <= jj))) & (spos <= tpos)
        bias_ref[0, pl.ds(k0, kc), :] = jnp.where(keep, 0.0, -MASK_BIG).astype(bias_ref.dtype)
        return carry

    lax.fori_loop(0, nchunk, emit, 0)

    def fill(c, carry):
        k0 = pl.multiple_of(c * kc, kc)
        bias_ref[0, pl.ds(k0, kc), :] = jnp.full((kc, tq), -MASK_BIG, bias_ref.dtype)
        return carry

    lax.fori_loop(nchunk, t // kc, fill, 0)


def dsa_select(qi, ki, wt, b, t, tq=128, kc=512):
    topk = min(DSA_TOPK_MAX, t // 4)
    kc = min(kc, t)
    assert topk <= kc and t % kc == 0
    return pl.pallas_call(
        functools.partial(_dsa_select_kernel, t=t, tq=tq, kc=kc, topk=topk),
        grid=(b, t // tq),
        in_specs=[pl.BlockSpec((1, tq, IDX_HEADS * IDX_DIM), lambda i, j: (i, j, 0)),
                  pl.BlockSpec((1, t, IDX_DIM), lambda i, j: (i, 0, 0)),
                  pl.BlockSpec((1, IDX_HEADS, tq), lambda i, j: (i, 0, j))],
        out_specs=pl.BlockSpec((1, t, tq), lambda i, j: (i, 0, j)),
        out_shape=jax.ShapeDtypeStruct((b, t, t), BF16),
        scratch_shapes=[pltpu.VMEM((IDX_HEADS * tq, IDX_DIM), BF16),
                        pltpu.VMEM((t, tq), jnp.int32)],
        compiler_params=_params(("parallel", "parallel"), VMEM_BIG),
        name="dsa_select",
    )(qi, ki, wt)


def _xattn_kernel(q_ref, g_ref, mk_ref, mv_ref, o_ref):
    scale = HEAD_DIM ** -0.5
    for h in range(MEM_HEADS):
        sl = slice(h * HEAD_DIM, (h + 1) * HEAD_DIM)
        qh = (_head_norm(q_ref[:, sl], g_ref[...]) * scale).astype(BF16)
        s = _nt_dot(qh, mk_ref[0, :, sl])
        m = jnp.max(s, axis=-1, keepdims=True)
        e = jnp.exp(s - m)
        p = e / jnp.sum(e, axis=-1, keepdims=True)
        o_ref[:, sl] = _dot(p.astype(BF16), mv_ref[0, :, sl],
                               preferred_element_type=F32).astype(o_ref.dtype)


def mem_xattn(qm, gain, mk, mv, b, t, tq=256):
    tq = min(tq, t)
    nt = t // tq
    mtok = mk.shape[1]
    return pl.pallas_call(
        _xattn_kernel,
        grid=(b, nt),
        in_specs=[pl.BlockSpec((tq, MEM_WIDTH), lambda i, j: (i * nt + j, 0)),
                  pl.BlockSpec((1, HEAD_DIM), lambda i, j: (0, 0)),
                  pl.BlockSpec((1, mtok, MEM_WIDTH), lambda i, j: (i, 0, 0)),
                  pl.BlockSpec((1, mtok, MEM_WIDTH), lambda i, j: (i, 0, 0))],
        out_specs=pl.BlockSpec((tq, MEM_WIDTH), lambda i, j: (i * nt + j, 0)),
        out_shape=jax.ShapeDtypeStruct((b * t, MEM_WIDTH), BF16),
        compiler_params=_params(("parallel", "parallel")),
        name="mem_xattn",
    )(qm, gain.reshape(1, HEAD_DIM), mk, mv)


def _kv_transposed(x2d, lo, b, t, heads, tk=FLASH_TK):
    tk = min(tk, t)
    v = x2d[:, lo:lo + heads * HEAD_DIM].astype(BF16)
    return v.reshape(b, t // tk, tk, heads, HEAD_DIM).transpose(0, 3, 1, 4, 2)


def nsa_mixer(pa, pc, b, t, tabs, cmp_tabs, q_norm, kc_norm, ks_norm, kw_norm, pe_k, pe_v,
              ck_w1, ck_w2, cv_w1, cv_w2):
    kvw = NSA_KV_HEADS * HEAD_DIM
    qn = norm_rope(pa, 0, NSA_HEADS, q_norm, tabs, scale=QK_SCALE, name="nsa_q_prep")
    ksn = norm_rope(pa, 6 * kvw, NSA_KV_HEADS, ks_norm, tabs, name="nsa_ks_prep").reshape(b, t, kvw)
    kwn = norm_rope(pa, 8 * kvw, NSA_KV_HEADS, kw_norm, tabs, name="nsa_kw_prep").reshape(b, t, kvw)
    vst = _kv_transposed(pa, 7 * kvw, b, t, NSA_KV_HEADS)
    vwt = _kv_transposed(pa, 9 * kvw, b, t, NSA_KV_HEADS)
    qn = qn.reshape(b, t, NSA_HEADS * HEAD_DIM)

    nc = t // CMP_STRIDE

    def chunked(lo):
        z = pa[:, lo:lo + kvw].astype(BF16).reshape(b, nc, CMP_STRIDE, NSA_KV_HEADS, HEAD_DIM)
        return z.transpose(0, 3, 1, 2, 4).reshape(b, NSA_KV_HEADS, nc, CMP_STRIDE * HEAD_DIM)

    kc = compress(chunked(4 * kvw), ck_w1, ck_w2, pe_k, key_extras=(kc_norm, cmp_tabs), name="compress_k")
    vc = compress(chunked(5 * kvw), cv_w1, cv_w2, pe_v, name="compress_v")
    vct = vc.transpose(0, 1, 3, 2)
    o_c, sel_bias = nsa_compressed(qn, kc, vct, b, t)
    o_s = flash_masked(qn, ksn, vst, mode="sel", group=NSA_GROUP, b=b, t=t, tq=256,
                       extra=sel_bias, name="nsa_selected")
    o_w = flash_masked(qn, kwn, vwt, mode="window", group=NSA_GROUP, b=b, t=t, tq=FLASH_TK,
                       name="nsa_window")
    m = b * t
    w = NSA_HEADS * HEAD_DIM
    return nsa_merge(o_c.reshape(m, w), o_s.reshape(m, w), o_w.reshape(m, w), pc)


def dsa_mixer_core(p1, p2, b, t, tabs, q_norm, k_norm, ki_norm):
    qw = DSA_HEADS * HEAD_DIM
    kvw = DSA_KV_HEADS * HEAD_DIM
    qn = norm_rope(p1, 0, DSA_HEADS, q_norm, tabs, scale=QK_SCALE, name="dsa_q_prep").reshape(b, t, qw)
    kn = norm_rope(p1, qw, DSA_KV_HEADS, k_norm, tabs, name="dsa_k_prep").reshape(b, t, kvw)
    vt = _kv_transposed(p1, qw + kvw, b, t, DSA_KV_HEADS)
    qi = norm_rope(p1, qw + 2 * kvw, IDX_HEADS, None, tabs, name="dsa_qi_prep")
    qi = qi.reshape(b, t, IDX_HEADS * IDX_DIM)
    ki = norm_rope(p2, 0, 1, ki_norm, tabs, name="dsa_ki_prep").reshape(b, t, IDX_DIM)
    wscale = IDX_HEADS ** -0.5 * IDX_DIM ** -0.5
    wt = (p2[:, IDX_DIM:IDX_DIM + IDX_HEADS] * wscale).reshape(b, t, IDX_HEADS).transpose(0, 2, 1)
    key_bias = dsa_select(qi, ki, wt, b, t)
    o = flash_masked(qn, kn, vt, mode="dsa", group=DSA_GROUP, b=b, t=t, tq=LANES,
                     extra=key_bias, out_dtype=BF16, name="dsa_attention")
    return o.reshape(b * t, qw)


def kernel(x, mem, positions, mem_norm, mem_w_kv, mem_k_norm, l0_mix_norm, l0_w_in, l0_w_out, nsa_q_norm, nsa_kc_norm, nsa_ks_norm, nsa_kw_norm, nsa_pe_k, nsa_pe_v, nsa_ck_w1, nsa_ck_w2, nsa_cv_w1, nsa_cv_w2, rwkv_mu, rwkv_w0, rwkv_w2, rwkv_a0, rwkv_a2, rwkv_g2, rwkv_kk, rwkv_ka, rwkv_rk, rwkv_gn_g, rwkv_gn_b, l0_xattn_norm, l0_mem_wq, l0_mem_q_norm, l0_mem_wo, l0_ffn_norm, l0_w1, l0_w3, l0_w2, l1_mix_norm, l1_w_in, l1_w_out, dsa_q_norm, dsa_k_norm, dsa_ki_norm, l1_xattn_norm, l1_mem_wq, l1_mem_q_norm, l1_mem_wo, l1_ffn_norm, l1_w1, l1_w3, l1_w2):
    b, t, d = x.shape
    m = b * t
    bf = lambda z: z.astype(BF16)
    x2 = x.reshape(m, d)

    tabs = rope_tables(positions.reshape(m))
    nc = t // CMP_STRIDE
    cmp_pos = positions[:, CMP_BLOCK - 1::CMP_STRIDE]
    cmp_pos = jnp.concatenate([cmp_pos, cmp_pos[:, -1:]], axis=1)
    cmp_tabs = rope_tables(cmp_pos.reshape(b * nc))

    mtok = mem.shape[1]
    memn = rmsnorm(mem.reshape(b * mtok, d), mem_norm)
    mkv = matmul(memn, mem_w_kv, tm=512, tn=512, name="mem_kv_proj")
    mk = norm_rope(mkv, 0, MEM_HEADS, mem_k_norm, None, name="mem_k_norm").reshape(b, mtok, MEM_WIDTH)
    mv = bf(mkv[:, MEM_WIDTH:]).reshape(b, mtok, MEM_WIDTH)

    def tail(x2, xn, wq, qn, wo, fn, w1, w3, w2):
        h = rmsnorm(x2, xn)
        qm = matmul(h, wq, tm=1024, tn=512, name="xattn_q_proj")
        o = mem_xattn(qm, qn, mk, mv, b, t)
        x2 = matmul(o, wo, tm=1024, tn=512, res=x2, name="xattn_out_proj")
        h = rmsnorm(x2, fn)
        u = matmul(h, [w1, w3], swiglu=True, tm=1024, tn=256, out_dtype=BF16, name="ffn_up")
        return matmul(u, bf(w2), tm=512, tn=512, res=x2, name="ffn_down")

    nsa_cols = NSA_HEADS * HEAD_DIM + 6 * NSA_KV_HEADS * HEAD_DIM
    gate_cols = 3 * NSA_HEADS
    rw0 = nsa_cols + gate_cols
    rw1 = rw0 + 3 * RWKV_WIDTH
    pad_cols = lambda wm: jnp.pad(wm, ((0, 0), (0, LANES - wm.shape[1])))
    w_c = jnp.concatenate([
        pad_cols(l0_w_in[:, nsa_cols:rw0]),
        pad_cols(l0_w_in[:, rw1:rw1 + LORA_DECAY]),
        pad_cols(l0_w_in[:, rw1 + LORA_DECAY:rw1 + LORA_DECAY + LORA_AAA]),
        l0_w_in[:, rw1 + LORA_DECAY + LORA_AAA:]], axis=1)
    h = rmsnorm(x2, l0_mix_norm)
    pa = matmul(h, l0_w_in, ncols=nsa_cols, tm=1024, tn=512, name="l0_proj_nsa")
    w_rkv = l0_w_in[:, rw0:rw1].reshape(d, 3, RWKV_WIDTH)
    pb = matmul(h, bf(_to_jh(w_rkv).reshape(d, 3 * RWKV_WIDTH)), tm=1024, tn=512, name="l0_proj_rwkv")
    pc = matmul(h, bf(w_c), tm=1024, tn=w_c.shape[1], name="l0_proj_small")
    o_a = nsa_mixer(pa, pc, b, t, tabs, cmp_tabs, nsa_q_norm, nsa_kc_norm, nsa_ks_norm, nsa_kw_norm,
                    nsa_pe_k, nsa_pe_v, nsa_ck_w1, nsa_ck_w2, nsa_cv_w1, nsa_cv_w2)
    o_b = rwkv7(pb, pc, b, t, rwkv_mu, rwkv_w0, rwkv_w2, rwkv_a0, rwkv_a2, rwkv_g2, rwkv_kk, rwkv_ka,
                rwkv_rk, rwkv_gn_g, rwkv_gn_b)
    nsa_w = NSA_HEADS * HEAD_DIM
    w_out_rwkv = l0_w_out[nsa_w:].reshape(RWKV_HEADS, RWKV_HEAD, d).swapaxes(0, 1).reshape(RWKV_WIDTH, d)
    x2 = matmul([o_a, o_b], [l0_w_out[:nsa_w], w_out_rwkv], tm=1024, tn=512, res=x2,
                name="l0_out_proj")
    x2 = tail(x2, l0_xattn_norm, l0_mem_wq, l0_mem_q_norm, l0_mem_wo, l0_ffn_norm, l0_w1, l0_w3, l0_w2)

    main_cols = DSA_HEADS * HEAD_DIM + 2 * DSA_KV_HEADS * HEAD_DIM + IDX_HEADS * IDX_DIM
    w_s = jnp.concatenate([l1_w_in[:, main_cols:main_cols + IDX_DIM],
                           pad_cols(l1_w_in[:, main_cols + IDX_DIM:])], axis=1)
    h = rmsnorm(x2, l1_mix_norm)
    p1 = matmul(h, l1_w_in, ncols=main_cols, tm=1024, tn=512, name="l1_proj_main")
    p2 = matmul(h, bf(w_s), tm=1024, tn=w_s.shape[1], name="l1_proj_small")
    o = dsa_mixer_core(p1, p2, b, t, tabs, dsa_q_norm, dsa_k_norm, dsa_ki_norm)
    x2 = matmul(o, l1_w_out, tm=1024, tn=512, res=x2, name="l1_out_proj")
    x2 = tail(x2, l1_xattn_norm, l1_mem_wq, l1_mem_q_norm, l1_mem_wo, l1_ffn_norm, l1_w1, l1_w3, l1_w2)
    return x2.reshape(b, t, d)
```

```python
import functools

import numpy as np
import jax
import jax.numpy as jnp
from jax import lax
from jax.experimental import pallas as pl
from jax.experimental.pallas import tpu as pltpu

F32 = jnp.float32
BF16 = jnp.bfloat16

HEAD_DIM = 128
ROPE_DIM = HEAD_DIM // 4
ROPE_HALF = ROPE_DIM // 2
ROPE_THETA = 500000.0
NORM_EPS = 1e-6
NEG_INF = -1e30

NSA_HEADS = 16
NSA_KV_HEADS = 4
NSA_GROUP = NSA_HEADS // NSA_KV_HEADS
CMP_BLOCK = 32
CMP_STRIDE = 16
SEL_BLOCK = 64
SEL_TOPN = 16
WINDOW = 512
FORCED_SCORE = 1e4

RWKV_WIDTH = 2048
RWKV_HEAD = 64
RWKV_HEADS = RWKV_WIDTH // RWKV_HEAD
LORA_DECAY = 96
LORA_AAA = 96
LORA_GATE = 256
GN_EPS = 64e-5

DSA_HEADS = 32
DSA_KV_HEADS = 4
DSA_GROUP = DSA_HEADS // DSA_KV_HEADS
IDX_HEADS = 32
IDX_DIM = 128
DSA_TOPK_MAX = 256

MEM_HEADS = 4
MEM_WIDTH = MEM_HEADS * HEAD_DIM

LANES = 128
VMEM_BIG = 56 * 1024 * 1024
VMEM_MID = 40 * 1024 * 1024
INT_MIN = -2 ** 31
MASK_BIG = 1e30
FLASH_TK = 512
QK_SCALE = HEAD_DIM ** -0.5 * float(np.log2(np.e))


def _params(sem, vmem=VMEM_MID):
    return pltpu.CompilerParams(dimension_semantics=sem, vmem_limit_bytes=vmem)


def _sigmoid(x):
    return 1.0 / (1.0 + jnp.exp(-x))


def _dot(a, b, preferred_element_type=F32, precision=None):
    return lax.dot_general(a, b, (((1,), (0,)), ((), ())), precision=precision,
                           preferred_element_type=preferred_element_type)


def _nt_dot(a, b):
    return lax.dot_general(a, b, (((1,), (1,)), ((), ())), preferred_element_type=F32)


def _split_dot(x, w_bf16):
    hi = x.astype(BF16)
    lo = (x - hi.astype(F32)).astype(BF16)
    return (_dot(hi, w_bf16, preferred_element_type=F32)
            + _dot(lo, w_bf16, preferred_element_type=F32))


def _rmsnorm_kernel(x_ref, g_ref, o_ref):
    x = x_ref[...]
    ms = jnp.mean(x * x, axis=-1, keepdims=True)
    o_ref[...] = (x * lax.rsqrt(ms + NORM_EPS) * g_ref[...]).astype(o_ref.dtype)


def rmsnorm(x2d, g, tm=256):
    m, d = x2d.shape
    return pl.pallas_call(
        _rmsnorm_kernel,
        grid=(m // tm,),
        in_specs=[pl.BlockSpec((tm, d), lambda i: (i, 0)),
                  pl.BlockSpec((1, d), lambda i: (0, 0))],
        out_specs=pl.BlockSpec((tm, d), lambda i: (i, 0)),
        out_shape=jax.ShapeDtypeStruct((m, d), BF16),
        compiler_params=_params(("parallel",)),
        name="rmsnorm",
    )(x2d, g.reshape(1, d))


def _mm_kernel(*refs, n_a, swiglu, has_res):
    a_refs = refs[:n_a]
    n_b = 2 if swiglu else n_a
    b_refs = refs[n_a:n_a + n_b]
    res_ref = refs[n_a + n_b] if has_res else None
    o_ref = refs[-1]
    if swiglu:
        a = a_refs[0][...]
        gate = _dot(a, b_refs[0][...].astype(BF16))
        y = gate * _sigmoid(gate) * _dot(a, b_refs[1][...].astype(BF16))
    else:
        y = _dot(a_refs[0][...], b_refs[0][...].astype(BF16))
        for a_ref, b_ref in zip(a_refs[1:], b_refs[1:]):
            y = y + _dot(a_ref[...], b_ref[...].astype(BF16))
    if has_res:
        y = y + res_ref[...]
    o_ref[...] = y.astype(o_ref.dtype)


def matmul(a, b, *, tm, tn, swiglu=False, col0=0, ncols=None, res=None, out_dtype=F32, name="matmul"):
    a_list = list(a) if isinstance(a, (list, tuple)) else [a]
    b_list = list(b) if isinstance(b, (list, tuple)) else [b]
    m = a_list[0].shape[0]
    n = b_list[0].shape[1] - col0 if ncols is None else ncols
    tm = min(tm, m)
    tn = min(tn, n)
    assert m % tm == 0 and n % tn == 0 and col0 % tn == 0
    off = col0 // tn
    in_specs = [pl.BlockSpec((tm, ai.shape[1]), lambda i, j: (i, 0)) for ai in a_list]
    in_specs += [pl.BlockSpec((bi.shape[0], tn), lambda i, j: (0, off + j)) for bi in b_list]
    args = a_list + b_list
    if res is not None:
        in_specs.append(pl.BlockSpec((tm, tn), lambda i, j: (i, j)))
        args.append(res)
    return pl.pallas_call(
        functools.partial(_mm_kernel, n_a=len(a_list), swiglu=swiglu, has_res=res is not None),
        grid=(m // tm, n // tn),
        in_specs=in_specs,
        out_specs=pl.BlockSpec((tm, tn), lambda i, j: (i, j)),
        out_shape=jax.ShapeDtypeStruct((m, n), out_dtype),
        compiler_params=_params(("parallel", "parallel"), VMEM_BIG),
        name=name,
    )(*args)


def _rope_tab_kernel(pos_ref, inv_ref, c_ref, sa_ref, sb_ref):
    ang = pos_ref[...].astype(F32) * inv_ref[...]
    c = jnp.cos(ang)
    s = jnp.sin(ang)
    lane = lax.broadcasted_iota(jnp.int32, ang.shape, 1)
    c_ref[...] = jnp.where(lane < ROPE_DIM, c, 1.0)
    sa_ref[...] = jnp.where(lane < ROPE_HALF, -s, 0.0)
    sb_ref[...] = jnp.where((lane >= ROPE_HALF) & (lane < ROPE_DIM), s, 0.0)


def rope_tables(pos_flat, tm=256):
    n = pos_flat.shape[0]
    tm = min(tm, n)
    inv = ROPE_THETA ** (-jnp.arange(0, ROPE_DIM, 2, dtype=F32) / ROPE_DIM)
    inv_row = jnp.concatenate([inv, inv, jnp.zeros((LANES - ROPE_DIM,), F32)]).reshape(1, LANES)
    pos_b = jnp.broadcast_to(pos_flat[:, None], (n, LANES))
    spec = pl.BlockSpec((tm, LANES), lambda i: (i, 0))
    shp = jax.ShapeDtypeStruct((n, LANES), F32)
    return pl.pallas_call(
        _rope_tab_kernel,
        grid=(n // tm,),
        in_specs=[spec, pl.BlockSpec((1, LANES), lambda i: (0, 0))],
        out_specs=[spec, spec, spec],
        out_shape=[shp, shp, shp],
        compiler_params=_params(("parallel",)),
        name="rope_tables",
    )(pos_b, inv_row)


def _head_norm(xh, g):
    ms = jnp.mean(xh * xh, axis=-1, keepdims=True)
    return xh * lax.rsqrt(ms + NORM_EPS) * g


def _rope(xh, c, sa, sb):
    return (xh * c + pltpu.roll(xh, LANES - ROPE_HALF, 1) * sa
            + pltpu.roll(xh, ROPE_HALF, 1) * sb)


def _norm_rope_kernel(*refs, heads, do_norm, do_rope, scale):
    x_ref = refs[0]
    pos = 1
    g = None
    if do_norm:
        g = refs[pos][...]
        pos += 1
    if do_rope:
        c, sa, sb = refs[pos][...], refs[pos + 1][...], refs[pos + 2][...]
        pos += 3
    o_ref = refs[pos]
    for h in range(heads):
        xh = x_ref[:, h * HEAD_DIM:(h + 1) * HEAD_DIM]
        if do_norm:
            xh = _head_norm(xh, g)
        if do_rope:
            xh = _rope(xh, c, sa, sb)
        if scale != 1.0:
            xh = xh * scale
        o_ref[:, h * HEAD_DIM:(h + 1) * HEAD_DIM] = xh.astype(o_ref.dtype)


def norm_rope(x2d, col_start, heads, gain, tabs, scale=1.0, tm=256, name="norm_rope"):
    m = x2d.shape[0]
    hb = heads
    while col_start % (hb * HEAD_DIM):
        hb //= 2
    w = hb * HEAD_DIM
    off = col_start // w
    tm = min(tm, m)
    in_specs = [pl.BlockSpec((tm, w), lambda i, j: (i, off + j))]
    args = [x2d]
    if gain is not None:
        in_specs.append(pl.BlockSpec((1, HEAD_DIM), lambda i, j: (0, 0)))
        args.append(gain.reshape(1, HEAD_DIM))
    if tabs is not None:
        in_specs += [pl.BlockSpec((tm, LANES), lambda i, j: (i, 0))] * 3
        args += list(tabs)
    return pl.pallas_call(
        functools.partial(_norm_rope_kernel, heads=hb, do_norm=gain is not None,
                          do_rope=tabs is not None, scale=scale),
        grid=(m // tm, heads // hb),
        in_specs=in_specs,
        out_specs=pl.BlockSpec((tm, w), lambda i, j: (i, j)),
        out_shape=jax.ShapeDtypeStruct((m, heads * HEAD_DIM), BF16),
        compiler_params=_params(("parallel", "parallel")),
        name=name,
    )(*args)


def _compress_kernel(*refs, is_key):
    if is_key:
        x_ref, w1_ref, w2_ref, pe_ref, g_ref, c_ref, sa_ref, sb_ref, o_ref = refs
    else:
        x_ref, w1_ref, w2_ref, pe_ref, o_ref = refs
    half = CMP_STRIDE * HEAD_DIM
    x = x_ref[0, 0]
    nc = x.shape[0]
    top = _dot(x, w1_ref[:half, :], preferred_element_type=F32)
    bot = _dot(x, w1_ref[half:, :], preferred_element_type=F32)
    pe_term = _dot(pe_ref[...], w1_ref[...], preferred_element_type=F32)[0:1, :]
    pre = top + pltpu.roll(bot, nc - 1, 0) + pe_term
    hid = jax.nn.gelu(pre)
    out = _dot(hid.astype(BF16), w2_ref[...], preferred_element_type=F32)
    if is_key:
        out = _rope(_head_norm(out, g_ref[...]), c_ref[0], sa_ref[0], sb_ref[0])
    o_ref[0, 0] = out.astype(o_ref.dtype)


def compress(xblk, w1, w2, pe, key_extras=None, name="compress"):
    b, h, nc, wdt = xblk.shape
    pe8 = jnp.broadcast_to(pe.reshape(1, CMP_BLOCK * HEAD_DIM), (8, CMP_BLOCK * HEAD_DIM)).astype(BF16)
    in_specs = [pl.BlockSpec((1, 1, nc, wdt), lambda i, j: (i, j, 0, 0)),
                pl.BlockSpec((CMP_BLOCK * HEAD_DIM, HEAD_DIM), lambda i, j: (0, 0)),
                pl.BlockSpec((HEAD_DIM, HEAD_DIM), lambda i, j: (0, 0)),
                pl.BlockSpec((8, CMP_BLOCK * HEAD_DIM), lambda i, j: (0, 0))]
    args = [xblk, w1.astype(BF16), w2.astype(BF16), pe8]
    if key_extras is not None:
        gain, tabs = key_extras
        in_specs.append(pl.BlockSpec((1, HEAD_DIM), lambda i, j: (0, 0)))
        in_specs += [pl.BlockSpec((1, nc, LANES), lambda i, j: (i, 0, 0))] * 3
        args += [gain.reshape(1, HEAD_DIM)] + [t.reshape(b, nc, LANES) for t in tabs]
    out_dtype = BF16
    return pl.pallas_call(
        functools.partial(_compress_kernel, is_key=key_extras is not None),
        grid=(b, h),
        in_specs=in_specs,
        out_specs=pl.BlockSpec((1, 1, nc, HEAD_DIM), lambda i, j: (i, j, 0, 0)),
        out_shape=jax.ShapeDtypeStruct((b, h, nc, HEAD_DIM), out_dtype),
        compiler_params=_params(("parallel", "parallel")),
        name=name,
    )(*args)


def _stack_heads(q_ref, group, tq):
    return jnp.concatenate([q_ref[0, :, g * HEAD_DIM:(g + 1) * HEAD_DIM] for g in range(group)], axis=0)


def _unstack_heads(o_ref, o_t, group, tq):
    o = o_t.T
    for g in range(group):
        o_ref[0, :, g * HEAD_DIM:(g + 1) * HEAD_DIM] = o[g * tq:(g + 1) * tq, :].astype(o_ref.dtype)


def _cmp_kernel(q_ref, kc_ref, vct_ref, mt_ref, o_ref, sel_ref, *, tq, n_sel, topn):
    group = NSA_GROUP
    t0 = pl.program_id(2) * tq
    q = _stack_heads(q_ref, group, tq)
    kc = kc_ref[0, 0]
    nc = kc.shape[0]
    s = _nt_dot(kc, q)
    c_idx = lax.broadcasted_iota(jnp.int32, (nc, tq), 0)
    t_idx = t0 + lax.broadcasted_iota(jnp.int32, (nc, tq), 1)
    ok1 = (CMP_STRIDE * c_idx + CMP_BLOCK - 1) <= t_idx
    bias1 = jnp.where(ok1, 0.0, NEG_INF)
    okf1 = jnp.where(ok1, 1.0, 0.0)
    bias = jnp.concatenate([bias1] * group, axis=1)
    okf = jnp.concatenate([okf1] * group, axis=1)
    s = s + bias
    m = jnp.max(s, axis=0, keepdims=True)
    e = jnp.exp2(s - m) * okf
    l = jnp.sum(e, axis=0, keepdims=True)
    p = e * jnp.where(l > 0.0, 1.0 / l, 0.0)
    o_t = _dot(vct_ref[0, 0], p.astype(BF16), preferred_element_type=F32)
    _unstack_heads(o_ref, o_t, group, tq)

    psum = p[:, 0:tq]
    for g in range(1, group):
        psum = psum + p[:, g * tq:(g + 1) * tq]
    imp = _dot(mt_ref[...], psum, preferred_element_type=F32,
                  precision=lax.Precision.HIGHEST)
    j_idx = lax.broadcasted_iota(jnp.int32, (n_sel, tq), 0)
    jt = (t0 + lax.broadcasted_iota(jnp.int32, (n_sel, tq), 1)) // SEL_BLOCK
    forced = (j_idx == 0) | (j_idx == jt) | (j_idx == jt - 1)
    imp = jnp.where(forced, FORCED_SCORE, imp)
    imp = jnp.where(j_idx <= jt, imp, -jnp.inf)
    rank = jnp.zeros((n_sel, tq), F32)
    for jp in range(n_sel):
        row = imp[jp:jp + 1, :]
        before = (row > imp) | ((row == imp) & (j_idx > jp))
        rank = rank + jnp.where(before, 1.0, 0.0)
    bias_t = jnp.where(rank < topn, 0.0, -MASK_BIG)
    bias_t = jnp.concatenate([bias_t, jnp.zeros((LANES - n_sel, tq), F32)], axis=0)
    sel_ref[0, 0] = bias_t.T.astype(sel_ref.dtype)


def nsa_compressed(qn, kc, vct, b, t, tq=256):
    nc = kc.shape[2]
    n_sel = t // SEL_BLOCK
    assert n_sel <= LANES
    topn = min(SEL_TOPN, n_sel)
    tq = min(tq, t)
    cs = CMP_STRIDE * np.arange(nc)[:, None]
    ss = SEL_BLOCK * np.arange(n_sel)[None, :]
    ov = np.clip(np.minimum(cs + CMP_BLOCK, ss + SEL_BLOCK) - np.maximum(cs, ss), 0, None) / CMP_BLOCK
    ov[nc - 1, :] = 0.0
    mt = jnp.asarray(ov.T, dtype=F32)
    gw = NSA_GROUP * HEAD_DIM
    return pl.pallas_call(
        functools.partial(_cmp_kernel, tq=tq, n_sel=n_sel, topn=topn),
        grid=(b, NSA_KV_HEADS, t // tq),
        in_specs=[pl.BlockSpec((1, tq, gw), lambda i, h, j: (i, j, h)),
                  pl.BlockSpec((1, 1, nc, HEAD_DIM), lambda i, h, j: (i, h, 0, 0)),
                  pl.BlockSpec((1, 1, HEAD_DIM, nc), lambda i, h, j: (i, h, 0, 0)),
                  pl.BlockSpec((n_sel, nc), lambda i, h, j: (0, 0))],
        out_specs=[pl.BlockSpec((1, tq, gw), lambda i, h, j: (i, j, h)),
                   pl.BlockSpec((1, 1, tq, LANES), lambda i, h, j: (i, h, j, 0))],
        out_shape=[jax.ShapeDtypeStruct((b, t, NSA_HEADS * HEAD_DIM), F32),
                   jax.ShapeDtypeStruct((b, NSA_KV_HEADS, t, LANES), BF16)],
        compiler_params=_params(("parallel", "parallel", "parallel")),
        name="nsa_compressed",
    )(qn, kc, vct, mt)


def _flash_kernel(*refs, mode, group, slab, tq, tk):
    if mode == "window":
        q_ref, k_ref, vt_ref, o_ref, q_s, m_s, l_s, acc_s, sa_s, sb_s, p_s = refs
        xq = None
    else:
        q_ref, k_ref, vt_ref, xq_ref, xk_ref, o_ref, q_s, m_s, l_s, acc_s, sa_s, sb_s, p_s = refs
        xq = xq_ref[0, 0] if mode == "sel" else xq_ref[...]
    qi = pl.program_id(2)
    t0 = qi * tq
    q = _stack_heads(q_ref, group, tq)
    if xq is not None:
        q = jnp.concatenate([q, jnp.concatenate([xq] * group, axis=0)], axis=1)
    q_s[...] = q
    m_s[...] = jnp.full(m_s.shape, NEG_INF, F32)
    l_s[...] = jnp.zeros(l_s.shape, F32)
    acc_s[...] = jnp.zeros(acc_s.shape, F32)

    def scores(j):
        k0 = pl.multiple_of(j * tk, tk)
        kt = k_ref[0, pl.ds(k0, tk), :]
        if mode == "sel":
            kt = jnp.concatenate([kt, xk_ref[pl.ds(k0, tk), :]], axis=1)
        elif mode == "dsa":
            kt = jnp.concatenate([kt, xk_ref[0, pl.ds(k0, tk), :]], axis=1)
        return _nt_dot(kt, q_s[...])

    def absorb(s_ref, j, keep):
        if keep is not None:
            kpos = j * tk + lax.broadcasted_iota(jnp.int32, (tk, tq), 0)
            tpos = t0 + lax.broadcasted_iota(jnp.int32, (tk, tq), 1)
            ok = keep(kpos, tpos)
            for g in range(group):
                cols = slice(g * tq, (g + 1) * tq)
                s_ref[:, cols] = jnp.where(ok, s_ref[:, cols], NEG_INF)
        m_old = m_s[...]
        m_new = jnp.maximum(m_old, jnp.max(s_ref[...], axis=0, keepdims=True))
        alpha = jnp.exp2(m_old - m_new)
        p = jnp.exp2(s_ref[...] - m_new)
        l_s[...] = alpha * l_s[...] + jnp.sum(p, axis=0, keepdims=True)
        p_s[...] = p.astype(BF16)
        acc_s[...] = acc_s[...] * alpha + _dot(vt_ref[0, 0, j], p_s[...])
        m_s[...] = m_new

    causal = lambda kpos, tpos: kpos <= tpos

    if mode == "window":
        nw = WINDOW // tk
        edge = lambda kpos, tpos: tpos - kpos < WINDOW
        keeps = [edge] + [None] * (nw - 1) + [causal]

        @pl.when(qi >= nw)
        def _():
            bufs = (sa_s, sb_s)
            sa_s[...] = scores(qi - nw)
            for n, keep in enumerate(keeps):
                if n + 1 < len(keeps):
                    bufs[(n + 1) % 2][...] = scores(qi - nw + n + 1)
                absorb(bufs[n % 2], qi - nw + n, keep)

        @pl.when(qi < nw)
        def _():
            def early(j, carry):
                sa_s[...] = scores(j)
                absorb(sa_s, j, None)
                return carry
            lax.fori_loop(0, qi, early, 0)
            sa_s[...] = scores(qi)
            absorb(sa_s, qi, causal)
    else:
        last = (t0 + tq - 1) // tk
        last_keep = causal if mode == "sel" else None
        sa_s[...] = scores(0)

        def pair(i, carry):
            j = 2 * i
            sb_s[...] = scores(j + 1)
            absorb(sa_s, j, None)
            sa_s[...] = scores(j + 2)
            absorb(sb_s, j + 1, None)
            return carry

        lax.fori_loop(0, last // 2, pair, 0)

        @pl.when(last % 2 == 1)
        def _():
            sb_s[...] = scores(last)
            absorb(sa_s, last - 1, None)
            absorb(sb_s, last, last_keep)

        @pl.when(last % 2 == 0)
        def _():
            absorb(sa_s, last, last_keep)

    _unstack_heads(o_ref, acc_s[...] * (1.0 / l_s[...]), group, tq)


def flash_masked(qn, kn, vt5, *, mode, group, b, t, tq, slab=None, extra=None, out_dtype=F32, name="flash"):
    kvh = kn.shape[2] // HEAD_DIM
    gw = group * HEAD_DIM
    tq = min(tq, t)
    ntk, tk = vt5.shape[2], vt5.shape[4]
    slab = group if slab is None else slab
    in_specs = [pl.BlockSpec((1, tq, gw), lambda i, h, qi: (i, qi, h)),
                pl.BlockSpec((1, t, HEAD_DIM), lambda i, h, qi: (i, 0, h)),
                pl.BlockSpec((1, 1, ntk, HEAD_DIM, tk), lambda i, h, qi: (i, h, 0, 0, 0))]
    args = [qn, kn, vt5]
    kdim = HEAD_DIM
    if mode == "sel":
        assert tk % tq == 0
        et = np.zeros((t, LANES), np.float32)
        et[np.arange(t), np.arange(t) // SEL_BLOCK] = 1.0
        in_specs += [pl.BlockSpec((1, 1, tq, LANES), lambda i, h, qi: (i, h, qi, 0)),
                     pl.BlockSpec((t, LANES), lambda i, h, qi: (0, 0))]
        args += [extra, jnp.asarray(et, dtype=BF16)]
        kdim += LANES
    elif mode == "dsa":
        assert tq == LANES
        in_specs += [pl.BlockSpec((tq, LANES), lambda i, h, qi: (0, 0)),
                     pl.BlockSpec((1, t, tq), lambda i, h, qi: (i, 0, qi))]
        args += [jnp.eye(tq, dtype=BF16), extra]
        kdim += LANES
    else:
        assert tk == tq and WINDOW % tk == 0
    return pl.pallas_call(
        functools.partial(_flash_kernel, mode=mode, group=group, slab=slab, tq=tq, tk=tk),
        grid=(b, kvh, t // tq),
        in_specs=in_specs,
        out_specs=pl.BlockSpec((1, tq, gw), lambda i, h, qi: (i, qi, h)),
        out_shape=jax.ShapeDtypeStruct((b, t, kvh * gw), out_dtype),
        scratch_shapes=[pltpu.VMEM((group * tq, kdim), BF16),
                        pltpu.VMEM((1, group * tq), F32),
                        pltpu.VMEM((1, group * tq), F32),
                        pltpu.VMEM((HEAD_DIM, group * tq), F32),
                        pltpu.VMEM((tk, group * tq), F32),
                        pltpu.VMEM((tk, group * tq), F32),
                        pltpu.VMEM((tk, group * tq), BF16)],
        compiler_params=_params(("parallel", "parallel", "parallel"), VMEM_BIG),
        name=name,
    )(*args)


def _nsa_merge_kernel(oc_ref, os_ref, ow_ref, g_ref, e_ref, o_ref):
    gate = _sigmoid(g_ref[...])
    out = None
    for j, src in enumerate((oc_ref, os_ref, ow_ref)):
        gj = _split_dot(gate, e_ref[j])
        term = gj * src[...]
        out = term if out is None else out + term
    o_ref[...] = out.astype(o_ref.dtype)


def nsa_merge(oc, os_, ow, pc, tm=256):
    m, w = oc.shape
    tm = min(tm, m)
    e = np.zeros((3, LANES, w), np.float32)
    for h in range(NSA_HEADS):
        for j in range(3):
            e[j, h * 3 + j, h * HEAD_DIM:(h + 1) * HEAD_DIM] = 1.0
    spec = pl.BlockSpec((tm, w), lambda i: (i, 0))
    return pl.pallas_call(
        _nsa_merge_kernel,
        grid=(m // tm,),
        in_specs=[spec, spec, spec,
                  pl.BlockSpec((tm, LANES), lambda i: (i, 0)),
                  pl.BlockSpec((3, LANES, w), lambda i: (0, 0, 0))],
        out_specs=spec,
        out_shape=jax.ShapeDtypeStruct((m, w), BF16),
        compiler_params=_params(("parallel",)),
        name="nsa_merge",
    )(oc, os_, ow, pc, jnp.asarray(e, dtype=BF16))


SUB = 8
HEADS_PER_SLAB = LANES // RWKV_HEADS
N_SLABS = RWKV_WIDTH // LANES


def _to_jh(p):
    lead = p.shape[:-1]
    return p.reshape(*lead, RWKV_HEADS, RWKV_HEAD).swapaxes(-1, -2).reshape(*lead, RWKV_WIDTH)


def _lane_table(p, nb):
    tab = p.reshape(RWKV_HEADS, RWKV_HEAD).T
    return jnp.tile(tab, (1, nb)).reshape(RWKV_HEAD, 1, LANES)


def _head_sum_jh(x, bd):
    acc = x[:, 0:LANES]
    for s in range(1, N_SLABS):
        acc = acc + x[:, s * LANES:(s + 1) * LANES]
    tot = _split_dot(acc, bd)
    return jnp.concatenate([tot] * N_SLABS, axis=1)


def _interleave(xs, out_ref):
    nb = len(xs)
    tt = xs[0].shape[0]
    group = (lax.broadcasted_iota(jnp.int32, xs[0].shape, 1) % LANES) // RWKV_HEADS
    moved = {}
    for delta in range(1 - nb, nb):
        if delta == 0:
            continue
        src = None
        for g in range(max(0, -delta), min(nb, nb - delta)):
            src = xs[g + delta] if src is None else jnp.where(group == g, xs[g + delta], src)
        moved[delta] = pltpu.roll(src, (delta * RWKV_HEADS) % RWKV_WIDTH, 1)
    for jj in range(HEADS_PER_SLAB):
        z = xs[jj]
        for bi in range(nb):
            if bi != jj:
                z = jnp.where(group == bi, moved[bi - jj], z)
        for tb in range(tt // SUB):
            for s in range(N_SLABS):
                out_ref[tb, HEADS_PER_SLAB * s + jj] = z[tb * SUB:(tb + 1) * SUB, s * LANES:(s + 1) * LANES]


def _deinterleave(val, o_ref):
    nb = o_ref.shape[0]
    tt = o_ref.shape[1]
    zs = []
    for jj in range(HEADS_PER_SLAB):
        z = jnp.concatenate([val[:, HEADS_PER_SLAB * s + jj] for s in range(N_SLABS)], axis=-1)
        zs.append(z.reshape(tt, RWKV_WIDTH))
    lane = lax.broadcasted_iota(jnp.int32, (tt, RWKV_WIDTH), 1) % LANES
    for bi in range(nb):
        out = None
        for jj in range(HEADS_PER_SLAB):
            shift = ((jj - bi) * RWKV_HEADS) % RWKV_WIDTH
            zj = zs[jj] if shift == 0 else pltpu.roll(zs[jj], shift, 1)
            out = zj if out is None else jnp.where(lane // RWKV_HEADS == jj, zj, out)
        o_ref[bi] = out.astype(o_ref.dtype)


def _shifted(x, prev_row, first):
    rolled = pltpu.roll(x, 1, 0)
    row0 = jnp.where(first, 0.0, prev_row)
    ridx = lax.broadcasted_iota(jnp.int32, x.shape, 0)
    return jnp.where(ridx == 0, row0, rolled)


def _rwkv_prep_kernel(pb_ref, pbp_ref, pc_ref, pcp_ref, mub_ref, muc_ref, w0_ref, w2_ref, a0_ref, a2_ref,
                      g2_ref, kkg_ref, ka_ref, bd_ref,
                      r_ref, w_ref, k_ref, v_ref, kk_ref, b_ref, g_ref):
    first = pl.program_id(0) == 0
    w = RWKV_WIDTH
    nb = pb_ref.shape[0]
    outs = [[] for _ in range(7)]
    for bi in range(nb):
        xb = pb_ref[bi]
        xb = xb + (_shifted(xb, pbp_ref[bi, SUB - 1:SUB, :], first) - xb) * mub_ref[...]
        xc = pc_ref[bi]
        xc = xc + (_shifted(xc, pcp_ref[bi, SUB - 1:SUB, :], first) - xc) * muc_ref[...]
        r, k, v = xb[:, :w], xb[:, w:2 * w], xb[:, 2 * w:]
        wd, ad, gd = xc[:, LANES:2 * LANES], xc[:, 2 * LANES:3 * LANES], xc[:, 3 * LANES:]
        z = w0_ref[...] + _dot(jnp.tanh(wd).astype(BF16), w2_ref[...])
        nz = -z
        softplus = jnp.maximum(nz, 0.0) + jnp.log(1.0 + jnp.exp(-jnp.abs(nz)))
        w_log = -softplus - 0.5
        a = _sigmoid(a0_ref[...] + _dot(ad.astype(BF16), a2_ref[...]))
        g = _dot(_sigmoid(gd).astype(BF16), g2_ref[...])
        kkv = k * kkg_ref[...]
        norm = jnp.sqrt(_head_sum_jh(kkv * kkv, bd_ref[...]))
        kkv = kkv / jnp.maximum(norm, 1e-12)
        vals = (r, jnp.exp(-jnp.exp(w_log)), k * (1.0 + (a - 1.0) * ka_ref[...]), v, kkv, kkv * a, g)
        for lst, val in zip(outs, vals):
            lst.append(val)
    for lst, ref in zip(outs, (r_ref, w_ref, k_ref, v_ref, kk_ref, b_ref, g_ref)):
        _interleave(lst, ref)


def _pad_rows(wm, rows):
    return jnp.pad(wm, ((0, rows - wm.shape[0]), (0, 0)))


def _same_head_ones():
    idx = np.arange(LANES) % RWKV_HEADS
    return jnp.asarray(idx[:, None] == idx[None, :], dtype=BF16)


def rwkv_prep(pb, pc, b, t, mu, w0, w2, a0, a2, g2, kk_gain, k_a, tt=32):
    w = RWKV_WIDTH
    assert b * RWKV_HEADS == LANES and t % tt == 0 and tt % SUB == 0
    mu_r, mu_k, mu_v, mu_wd, mu_ad, mu_gd = jnp.split(
        mu, [int(x) for x in np.cumsum([w, w, w, LORA_DECAY, LORA_AAA])])
    mub = jnp.concatenate([_to_jh(mu_r), _to_jh(mu_k), _to_jh(mu_v)]).reshape(1, 3 * w)
    pad = lambda z: jnp.pad(z, (0, LANES - z.shape[0]))
    muc = jnp.concatenate([jnp.zeros((LANES,), F32), pad(mu_wd), pad(mu_ad), mu_gd]).reshape(1, -1)
    cw = pc.shape[1]
    row = lambda z: _to_jh(z).reshape(1, w)
    lora = lambda wm, rows: _to_jh(_pad_rows(wm, rows)).astype(BF16)
    full = lambda shape: pl.BlockSpec(shape, lambda i: (0,) * len(shape))
    tile = lambda width: pl.BlockSpec((b, tt, width), lambda i: (0, i, 0))
    prev = lambda width: pl.BlockSpec((b, SUB, width), lambda i: (0, jnp.maximum(i * (tt // SUB) - 1, 0), 0))
    out_spec = pl.BlockSpec((tt // SUB, RWKV_HEAD, SUB, LANES), lambda i: (i, 0, 0, 0))
    shp = jax.ShapeDtypeStruct((t // SUB, RWKV_HEAD, SUB, LANES), F32)
    pb3 = pb.reshape(b, t, 3 * w)
    pc3 = pc.reshape(b, t, cw)
    return pl.pallas_call(
        _rwkv_prep_kernel,
        grid=(t // tt,),
        in_specs=[tile(3 * w), prev(3 * w), tile(cw), prev(cw), full((1, 3 * w)), full((1, cw)),
                  full((1, w)), full((LANES, w)), full((1, w)), full((LANES, w)), full((LORA_GATE, w)),
                  full((1, w)), full((1, w)), full((LANES, LANES))],
        out_specs=[out_spec] * 7,
        out_shape=[shp] * 7,
        compiler_params=_params(("parallel",), VMEM_BIG),
        name="rwkv_prep",
    )(pb3, pb3, pc3, pc3, mub, muc, row(w0), lora(w2, LANES), row(a0), lora(a2, LANES), lora(g2, LORA_GATE),
      row(kk_gain), row(k_a), _same_head_ones())


def _rwkv_scan_kernel(w_ref, kk_ref, b_ref, k_ref, r_ref, v_ref, y_ref, s_ref, op_s, p_s, *, tb, rows):
    ig = pl.program_id(1)
    blk = SUB * RWKV_HEAD

    @pl.when(pl.program_id(0) == 0)
    def _():
        s_ref[ig] = jnp.zeros(s_ref.shape[1:], F32)

    @pl.when(ig == 0)
    def _():
        def gather(u, p):
            base = pl.multiple_of(u * blk, blk)
            for q in range(SUB):
                key = pl.ds(base + q, RWKV_HEAD, stride=SUB)
                t = u * SUB + q
                op_s[0, t] = -kk_ref[key, :] * p
                p = p * w_ref[key, :]
                inv = 1.0 / p
                op_s[1, t] = b_ref[key, :] * inv
                op_s[2, t] = k_ref[key, :] * inv
                op_s[3, t] = r_ref[key, :] * p
            return p
        p_s[...] = lax.fori_loop(0, tb // SUB, gather, jnp.ones(p_s.shape, F32))

    def steps(u, z):
        base = pl.multiple_of(u * blk, blk) + ig * (rows * SUB)
        for q in range(SUB):
            t = u * SUB + q
            val = pl.ds(base + q, rows, stride=SUB)
            sa = jnp.sum(z * op_s[0, t][None], axis=1)
            z = z + sa[:, None, :] * op_s[1, t][None] + v_ref[val, :][:, None, :] * op_s[2, t][None]
            y_ref[val, :] = jnp.sum(z * op_s[3, t][None], axis=1)
        return z

    s_ref[ig] = lax.fori_loop(0, tb // SUB, steps, s_ref[ig]) * p_s[...][None]


def rwkv_scan(w, kk, bb, k, r, v, tb=64, rows=4):
    n = RWKV_HEAD
    t = w.shape[0] // n
    groups = n // rows
    tb = min(tb, t)
    op = pl.BlockSpec((tb * n, LANES), lambda i, j: (i, 0))
    return pl.pallas_call(
        functools.partial(_rwkv_scan_kernel, tb=tb, rows=rows),
        grid=(t // tb, groups),
        in_specs=[op] * 6,
        out_specs=op,
        out_shape=jax.ShapeDtypeStruct((t * n, LANES), F32),
        scratch_shapes=[pltpu.VMEM((groups, rows, n, LANES), F32),
                        pltpu.VMEM((4, tb, n, LANES), F32),
                        pltpu.VMEM((n, LANES), F32)],
        compiler_params=_params(("arbitrary", "arbitrary"), VMEM_BIG),
        name="rwkv_scan",
    )(w, kk, bb, k, r, v)


def _rwkv_post_kernel(y_ref, r_ref, k_ref, v_ref, g_ref, rk_ref, gg_ref, gb_ref, o_ref):
    y = y_ref[...]
    inv_n = 1.0 / RWKV_HEAD
    mean = jnp.sum(y, axis=1, keepdims=True) * inv_n
    d = y - mean
    var = jnp.sum(d * d, axis=1, keepdims=True) * inv_n
    yn = d * lax.rsqrt(var + GN_EPS) * gg_ref[...][None] + gb_ref[...][None]
    bonus = jnp.sum(r_ref[...] * k_ref[...] * rk_ref[...][None], axis=1, keepdims=True) * v_ref[...]
    _deinterleave((yn + bonus) * g_ref[...], o_ref)


def rwkv_post(y, r, k, v, g, b, t, r_k, gn_g, gn_b, tt=32):
    tile = pl.BlockSpec((tt // SUB, RWKV_HEAD, SUB, LANES), lambda i: (i, 0, 0, 0))
    tab = pl.BlockSpec((RWKV_HEAD, 1, LANES), lambda i: (0, 0, 0))
    return pl.pallas_call(
        _rwkv_post_kernel,
        grid=(t // tt,),
        in_specs=[tile] * 5 + [tab] * 3,
        out_specs=pl.BlockSpec((b, tt, RWKV_WIDTH), lambda i: (0, i, 0)),
        out_shape=jax.ShapeDtypeStruct((b, t, RWKV_WIDTH), BF16),
        compiler_params=_params(("parallel",)),
        name="rwkv_post",
    )(y, r, k, v, g, _lane_table(r_k.reshape(-1), b), _lane_table(gn_g, b), _lane_table(gn_b, b))


def rwkv7(pb, pc, b, t, mu, w0, w2, a0, a2, g2, kk_gain, k_a, r_k, gn_g, gn_b):
    r, w, k, v, kk, bb, g = rwkv_prep(pb, pc, b, t, mu, w0, w2, a0, a2, g2, kk_gain, k_a)
    rows2d = lambda z: z.reshape(t * RWKV_HEAD, LANES)
    y = rwkv_scan(rows2d(w), rows2d(kk), rows2d(bb), rows2d(k), rows2d(r), rows2d(v))
    y = y.reshape(t // SUB, RWKV_HEAD, SUB, LANES)
    return rwkv_post(y, r, k, v, g, b, t, r_k, gn_g, gn_b).reshape(b * t, RWKV_WIDTH)


def _dsa_select_kernel(qi_ref, ki_ref, wt_ref, bias_ref, q_s, key_s, *, t, tq, kc, topk):
    t0 = pl.program_id(1) * tq
    nchunk = (t0 + tq - 1) // kc + 1
    for h in range(IDX_HEADS):
        q_s[h * tq:(h + 1) * tq, :] = qi_ref[0, :, h * IDX_DIM:(h + 1) * IDX_DIM]
    wt = wt_ref[0]

    def chunk(c, carry):
        k0 = pl.multiple_of(c * kc, kc)
        lg = _nt_dot(ki_ref[0, pl.ds(k0, kc), :], q_s[...])
        acc = jnp.zeros((kc, tq), F32)
        for h in range(IDX_HEADS):
            acc = acc + jnp.maximum(lg[:, h * tq:(h + 1) * tq], 0.0) * wt[h:h + 1, :]
        spos = k0 + lax.broadcasted_iota(jnp.int32, (kc, tq), 0)
        tpos = t0 + lax.broadcasted_iota(jnp.int32, (kc, tq), 1)
        score = jnp.where(spos <= tpos, acc + 0.0, -jnp.inf)
        bits = pltpu.bitcast(score, jnp.int32)
        key_s[pl.ds(k0, kc), :] = jnp.where(bits < 0, bits ^ jnp.int32(0x7FFFFFFF), bits)
        return carry

    lax.fori_loop(0, nchunk, chunk, 0)
    kf = jnp.float32(topk)
    part = 64

    def count(pred):
        def body(c, acc):
            k0 = pl.multiple_of(c * kc, kc)
            ind = jnp.where(pred(key_s[pl.ds(k0, kc), :], k0), 1.0, 0.0)
            return acc + jnp.sum(ind.reshape(kc // part, part, tq), axis=0)
        acc = lax.fori_loop(0, nchunk, body, jnp.zeros((part, tq), F32))
        return jnp.sum(acc, axis=0, keepdims=True)

    def value_bit(it, tau_u):
        cand_u = tau_u | lax.shift_left(jnp.int32(1), 31 - it)
        cand_s = cand_u ^ jnp.int32(INT_MIN)
        return jnp.where(count(lambda blk, k0: blk >= cand_s) >= kf, cand_u, tau_u)

    tau_u = lax.fori_loop(0, 32, value_bit, jnp.zeros((1, tq), jnp.int32))
    tau = tau_u ^ jnp.int32(INT_MIN)
    need = kf - count(lambda blk, k0: blk > tau)
    n_eq = count(lambda blk, k0: blk == tau)
    nbits = int(np.log2(t))

    def tie_search():
        def index_bit(it, jj):
            cand = jj | lax.shift_left(jnp.int32(1), nbits - 1 - it)

            def pred(blk, k0):
                spos = k0 + lax.broadcasted_iota(jnp.int32, (kc, tq), 0)
                return (blk == tau) & (spos < cand)
            return jnp.where(count(pred) < need, cand, jj)
        return lax.fori_loop(0, nbits, index_bit, jnp.zeros((1, tq), jnp.int32))

    jj = lax.cond(jnp.max(n_eq - need) > 0.0, tie_search, lambda: jnp.full((1, tq), t, jnp.int32))

    def emit(c, carry):
        k0 = pl.multiple_of(c * kc, kc)
        blk = key_s[pl.ds(k0, kc), :]
        spos = k0 + lax.broadcasted_iota(jnp.int32, (kc, tq), 0)
        tpos = t0 + lax.broadcasted_iota(jnp.int32, (kc, tq), 1)
        keep = ((blk > tau) | ((blk == tau) & (spos <= jj))) & (spos <= tpos)
        bias_ref[0, pl.ds(k0, kc), :] = jnp.where(keep, 0.0, -MASK_BIG).astype(bias_ref.dtype)
        return carry

    lax.fori_loop(0, nchunk, emit, 0)

    def fill(c, carry):
        k0 = pl.multiple_of(c * kc, kc)
        bias_ref[0, pl.ds(k0, kc), :] = jnp.full((kc, tq), -MASK_BIG, bias_ref.dtype)
        return carry

    lax.fori_loop(nchunk, t // kc, fill, 0)


def dsa_select(qi, ki, wt, b, t, tq=128, kc=512):
    topk = min(DSA_TOPK_MAX, t // 4)
    kc = min(kc, t)
    assert topk <= kc and t % kc == 0
    return pl.pallas_call(
        functools.partial(_dsa_select_kernel, t=t, tq=tq, kc=kc, topk=topk),
        grid=(b, t // tq),
        in_specs=[pl.BlockSpec((1, tq, IDX_HEADS * IDX_DIM), lambda i, j: (i, j, 0)),
                  pl.BlockSpec((1, t, IDX_DIM), lambda i, j: (i, 0, 0)),
                  pl.BlockSpec((1, IDX_HEADS, tq), lambda i, j: (i, 0, j))],
        out_specs=pl.BlockSpec((1, t, tq), lambda i, j: (i, 0, j)),
        out_shape=jax.ShapeDtypeStruct((b, t, t), BF16),
        scratch_shapes=[pltpu.VMEM((IDX_HEADS * tq, IDX_DIM), BF16),
                        pltpu.VMEM((t, tq), jnp.int32)],
        compiler_params=_params(("parallel", "parallel"), VMEM_BIG),
        name="dsa_select",
    )(qi, ki, wt)


def _xattn_kernel(q_ref, g_ref, mk_ref, mv_ref, o_ref):
    scale = HEAD_DIM ** -0.5
    for h in range(MEM_HEADS):
        sl = slice(h * HEAD_DIM, (h + 1) * HEAD_DIM)
        qh = (_head_norm(q_ref[:, sl], g_ref[...]) * scale).astype(BF16)
        s = _nt_dot(qh, mk_ref[0, :, sl])
        m = jnp.max(s, axis=-1, keepdims=True)
        e = jnp.exp(s - m)
        p = e / jnp.sum(e, axis=-1, keepdims=True)
        o_ref[:, sl] = _dot(p.astype(BF16), mv_ref[0, :, sl],
                               preferred_element_type=F32).astype(o_ref.dtype)


def mem_xattn(qm, gain, mk, mv, b, t, tq=256):
    tq = min(tq, t)
    nt = t // tq
    mtok = mk.shape[1]
    return pl.pallas_call(
        _xattn_kernel,
        grid=(b, nt),
        in_specs=[pl.BlockSpec((tq, MEM_WIDTH), lambda i, j: (i * nt + j, 0)),
                  pl.BlockSpec((1, HEAD_DIM), lambda i, j: (0, 0)),
                  pl.BlockSpec((1, mtok, MEM_WIDTH), lambda i, j: (i, 0, 0)),
                  pl.BlockSpec((1, mtok, MEM_WIDTH), lambda i, j: (i, 0, 0))],
        out_specs=pl.BlockSpec((tq, MEM_WIDTH), lambda i, j: (i * nt + j, 0)),
        out_shape=jax.ShapeDtypeStruct((b * t, MEM_WIDTH), BF16),
        compiler_params=_params(("parallel", "parallel")),
        name="mem_xattn",
    )(qm, gain.reshape(1, HEAD_DIM), mk, mv)


def _kv_transposed(x2d, lo, b, t, heads, tk=FLASH_TK):
    tk = min(tk, t)
    v = x2d[:, lo:lo + heads * HEAD_DIM].astype(BF16)
    return v.reshape(b, t // tk, tk, heads, HEAD_DIM).transpose(0, 3, 1, 4, 2)


def nsa_mixer(pa, pc, b, t, tabs, cmp_tabs, q_norm, kc_norm, ks_norm, kw_norm, pe_k, pe_v,
              ck_w1, ck_w2, cv_w1, cv_w2):
    kvw = NSA_KV_HEADS * HEAD_DIM
    qn = norm_rope(pa, 0, NSA_HEADS, q_norm, tabs, scale=QK_SCALE, name="nsa_q_prep")
    ksn = norm_rope(pa, 6 * kvw, NSA_KV_HEADS, ks_norm, tabs, name="nsa_ks_prep").reshape(b, t, kvw)
    kwn = norm_rope(pa, 8 * kvw, NSA_KV_HEADS, kw_norm, tabs, name="nsa_kw_prep").reshape(b, t, kvw)
    vst = _kv_transposed(pa, 7 * kvw, b, t, NSA_KV_HEADS)
    vwt = _kv_transposed(pa, 9 * kvw, b, t, NSA_KV_HEADS)
    qn = qn.reshape(b, t, NSA_HEADS * HEAD_DIM)

    nc = t // CMP_STRIDE

    def chunked(lo):
        z = pa[:, lo:lo + kvw].astype(BF16).reshape(b, nc, CMP_STRIDE, NSA_KV_HEADS, HEAD_DIM)
        return z.transpose(0, 3, 1, 2, 4).reshape(b, NSA_KV_HEADS, nc, CMP_STRIDE * HEAD_DIM)

    kc = compress(chunked(4 * kvw), ck_w1, ck_w2, pe_k, key_extras=(kc_norm, cmp_tabs), name="compress_k")
    vc = compress(chunked(5 * kvw), cv_w1, cv_w2, pe_v, name="compress_v")
    vct = vc.transpose(0, 1, 3, 2)
    o_c, sel_bias = nsa_compressed(qn, kc, vct, b, t)
    o_s = flash_masked(qn, ksn, vst, mode="sel", group=NSA_GROUP, b=b, t=t, tq=256,
                       extra=sel_bias, name="nsa_selected")
    o_w = flash_masked(qn, kwn, vwt, mode="window", group=NSA_GROUP, b=b, t=t, tq=FLASH_TK,
                       name="nsa_window")
    m = b * t
    w = NSA_HEADS * HEAD_DIM
    return nsa_merge(o_c.reshape(m, w), o_s.reshape(m, w), o_w.reshape(m, w), pc)


def dsa_mixer_core(p1, p2, b, t, tabs, q_norm, k_norm, ki_norm):
    qw = DSA_HEADS * HEAD_DIM
    kvw = DSA_KV_HEADS * HEAD_DIM
    qn = norm_rope(p1, 0, DSA_HEADS, q_norm, tabs, scale=QK_SCALE, name="dsa_q_prep").reshape(b, t, qw)
    kn = norm_rope(p1, qw, DSA_KV_HEADS, k_norm, tabs, name="dsa_k_prep").reshape(b, t, kvw)
    vt = _kv_transposed(p1, qw + kvw, b, t, DSA_KV_HEADS)
    qi = norm_rope(p1, qw + 2 * kvw, IDX_HEADS, None, tabs, name="dsa_qi_prep")
    qi = qi.reshape(b, t, IDX_HEADS * IDX_DIM)
    ki = norm_rope(p2, 0, 1, ki_norm, tabs, name="dsa_ki_prep").reshape(b, t, IDX_DIM)
    wscale = IDX_HEADS ** -0.5 * IDX_DIM ** -0.5
    wt = (p2[:, IDX_DIM:IDX_DIM + IDX_HEADS] * wscale).reshape(b, t, IDX_HEADS).transpose(0, 2, 1)
    key_bias = dsa_select(qi, ki, wt, b, t)
    o = flash_masked(qn, kn, vt, mode="dsa", group=DSA_GROUP, b=b, t=t, tq=LANES,
                     extra=key_bias, out_dtype=BF16, name="dsa_attention")
    return o.reshape(b * t, qw)


def kernel(x, mem, positions, mem_norm, mem_w_kv, mem_k_norm, l0_mix_norm, l0_w_in, l0_w_out, nsa_q_norm, nsa_kc_norm, nsa_ks_norm, nsa_kw_norm, nsa_pe_k, nsa_pe_v, nsa_ck_w1, nsa_ck_w2, nsa_cv_w1, nsa_cv_w2, rwkv_mu, rwkv_w0, rwkv_w2, rwkv_a0, rwkv_a2, rwkv_g2, rwkv_kk, rwkv_ka, rwkv_rk, rwkv_gn_g, rwkv_gn_b, l0_xattn_norm, l0_mem_wq, l0_mem_q_norm, l0_mem_wo, l0_ffn_norm, l0_w1, l0_w3, l0_w2, l1_mix_norm, l1_w_in, l1_w_out, dsa_q_norm, dsa_k_norm, dsa_ki_norm, l1_xattn_norm, l1_mem_wq, l1_mem_q_norm, l1_mem_wo, l1_ffn_norm, l1_w1, l1_w3, l1_w2):
    b, t, d = x.shape
    m = b * t
    bf = lambda z: z.astype(BF16)
    x2 = x.reshape(m, d)

    tabs = rope_tables(positions.reshape(m))
    nc = t // CMP_STRIDE
    cmp_pos = positions[:, CMP_BLOCK - 1::CMP_STRIDE]
    cmp_pos = jnp.concatenate([cmp_pos, cmp_pos[:, -1:]], axis=1)
    cmp_tabs = rope_tables(cmp_pos.reshape(b * nc))

    mtok = mem.shape[1]
    memn = rmsnorm(mem.reshape(b * mtok, d), mem_norm)
    mkv = matmul(memn, mem_w_kv, tm=512, tn=512, name="mem_kv_proj")
    mk = norm_rope(mkv, 0, MEM_HEADS, mem_k_norm, None, name="mem_k_norm").reshape(b, mtok, MEM_WIDTH)
    mv = bf(mkv[:, MEM_WIDTH:]).reshape(b, mtok, MEM_WIDTH)

    def tail(x2, xn, wq, qn, wo, fn, w1, w3, w2):
        h = rmsnorm(x2, xn)
        qm = matmul(h, wq, tm=1024, tn=512, name="xattn_q_proj")
        o = mem_xattn(qm, qn, mk, mv, b, t)
        x2 = matmul(o, wo, tm=1024, tn=512, res=x2, name="xattn_out_proj")
        h = rmsnorm(x2, fn)
        u = matmul(h, [w1, w3], swiglu=True, tm=1024, tn=256, out_dtype=BF16, name="ffn_up")
        return matmul(u, bf(w2), tm=512, tn=512, res=x2, name="ffn_down")

    nsa_cols = NSA_HEADS * HEAD_DIM + 6 * NSA_KV_HEADS * HEAD_DIM
    gate_cols = 3 * NSA_HEADS
    rw0 = nsa_cols + gate_cols
    rw1 = rw0 + 3 * RWKV_WIDTH
    pad_cols = lambda wm: jnp.pad(wm, ((0, 0), (0, LANES - wm.shape[1])))
    w_c = jnp.concatenate([
        pad_cols(l0_w_in[:, nsa_cols:rw0]),
        pad_cols(l0_w_in[:, rw1:rw1 + LORA_DECAY]),
        pad_cols(l0_w_in[:, rw1 + LORA_DECAY:rw1 + LORA_DECAY + LORA_AAA]),
        l0_w_in[:, rw1 + LORA_DECAY + LORA_AAA:]], axis=1)
    h = rmsnorm(x2, l0_mix_norm)
    pa = matmul(h, l0_w_in, ncols=nsa_cols, tm=1024, tn=512, name="l0_proj_nsa")
    w_rkv = bf(l0_w_in[:, rw0:rw1]).reshape(d, 3, RWKV_WIDTH)
    pb = matmul(h, _to_jh(w_rkv).reshape(d, 3 * RWKV_WIDTH), tm=1024, tn=512, name="l0_proj_rwkv")
    pc = matmul(h, bf(w_c), tm=1024, tn=w_c.shape[1], name="l0_proj_small")
    o_a = nsa_mixer(pa, pc, b, t, tabs, cmp_tabs, nsa_q_norm, nsa_kc_norm, nsa_ks_norm, nsa_kw_norm,
                    nsa_pe_k, nsa_pe_v, nsa_ck_w1, nsa_ck_w2, nsa_cv_w1, nsa_cv_w2)
    o_b = rwkv7(pb, pc, b, t, rwkv_mu, rwkv_w0, rwkv_w2, rwkv_a0, rwkv_a2, rwkv_g2, rwkv_kk, rwkv_ka,
                rwkv_rk, rwkv_gn_g, rwkv_gn_b)
    nsa_w = NSA_HEADS * HEAD_DIM
    w_out_rwkv = bf(l0_w_out[nsa_w:]).reshape(RWKV_HEADS, RWKV_HEAD, d).swapaxes(0, 1).reshape(RWKV_WIDTH, d)
    x2 = matmul([o_a, o_b], [l0_w_out[:nsa_w], w_out_rwkv], tm=1024, tn=512, res=x2,
                name="l0_out_proj")
    x2 = tail(x2, l0_xattn_norm, l0_mem_wq, l0_mem_q_norm, l0_mem_wo, l0_ffn_norm, l0_w1, l0_w3, l0_w2)

    main_cols = DSA_HEADS * HEAD_DIM + 2 * DSA_KV_HEADS * HEAD_DIM + IDX_HEADS * IDX_DIM
    w_s = jnp.concatenate([l1_w_in[:, main_cols:main_cols + IDX_DIM],
                           pad_cols(l1_w_in[:, main_cols + IDX_DIM:])], axis=1)
    h = rmsnorm(x2, l1_mix_norm)
    p1 = matmul(h, l1_w_in, ncols=main_cols, tm=1024, tn=512, name="l1_proj_main")
    p2 = matmul(h, bf(w_s), tm=1024, tn=w_s.shape[1], name="l1_proj_small")
    o = dsa_mixer_core(p1, p2, b, t, tabs, dsa_q_norm, dsa_k_norm, dsa_ki_norm)
    x2 = matmul(o, l1_w_out, tm=1024, tn=512, res=x2, name="l1_out_proj")
    x2 = tail(x2, l1_xattn_norm, l1_mem_wq, l1_mem_q_norm, l1_mem_wo, l1_ffn_norm, l1_w1, l1_w3, l1_w2)
    return x2.reshape(b, t, d)
```

```python
import functools

import numpy as np
import jax
import jax.numpy as jnp
from jax import lax
from jax.experimental import pallas as pl
from jax.experimental.pallas import tpu as pltpu

F32 = jnp.float32
BF16 = jnp.bfloat16

HEAD_DIM = 128
ROPE_DIM = HEAD_DIM // 4
ROPE_HALF = ROPE_DIM // 2
ROPE_THETA = 500000.0
NORM_EPS = 1e-6
NEG_INF = -1e30

NSA_HEADS = 16
NSA_KV_HEADS = 4
NSA_GROUP = NSA_HEADS // NSA_KV_HEADS
CMP_BLOCK = 32
CMP_STRIDE = 16
SEL_BLOCK = 64
SEL_TOPN = 16
WINDOW = 512
FORCED_SCORE = 1e4

RWKV_WIDTH = 2048
RWKV_HEAD = 64
RWKV_HEADS = RWKV_WIDTH // RWKV_HEAD
LORA_DECAY = 96
LORA_AAA = 96
LORA_GATE = 256
GN_EPS = 64e-5

DSA_HEADS = 32
DSA_KV_HEADS = 4
DSA_GROUP = DSA_HEADS // DSA_KV_HEADS
IDX_HEADS = 32
IDX_DIM = 128
DSA_TOPK_MAX = 256

MEM_HEADS = 4
MEM_WIDTH = MEM_HEADS * HEAD_DIM

LANES = 128
VMEM_BIG = 56 * 1024 * 1024
VMEM_MID = 40 * 1024 * 1024
INT_MIN = -2 ** 31
MASK_BIG = 1e30
FLASH_TK = 512
QK_SCALE = HEAD_DIM ** -0.5 * float(np.log2(np.e))


def _params(sem, vmem=VMEM_MID):
    return pltpu.CompilerParams(dimension_semantics=sem, vmem_limit_bytes=vmem)


def _sigmoid(x):
    return 1.0 / (1.0 + jnp.exp(-x))


def _dot(a, b, preferred_element_type=F32, precision=None):
    return lax.dot_general(a, b, (((1,), (0,)), ((), ())), precision=precision,
                           preferred_element_type=preferred_element_type)


def _nt_dot(a, b):
    return lax.dot_general(a, b, (((1,), (1,)), ((), ())), preferred_element_type=F32)


def _split_dot(x, w_bf16):
    hi = x.astype(BF16)
    lo = (x - hi.astype(F32)).astype(BF16)
    return (_dot(hi, w_bf16, preferred_element_type=F32)
            + _dot(lo, w_bf16, preferred_element_type=F32))


def _rmsnorm_kernel(x_ref, g_ref, o_ref):
    x = x_ref[...]
    ms = jnp.mean(x * x, axis=-1, keepdims=True)
    o_ref[...] = (x * lax.rsqrt(ms + NORM_EPS) * g_ref[...]).astype(o_ref.dtype)


def rmsnorm(x2d, g, tm=256):
    m, d = x2d.shape
    return pl.pallas_call(
        _rmsnorm_kernel,
        grid=(m // tm,),
        in_specs=[pl.BlockSpec((tm, d), lambda i: (i, 0)),
                  pl.BlockSpec((1, d), lambda i: (0, 0))],
        out_specs=pl.BlockSpec((tm, d), lambda i: (i, 0)),
        out_shape=jax.ShapeDtypeStruct((m, d), BF16),
        compiler_params=_params(("parallel",)),
        name="rmsnorm",
    )(x2d, g.reshape(1, d))


def _mm_kernel(*refs, n_a, swiglu, has_res):
    a_refs = refs[:n_a]
    n_b = 2 if swiglu else n_a
    b_refs = refs[n_a:n_a + n_b]
    res_ref = refs[n_a + n_b] if has_res else None
    o_ref = refs[-1]
    if swiglu:
        a = a_refs[0][...]
        gate = _dot(a, b_refs[0][...].astype(BF16))
        y = gate * _sigmoid(gate) * _dot(a, b_refs[1][...].astype(BF16))
    else:
        y = _dot(a_refs[0][...], b_refs[0][...].astype(BF16))
        for a_ref, b_ref in zip(a_refs[1:], b_refs[1:]):
            y = y + _dot(a_ref[...], b_ref[...].astype(BF16))
    if has_res:
        y = y + res_ref[...]
    o_ref[...] = y.astype(o_ref.dtype)


def matmul(a, b, *, tm, tn, swiglu=False, col0=0, ncols=None, res=None, out_dtype=F32, name="matmul"):
    a_list = list(a) if isinstance(a, (list, tuple)) else [a]
    b_list = list(b) if isinstance(b, (list, tuple)) else [b]
    m = a_list[0].shape[0]
    n = b_list[0].shape[1] - col0 if ncols is None else ncols
    tm = min(tm, m)
    tn = min(tn, n)
    assert m % tm == 0 and n % tn == 0 and col0 % tn == 0
    off = col0 // tn
    in_specs = [pl.BlockSpec((tm, ai.shape[1]), lambda i, j: (i, 0)) for ai in a_list]
    in_specs += [pl.BlockSpec((bi.shape[0], tn), lambda i, j: (0, off + j)) for bi in b_list]
    args = a_list + b_list
    if res is not None:
        in_specs.append(pl.BlockSpec((tm, tn), lambda i, j: (i, j)))
        args.append(res)
    return pl.pallas_call(
        functools.partial(_mm_kernel, n_a=len(a_list), swiglu=swiglu, has_res=res is not None),
        grid=(m // tm, n // tn),
        in_specs=in_specs,
        out_specs=pl.BlockSpec((tm, tn), lambda i, j: (i, j)),
        out_shape=jax.ShapeDtypeStruct((m, n), out_dtype),
        compiler_params=_params(("parallel", "parallel"), VMEM_BIG),
        name=name,
    )(*args)


def _rope_tab_kernel(pos_ref, inv_ref, c_ref, sa_ref, sb_ref):
    ang = pos_ref[...].astype(F32) * inv_ref[...]
    c = jnp.cos(ang)
    s = jnp.sin(ang)
    lane = lax.broadcasted_iota(jnp.int32, ang.shape, 1)
    c_ref[...] = jnp.where(lane < ROPE_DIM, c, 1.0)
    sa_ref[...] = jnp.where(lane < ROPE_HALF, -s, 0.0)
    sb_ref[...] = jnp.where((lane >= ROPE_HALF) & (lane < ROPE_DIM), s, 0.0)


def rope_tables(pos_flat, tm=256):
    n = pos_flat.shape[0]
    tm = min(tm, n)
    inv = ROPE_THETA ** (-jnp.arange(0, ROPE_DIM, 2, dtype=F32) / ROPE_DIM)
    inv_row = jnp.concatenate([inv, inv, jnp.zeros((LANES - ROPE_DIM,), F32)]).reshape(1, LANES)
    pos_b = jnp.broadcast_to(pos_flat[:, None], (n, LANES))
    spec = pl.BlockSpec((tm, LANES), lambda i: (i, 0))
    shp = jax.ShapeDtypeStruct((n, LANES), F32)
    return pl.pallas_call(
        _rope_tab_kernel,
        grid=(n // tm,),
        in_specs=[spec, pl.BlockSpec((1, LANES), lambda i: (0, 0))],
        out_specs=[spec, spec, spec],
        out_shape=[shp, shp, shp],
        compiler_params=_params(("parallel",)),
        name="rope_tables",
    )(pos_b, inv_row)


def _head_norm(xh, g):
    ms = jnp.mean(xh * xh, axis=-1, keepdims=True)
    return xh * lax.rsqrt(ms + NORM_EPS) * g


def _rope(xh, c, sa, sb):
    return (xh * c + pltpu.roll(xh, LANES - ROPE_HALF, 1) * sa
            + pltpu.roll(xh, ROPE_HALF, 1) * sb)


def _norm_rope_kernel(*refs, heads, do_norm, do_rope, scale):
    x_ref = refs[0]
    pos = 1
    g = None
    if do_norm:
        g = refs[pos][...]
        pos += 1
    if do_rope:
        c, sa, sb = refs[pos][...], refs[pos + 1][...], refs[pos + 2][...]
        pos += 3
    o_ref = refs[pos]
    for h in range(heads):
        xh = x_ref[:, h * HEAD_DIM:(h + 1) * HEAD_DIM]
        if do_norm:
            xh = _head_norm(xh, g)
        if do_rope:
            xh = _rope(xh, c, sa, sb)
        if scale != 1.0:
            xh = xh * scale
        o_ref[:, h * HEAD_DIM:(h + 1) * HEAD_DIM] = xh.astype(o_ref.dtype)


def norm_rope(x2d, col_start, heads, gain, tabs, scale=1.0, tm=256, name="norm_rope"):
    m = x2d.shape[0]
    hb = heads
    while col_start % (hb * HEAD_DIM):
        hb //= 2
    w = hb * HEAD_DIM
    off = col_start // w
    tm = min(tm, m)
    in_specs = [pl.BlockSpec((tm, w), lambda i, j: (i, off + j))]
    args = [x2d]
    if gain is not None:
        in_specs.append(pl.BlockSpec((1, HEAD_DIM), lambda i, j: (0, 0)))
        args.append(gain.reshape(1, HEAD_DIM))
    if tabs is not None:
        in_specs += [pl.BlockSpec((tm, LANES), lambda i, j: (i, 0))] * 3
        args += list(tabs)
    return pl.pallas_call(
        functools.partial(_norm_rope_kernel, heads=hb, do_norm=gain is not None,
                          do_rope=tabs is not None, scale=scale),
        grid=(m // tm, heads // hb),
        in_specs=in_specs,
        out_specs=pl.BlockSpec((tm, w), lambda i, j: (i, j)),
        out_shape=jax.ShapeDtypeStruct((m, heads * HEAD_DIM), BF16),
        compiler_params=_params(("parallel", "parallel")),
        name=name,
    )(*args)


def _compress_kernel(*refs, is_key):
    if is_key:
        x_ref, w1_ref, w2_ref, pe_ref, g_ref, c_ref, sa_ref, sb_ref, o_ref = refs
    else:
        x_ref, w1_ref, w2_ref, pe_ref, o_ref = refs
    half = CMP_STRIDE * HEAD_DIM
    x = x_ref[0, 0]
    nc = x.shape[0]
    top = _dot(x, w1_ref[:half, :], preferred_element_type=F32)
    bot = _dot(x, w1_ref[half:, :], preferred_element_type=F32)
    pe_term = _dot(pe_ref[...], w1_ref[...], preferred_element_type=F32)[0:1, :]
    pre = top + pltpu.roll(bot, nc - 1, 0) + pe_term
    hid = jax.nn.gelu(pre)
    out = _dot(hid.astype(BF16), w2_ref[...], preferred_element_type=F32)
    if is_key:
        out = _rope(_head_norm(out, g_ref[...]), c_ref[0], sa_ref[0], sb_ref[0])
    o_ref[0, 0] = out.astype(o_ref.dtype)


def compress(xblk, w1, w2, pe, key_extras=None, name="compress"):
    b, h, nc, wdt = xblk.shape
    pe8 = jnp.broadcast_to(pe.reshape(1, CMP_BLOCK * HEAD_DIM), (8, CMP_BLOCK * HEAD_DIM)).astype(BF16)
    in_specs = [pl.BlockSpec((1, 1, nc, wdt), lambda i, j: (i, j, 0, 0)),
                pl.BlockSpec((CMP_BLOCK * HEAD_DIM, HEAD_DIM), lambda i, j: (0, 0)),
                pl.BlockSpec((HEAD_DIM, HEAD_DIM), lambda i, j: (0, 0)),
                pl.BlockSpec((8, CMP_BLOCK * HEAD_DIM), lambda i, j: (0, 0))]
    args = [xblk, w1.astype(BF16), w2.astype(BF16), pe8]
    if key_extras is not None:
        gain, tabs = key_extras
        in_specs.append(pl.BlockSpec((1, HEAD_DIM), lambda i, j: (0, 0)))
        in_specs += [pl.BlockSpec((1, nc, LANES), lambda i, j: (i, 0, 0))] * 3
        args += [gain.reshape(1, HEAD_DIM)] + [t.reshape(b, nc, LANES) for t in tabs]
    out_dtype = BF16
    return pl.pallas_call(
        functools.partial(_compress_kernel, is_key=key_extras is not None),
        grid=(b, h),
        in_specs=in_specs,
        out_specs=pl.BlockSpec((1, 1, nc, HEAD_DIM), lambda i, j: (i, j, 0, 0)),
        out_shape=jax.ShapeDtypeStruct((b, h, nc, HEAD_DIM), out_dtype),
        compiler_params=_params(("parallel", "parallel")),
        name=name,
    )(*args)


def _stack_heads(q_ref, group, tq):
    return jnp.concatenate([q_ref[0, :, g * HEAD_DIM:(g + 1) * HEAD_DIM] for g in range(group)], axis=0)


def _unstack_heads(o_ref, o_t, group, tq):
    o = o_t.T
    for g in range(group):
        o_ref[0, :, g * HEAD_DIM:(g + 1) * HEAD_DIM] = o[g * tq:(g + 1) * tq, :].astype(o_ref.dtype)


def _cmp_kernel(q_ref, kc_ref, vct_ref, mt_ref, o_ref, sel_ref, *, tq, n_sel, topn):
    group = NSA_GROUP
    t0 = pl.program_id(2) * tq
    q = _stack_heads(q_ref, group, tq)
    kc = kc_ref[0, 0]
    nc = kc.shape[0]
    s = _nt_dot(kc, q)
    c_idx = lax.broadcasted_iota(jnp.int32, (nc, tq), 0)
    t_idx = t0 + lax.broadcasted_iota(jnp.int32, (nc, tq), 1)
    ok1 = (CMP_STRIDE * c_idx + CMP_BLOCK - 1) <= t_idx
    bias1 = jnp.where(ok1, 0.0, NEG_INF)
    okf1 = jnp.where(ok1, 1.0, 0.0)
    bias = jnp.concatenate([bias1] * group, axis=1)
    okf = jnp.concatenate([okf1] * group, axis=1)
    s = s + bias
    m = jnp.max(s, axis=0, keepdims=True)
    e = jnp.exp2(s - m) * okf
    l = jnp.sum(e, axis=0, keepdims=True)
    p = e * jnp.where(l > 0.0, 1.0 / l, 0.0)
    o_t = _dot(vct_ref[0, 0], p.astype(BF16), preferred_element_type=F32)
    _unstack_heads(o_ref, o_t, group, tq)

    psum = p[:, 0:tq]
    for g in range(1, group):
        psum = psum + p[:, g * tq:(g + 1) * tq]
    imp = _dot(mt_ref[...], psum, preferred_element_type=F32,
                  precision=lax.Precision.HIGHEST)
    j_idx = lax.broadcasted_iota(jnp.int32, (n_sel, tq), 0)
    jt = (t0 + lax.broadcasted_iota(jnp.int32, (n_sel, tq), 1)) // SEL_BLOCK
    forced = (j_idx == 0) | (j_idx == jt) | (j_idx == jt - 1)
    imp = jnp.where(forced, FORCED_SCORE, imp)
    imp = jnp.where(j_idx <= jt, imp, -jnp.inf)
    rank = jnp.zeros((n_sel, tq), F32)
    for jp in range(n_sel):
        row = imp[jp:jp + 1, :]
        before = (row > imp) | ((row == imp) & (j_idx > jp))
        rank = rank + jnp.where(before, 1.0, 0.0)
    bias_t = jnp.where(rank < topn, 0.0, -MASK_BIG)
    bias_t = jnp.concatenate([bias_t, jnp.zeros((LANES - n_sel, tq), F32)], axis=0)
    sel_ref[0, 0] = bias_t.T.astype(sel_ref.dtype)


def nsa_compressed(qn, kc, vct, b, t, tq=256):
    nc = kc.shape[2]
    n_sel = t // SEL_BLOCK
    assert n_sel <= LANES
    topn = min(SEL_TOPN, n_sel)
    tq = min(tq, t)
    cs = CMP_STRIDE * np.arange(nc)[:, None]
    ss = SEL_BLOCK * np.arange(n_sel)[None, :]
    ov = np.clip(np.minimum(cs + CMP_BLOCK, ss + SEL_BLOCK) - np.maximum(cs, ss), 0, None) / CMP_BLOCK
    ov[nc - 1, :] = 0.0
    mt = jnp.asarray(ov.T, dtype=F32)
    gw = NSA_GROUP * HEAD_DIM
    return pl.pallas_call(
        functools.partial(_cmp_kernel, tq=tq, n_sel=n_sel, topn=topn),
        grid=(b, NSA_KV_HEADS, t // tq),
        in_specs=[pl.BlockSpec((1, tq, gw), lambda i, h, j: (i, j, h)),
                  pl.BlockSpec((1, 1, nc, HEAD_DIM), lambda i, h, j: (i, h, 0, 0)),
                  pl.BlockSpec((1, 1, HEAD_DIM, nc), lambda i, h, j: (i, h, 0, 0)),
                  pl.BlockSpec((n_sel, nc), lambda i, h, j: (0, 0))],
        out_specs=[pl.BlockSpec((1, tq, gw), lambda i, h, j: (i, j, h)),
                   pl.BlockSpec((1, 1, tq, LANES), lambda i, h, j: (i, h, j, 0))],
        out_shape=[jax.ShapeDtypeStruct((b, t, NSA_HEADS * HEAD_DIM), F32),
                   jax.ShapeDtypeStruct((b, NSA_KV_HEADS, t, LANES), BF16)],
        compiler_params=_params(("parallel", "parallel", "parallel")),
        name="nsa_compressed",
    )(qn, kc, vct, mt)


def _flash_kernel(*refs, mode, group, slab, tq, tk):
    if mode == "window":
        q_ref, k_ref, vt_ref, o_ref, q_s, m_s, l_s, acc_s, sa_s, sb_s, p_s = refs
        xq = None
    else:
        q_ref, k_ref, vt_ref, xq_ref, xk_ref, o_ref, q_s, m_s, l_s, acc_s, sa_s, sb_s, p_s = refs
        xq = xq_ref[0, 0] if mode == "sel" else xq_ref[...]
    qi = pl.program_id(2)
    t0 = qi * tq
    q = _stack_heads(q_ref, group, tq)
    if xq is not None:
        q = jnp.concatenate([q, jnp.concatenate([xq] * group, axis=0)], axis=1)
    q_s[...] = q
    m_s[...] = jnp.full(m_s.shape, NEG_INF, F32)
    l_s[...] = jnp.zeros(l_s.shape, F32)
    acc_s[...] = jnp.zeros(acc_s.shape, F32)

    def scores(j):
        k0 = pl.multiple_of(j * tk, tk)
        kt = k_ref[0, pl.ds(k0, tk), :]
        if mode == "sel":
            kt = jnp.concatenate([kt, xk_ref[pl.ds(k0, tk), :]], axis=1)
        elif mode == "dsa":
            kt = jnp.concatenate([kt, xk_ref[0, pl.ds(k0, tk), :]], axis=1)
        return _nt_dot(kt, q_s[...])

    def absorb(s_ref, j, keep):
        if keep is not None:
            kpos = j * tk + lax.broadcasted_iota(jnp.int32, (tk, tq), 0)
            tpos = t0 + lax.broadcasted_iota(jnp.int32, (tk, tq), 1)
            ok = keep(kpos, tpos)
            for g in range(group):
                cols = slice(g * tq, (g + 1) * tq)
                s_ref[:, cols] = jnp.where(ok, s_ref[:, cols], NEG_INF)
        m_old = m_s[...]
        m_new = jnp.maximum(m_old, jnp.max(s_ref[...], axis=0, keepdims=True))
        alpha = jnp.exp2(m_old - m_new)
        p = jnp.exp2(s_ref[...] - m_new)
        l_s[...] = alpha * l_s[...] + jnp.sum(p, axis=0, keepdims=True)
        p_s[...] = p.astype(BF16)
        acc_s[...] = acc_s[...] * alpha + _dot(vt_ref[0, 0, j], p_s[...])
        m_s[...] = m_new

    causal = lambda kpos, tpos: kpos <= tpos

    if mode == "window":
        nw = WINDOW // tk
        edge = lambda kpos, tpos: tpos - kpos < WINDOW
        keeps = [edge] + [None] * (nw - 1) + [causal]

        @pl.when(qi >= nw)
        def _():
            bufs = (sa_s, sb_s)
            sa_s[...] = scores(qi - nw)
            for n, keep in enumerate(keeps):
                if n + 1 < len(keeps):
                    bufs[(n + 1) % 2][...] = scores(qi - nw + n + 1)
                absorb(bufs[n % 2], qi - nw + n, keep)

        @pl.when(qi < nw)
        def _():
            def early(j, carry):
                sa_s[...] = scores(j)
                absorb(sa_s, j, None)
                return carry
            lax.fori_loop(0, qi, early, 0)
            sa_s[...] = scores(qi)
            absorb(sa_s, qi, causal)
    else:
        last = (t0 + tq - 1) // tk
        last_keep = causal if mode == "sel" else None
        sa_s[...] = scores(0)

        def pair(i, carry):
            j = 2 * i
            sb_s[...] = scores(j + 1)
            absorb(sa_s, j, None)
            sa_s[...] = scores(j + 2)
            absorb(sb_s, j + 1, None)
            return carry

        lax.fori_loop(0, last // 2, pair, 0)

        @pl.when(last % 2 == 1)
        def _():
            sb_s[...] = scores(last)
            absorb(sa_s, last - 1, None)
            absorb(sb_s, last, last_keep)

        @pl.when(last % 2 == 0)
        def _():
            absorb(sa_s, last, last_keep)

    _unstack_heads(o_ref, acc_s[...] * (1.0 / l_s[...]), group, tq)


def flash_masked(qn, kn, vt5, *, mode, group, b, t, tq, slab=None, extra=None, out_dtype=F32, name="flash"):
    kvh = kn.shape[2] // HEAD_DIM
    gw = group * HEAD_DIM
    tq = min(tq, t)
    ntk, tk = vt5.shape[2], vt5.shape[4]
    slab = group if slab is None else slab
    in_specs = [pl.BlockSpec((1, tq, gw), lambda i, h, qi: (i, qi, h)),
                pl.BlockSpec((1, t, HEAD_DIM), lambda i, h, qi: (i, 0, h)),
                pl.BlockSpec((1, 1, ntk, HEAD_DIM, tk), lambda i, h, qi: (i, h, 0, 0, 0))]
    args = [qn, kn, vt5]
    kdim = HEAD_DIM
    if mode == "sel":
        assert tk % tq == 0
        et = np.zeros((t, LANES), np.float32)
        et[np.arange(t), np.arange(t) // SEL_BLOCK] = 1.0
        in_specs += [pl.BlockSpec((1, 1, tq, LANES), lambda i, h, qi: (i, h, qi, 0)),
                     pl.BlockSpec((t, LANES), lambda i, h, qi: (0, 0))]
        args += [extra, jnp.asarray(et, dtype=BF16)]
        kdim += LANES
    elif mode == "dsa":
        assert tq == LANES
        in_specs += [pl.BlockSpec((tq, LANES), lambda i, h, qi: (0, 0)),
                     pl.BlockSpec((1, t, tq), lambda i, h, qi: (i, 0, qi))]
        args += [jnp.eye(tq, dtype=BF16), extra]
        kdim += LANES
    else:
        assert tk == tq and WINDOW % tk == 0
    return pl.pallas_call(
        functools.partial(_flash_kernel, mode=mode, group=group, slab=slab, tq=tq, tk=tk),
        grid=(b, kvh, t // tq),
        in_specs=in_specs,
        out_specs=pl.BlockSpec((1, tq, gw), lambda i, h, qi: (i, qi, h)),
        out_shape=jax.ShapeDtypeStruct((b, t, kvh * gw), out_dtype),
        scratch_shapes=[pltpu.VMEM((group * tq, kdim), BF16),
                        pltpu.VMEM((1, group * tq), F32),
                        pltpu.VMEM((1, group * tq), F32),
                        pltpu.VMEM((HEAD_DIM, group * tq), F32),
                        pltpu.VMEM((tk, group * tq), F32),
                        pltpu.VMEM((tk, group * tq), F32),
                        pltpu.VMEM((tk, group * tq), BF16)],
        compiler_params=_params(("parallel", "parallel", "parallel"), VMEM_BIG),
        name=name,
    )(*args)


def _nsa_merge_kernel(oc_ref, os_ref, ow_ref, g_ref, e_ref, o_ref):
    gate = _sigmoid(g_ref[...])
    out = None
    for j, src in enumerate((oc_ref, os_ref, ow_ref)):
        gj = _split_dot(gate, e_ref[j])
        term = gj * src[...]
        out = term if out is None else out + term
    o_ref[...] = out.astype(o_ref.dtype)


def nsa_merge(oc, os_, ow, pc, tm=256):
    m, w = oc.shape
    tm = min(tm, m)
    e = np.zeros((3, LANES, w), np.float32)
    for h in range(NSA_HEADS):
        for j in range(3):
            e[j, h * 3 + j, h * HEAD_DIM:(h + 1) * HEAD_DIM] = 1.0
    spec = pl.BlockSpec((tm, w), lambda i: (i, 0))
    return pl.pallas_call(
        _nsa_merge_kernel,
        grid=(m // tm,),
        in_specs=[spec, spec, spec,
                  pl.BlockSpec((tm, LANES), lambda i: (i, 0)),
                  pl.BlockSpec((3, LANES, w), lambda i: (0, 0, 0))],
        out_specs=spec,
        out_shape=jax.ShapeDtypeStruct((m, w), BF16),
        compiler_params=_params(("parallel",)),
        name="nsa_merge",
    )(oc, os_, ow, pc, jnp.asarray(e, dtype=BF16))


SUB = 8
HEADS_PER_SLAB = LANES // RWKV_HEADS
N_SLABS = RWKV_WIDTH // LANES


def _to_jh(p):
    lead = p.shape[:-1]
    return p.reshape(*lead, RWKV_HEADS, RWKV_HEAD).swapaxes(-1, -2).reshape(*lead, RWKV_WIDTH)


def _lane_table(p, nb):
    tab = p.reshape(RWKV_HEADS, RWKV_HEAD).T
    return jnp.tile(tab, (1, nb)).reshape(RWKV_HEAD, 1, LANES)


def _head_sum_jh(x, bd):
    acc = x[:, 0:LANES]
    for s in range(1, N_SLABS):
        acc = acc + x[:, s * LANES:(s + 1) * LANES]
    tot = _split_dot(acc, bd)
    return jnp.concatenate([tot] * N_SLABS, axis=1)


def _interleave(xs, out_ref):
    nb = len(xs)
    tt = xs[0].shape[0]
    group = (lax.broadcasted_iota(jnp.int32, xs[0].shape, 1) % LANES) // RWKV_HEADS
    moved = {}
    for delta in range(1 - nb, nb):
        if delta == 0:
            continue
        src = None
        for g in range(max(0, -delta), min(nb, nb - delta)):
            src = xs[g + delta] if src is None else jnp.where(group == g, xs[g + delta], src)
        moved[delta] = pltpu.roll(src, (delta * RWKV_HEADS) % RWKV_WIDTH, 1)
    for jj in range(HEADS_PER_SLAB):
        z = xs[jj]
        for bi in range(nb):
            if bi != jj:
                z = jnp.where(group == bi, moved[bi - jj], z)
        for tb in range(tt // SUB):
            for s in range(N_SLABS):
                out_ref[tb, HEADS_PER_SLAB * s + jj] = z[tb * SUB:(tb + 1) * SUB, s * LANES:(s + 1) * LANES]


def _deinterleave(val, o_ref):
    nb = o_ref.shape[0]
    tt = o_ref.shape[1]
    zs = []
    for jj in range(HEADS_PER_SLAB):
        z = jnp.concatenate([val[:, HEADS_PER_SLAB * s + jj] for s in range(N_SLABS)], axis=-1)
        zs.append(z.reshape(tt, RWKV_WIDTH))
    lane = lax.broadcasted_iota(jnp.int32, (tt, RWKV_WIDTH), 1) % LANES
    for bi in range(nb):
        out = None
        for jj in range(HEADS_PER_SLAB):
            shift = ((jj - bi) * RWKV_HEADS) % RWKV_WIDTH
            zj = zs[jj] if shift == 0 else pltpu.roll(zs[jj], shift, 1)
            out = zj if out is None else jnp.where(lane // RWKV_HEADS == jj, zj, out)
        o_ref[bi] = out.astype(o_ref.dtype)


def _shifted(x, prev_row, first):
    rolled = pltpu.roll(x, 1, 0)
    row0 = jnp.where(first, 0.0, prev_row)
    ridx = lax.broadcasted_iota(jnp.int32, x.shape, 0)
    return jnp.where(ridx == 0, row0, rolled)


def _rwkv_prep_kernel(pb_ref, pbp_ref, pc_ref, pcp_ref, mub_ref, muc_ref, w0_ref, w2_ref, a0_ref, a2_ref,
                      g2_ref, kkg_ref, ka_ref, bd_ref,
                      r_ref, w_ref, k_ref, v_ref, kk_ref, b_ref, g_ref):
    first = pl.program_id(0) == 0
    w = RWKV_WIDTH
    nb = pb_ref.shape[0]
    outs = [[] for _ in range(7)]
    for bi in range(nb):
        xb = pb_ref[bi]
        xb = xb + (_shifted(xb, pbp_ref[bi, SUB - 1:SUB, :], first) - xb) * mub_ref[...]
        xc = pc_ref[bi]
        xc = xc + (_shifted(xc, pcp_ref[bi, SUB - 1:SUB, :], first) - xc) * muc_ref[...]
        r, k, v = xb[:, :w], xb[:, w:2 * w], xb[:, 2 * w:]
        wd, ad, gd = xc[:, LANES:2 * LANES], xc[:, 2 * LANES:3 * LANES], xc[:, 3 * LANES:]
        z = w0_ref[...] + _dot(jnp.tanh(wd).astype(BF16), w2_ref[...])
        nz = -z
        softplus = jnp.maximum(nz, 0.0) + jnp.log(1.0 + jnp.exp(-jnp.abs(nz)))
        w_log = -softplus - 0.5
        a = _sigmoid(a0_ref[...] + _dot(ad.astype(BF16), a2_ref[...]))
        g = _dot(_sigmoid(gd).astype(BF16), g2_ref[...])
        kkv = k * kkg_ref[...]
        norm = jnp.sqrt(_head_sum_jh(kkv * kkv, bd_ref[...]))
        kkv = kkv / jnp.maximum(norm, 1e-12)
        vals = (r, jnp.exp(-jnp.exp(w_log)), k * (1.0 + (a - 1.0) * ka_ref[...]), v, kkv, kkv * a, g)
        for lst, val in zip(outs, vals):
            lst.append(val)
    for lst, ref in zip(outs, (r_ref, w_ref, k_ref, v_ref, kk_ref, b_ref, g_ref)):
        _interleave(lst, ref)


def _pad_rows(wm, rows):
    return jnp.pad(wm, ((0, rows - wm.shape[0]), (0, 0)))


def _same_head_ones():
    idx = np.arange(LANES) % RWKV_HEADS
    return jnp.asarray(idx[:, None] == idx[None, :], dtype=BF16)


def rwkv_prep(pb, pc, b, t, mu, w0, w2, a0, a2, g2, kk_gain, k_a, tt=32):
    w = RWKV_WIDTH
    assert b * RWKV_HEADS == LANES and t % tt == 0 and tt % SUB == 0
    mu_r, mu_k, mu_v, mu_wd, mu_ad, mu_gd = jnp.split(
        mu, [int(x) for x in np.cumsum([w, w, w, LORA_DECAY, LORA_AAA])])
    mub = jnp.concatenate([_to_jh(mu_r), _to_jh(mu_k), _to_jh(mu_v)]).reshape(1, 3 * w)
    pad = lambda z: jnp.pad(z, (0, LANES - z.shape[0]))
    muc = jnp.concatenate([jnp.zeros((LANES,), F32), pad(mu_wd), pad(mu_ad), mu_gd]).reshape(1, -1)
    cw = pc.shape[1]
    row = lambda z: _to_jh(z).reshape(1, w)
    lora = lambda wm, rows: _to_jh(_pad_rows(wm, rows)).astype(BF16)
    full = lambda shape: pl.BlockSpec(shape, lambda i: (0,) * len(shape))
    tile = lambda width: pl.BlockSpec((b, tt, width), lambda i: (0, i, 0))
    prev = lambda width: pl.BlockSpec((b, SUB, width), lambda i: (0, jnp.maximum(i * (tt // SUB) - 1, 0), 0))
    out_spec = pl.BlockSpec((tt // SUB, RWKV_HEAD, SUB, LANES), lambda i: (i, 0, 0, 0))
    shp = jax.ShapeDtypeStruct((t // SUB, RWKV_HEAD, SUB, LANES), F32)
    pb3 = pb.reshape(b, t, 3 * w)
    pc3 = pc.reshape(b, t, cw)
    return pl.pallas_call(
        _rwkv_prep_kernel,
        grid=(t // tt,),
        in_specs=[tile(3 * w), prev(3 * w), tile(cw), prev(cw), full((1, 3 * w)), full((1, cw)),
                  full((1, w)), full((LANES, w)), full((1, w)), full((LANES, w)), full((LORA_GATE, w)),
                  full((1, w)), full((1, w)), full((LANES, LANES))],
        out_specs=[out_spec] * 7,
        out_shape=[shp] * 7,
        compiler_params=_params(("parallel",), VMEM_BIG),
        name="rwkv_prep",
    )(pb3, pb3, pc3, pc3, mub, muc, row(w0), lora(w2, LANES), row(a0), lora(a2, LANES), lora(g2, LORA_GATE),
      row(kk_gain), row(k_a), _same_head_ones())


def _rwkv_scan_kernel(w_ref, kk_ref, b_ref, k_ref, r_ref, v_ref, y_ref, s_ref, op_s, p_s, *, tb, rows):
    ig = pl.program_id(1)
    blk = SUB * RWKV_HEAD

    @pl.when(pl.program_id(0) == 0)
    def _():
        s_ref[ig] = jnp.zeros(s_ref.shape[1:], F32)

    @pl.when(ig == 0)
    def _():
        def gather(u, p):
            base = pl.multiple_of(u * blk, blk)
            for q in range(SUB):
                key = pl.ds(base + q, RWKV_HEAD, stride=SUB)
                t = u * SUB + q
                op_s[0, t] = -kk_ref[key, :] * p
                p = p * w_ref[key, :]
                inv = 1.0 / p
                op_s[1, t] = b_ref[key, :] * inv
                op_s[2, t] = k_ref[key, :] * inv
                op_s[3, t] = r_ref[key, :] * p
            return p
        p_s[...] = lax.fori_loop(0, tb // SUB, gather, jnp.ones(p_s.shape, F32))

    def steps(u, z):
        base = pl.multiple_of(u * blk, blk) + ig * (rows * SUB)
        for q in range(SUB):
            t = u * SUB + q
            val = pl.ds(base + q, rows, stride=SUB)
            sa = jnp.sum(z * op_s[0, t][None], axis=1)
            z = z + sa[:, None, :] * op_s[1, t][None] + v_ref[val, :][:, None, :] * op_s[2, t][None]
            y_ref[val, :] = jnp.sum(z * op_s[3, t][None], axis=1)
        return z

    s_ref[ig] = lax.fori_loop(0, tb // SUB, steps, s_ref[ig]) * p_s[...][None]


def rwkv_scan(w, kk, bb, k, r, v, tb=64, rows=4):
    n = RWKV_HEAD
    t = w.shape[0] // n
    groups = n // rows
    tb = min(tb, t)
    op = pl.BlockSpec((tb * n, LANES), lambda i, j: (i, 0))
    return pl.pallas_call(
        functools.partial(_rwkv_scan_kernel, tb=tb, rows=rows),
        grid=(t // tb, groups),
        in_specs=[op] * 6,
        out_specs=op,
        out_shape=jax.ShapeDtypeStruct((t * n, LANES), F32),
        scratch_shapes=[pltpu.VMEM((groups, rows, n, LANES), F32),
                        pltpu.VMEM((4, tb, n, LANES), F32),
                        pltpu.VMEM((n, LANES), F32)],
        compiler_params=_params(("arbitrary", "arbitrary"), VMEM_BIG),
        name="rwkv_scan",
    )(w, kk, bb, k, r, v)


def _rwkv_post_kernel(y_ref, r_ref, k_ref, v_ref, g_ref, rk_ref, gg_ref, gb_ref, o_ref):
    y = y_ref[...]
    inv_n = 1.0 / RWKV_HEAD
    mean = jnp.sum(y, axis=1, keepdims=True) * inv_n
    d = y - mean
    var = jnp.sum(d * d, axis=1, keepdims=True) * inv_n
    yn = d * lax.rsqrt(var + GN_EPS) * gg_ref[...][None] + gb_ref[...][None]
    bonus = jnp.sum(r_ref[...] * k_ref[...] * rk_ref[...][None], axis=1, keepdims=True) * v_ref[...]
    _deinterleave((yn + bonus) * g_ref[...], o_ref)


def rwkv_post(y, r, k, v, g, b, t, r_k, gn_g, gn_b, tt=32):
    tile = pl.BlockSpec((tt // SUB, RWKV_HEAD, SUB, LANES), lambda i: (i, 0, 0, 0))
    tab = pl.BlockSpec((RWKV_HEAD, 1, LANES), lambda i: (0, 0, 0))
    return pl.pallas_call(
        _rwkv_post_kernel,
        grid=(t // tt,),
        in_specs=[tile] * 5 + [tab] * 3,
        out_specs=pl.BlockSpec((b, tt, RWKV_WIDTH), lambda i: (0, i, 0)),
        out_shape=jax.ShapeDtypeStruct((b, t, RWKV_WIDTH), BF16),
        compiler_params=_params(("parallel",)),
        name="rwkv_post",
    )(y, r, k, v, g, _lane_table(r_k.reshape(-1), b), _lane_table(gn_g, b), _lane_table(gn_b, b))


def rwkv7(pb, pc, b, t, mu, w0, w2, a0, a2, g2, kk_gain, k_a, r_k, gn_g, gn_b):
    r, w, k, v, kk, bb, g = rwkv_prep(pb, pc, b, t, mu, w0, w2, a0, a2, g2, kk_gain, k_a)
    rows2d = lambda z: z.reshape(t * RWKV_HEAD, LANES)
    y = rwkv_scan(rows2d(w), rows2d(kk), rows2d(bb), rows2d(k), rows2d(r), rows2d(v))
    y = y.reshape(t // SUB, RWKV_HEAD, SUB, LANES)
    return rwkv_post(y, r, k, v, g, b, t, r_k, gn_g, gn_b).reshape(b * t, RWKV_WIDTH)


def _dsa_select_kernel(qi_ref, ki_ref, wt_ref, bias_ref, q_s, key_s, *, t, tq, kc, topk):
    t0 = pl.program_id(1) * tq
    nchunk = (t0 + tq - 1) // kc + 1
    for h in range(IDX_HEADS):
        q_s[h * tq:(h + 1) * tq, :] = qi_ref[0, :, h * IDX_DIM:(h + 1) * IDX_DIM]
    wt = wt_ref[0]

    def chunk(c, carry):
        k0 = pl.multiple_of(c * kc, kc)
        lg = _nt_dot(ki_ref[0, pl.ds(k0, kc), :], q_s[...])
        acc = jnp.zeros((kc, tq), F32)
        for h in range(IDX_HEADS):
            acc = acc + jnp.maximum(lg[:, h * tq:(h + 1) * tq], 0.0) * wt[h:h + 1, :]
        spos = k0 + lax.broadcasted_iota(jnp.int32, (kc, tq), 0)
        tpos = t0 + lax.broadcasted_iota(jnp.int32, (kc, tq), 1)
        score = jnp.where(spos <= tpos, acc + 0.0, -jnp.inf)
        bits = pltpu.bitcast(score, jnp.int32)
        key_s[pl.ds(k0, kc), :] = jnp.where(bits < 0, bits ^ jnp.int32(0x7FFFFFFF), bits)
        return carry

    lax.fori_loop(0, nchunk, chunk, 0)
    kf = jnp.float32(topk)
    part = 64

    def count(pred):
        def body(c, acc):
            k0 = pl.multiple_of(c * kc, kc)
            ind = jnp.where(pred(key_s[pl.ds(k0, kc), :], k0), 1.0, 0.0)
            return acc + jnp.sum(ind.reshape(kc // part, part, tq), axis=0)
        acc = lax.fori_loop(0, nchunk, body, jnp.zeros((part, tq), F32))
        return jnp.sum(acc, axis=0, keepdims=True)

    def value_bit(it, tau_u):
        cand_u = tau_u | lax.shift_left(jnp.int32(1), 31 - it)
        cand_s = cand_u ^ jnp.int32(INT_MIN)
        return jnp.where(count(lambda blk, k0: blk >= cand_s) >= kf, cand_u, tau_u)

    tau_u = lax.fori_loop(0, 32, value_bit, jnp.zeros((1, tq), jnp.int32))
    tau = tau_u ^ jnp.int32(INT_MIN)
    need = kf - count(lambda blk, k0: blk > tau)
    n_eq = count(lambda blk, k0: blk == tau)
    nbits = int(np.log2(t))

    def tie_search():
        def index_bit(it, jj):
            cand = jj | lax.shift_left(jnp.int32(1), nbits - 1 - it)

            def pred(blk, k0):
                spos = k0 + lax.broadcasted_iota(jnp.int32, (kc, tq), 0)
                return (blk == tau) & (spos < cand)
            return jnp.where(count(pred) < need, cand, jj)
        return lax.fori_loop(0, nbits, index_bit, jnp.zeros((1, tq), jnp.int32))

    jj = lax.cond(jnp.max(n_eq - need) > 0.0, tie_search, lambda: jnp.full((1, tq), t, jnp.int32))

    def emit(c, carry):
        k0 = pl.multiple_of(c * kc, kc)
        blk = key_s[pl.ds(k0, kc), :]
        spos = k0 + lax.broadcasted_iota(jnp.int32, (kc, tq), 0)
        tpos = t0 + lax.broadcasted_iota(jnp.int32, (kc, tq), 1)
        keep = ((blk > tau) | ((blk == tau) & (spos <= jj))) & (spos <= tpos)
        bias_ref[0, pl.ds(k0, kc), :] = jnp.where(keep, 0.0, -MASK_BIG).astype(bias_ref.dtype)
        return carry

    lax.fori_loop(0, nchunk, emit, 0)

    def fill(c, carry):
        k0 = pl.multiple_of(c * kc, kc)
        bias_ref[0, pl.ds(k0, kc), :] = jnp.full((kc, tq), -MASK_BIG, bias_ref.dtype)
        return carry

    lax.fori_loop(nchunk, t // kc, fill, 0)


def dsa_select(qi, ki, wt, b, t, tq=128, kc=512):
    topk = min(DSA_TOPK_MAX, t // 4)
    kc = min(kc, t)
    assert topk <= kc and t % kc == 0
    return pl.pallas_call(
        functools.partial(_dsa_select_kernel, t=t, tq=tq, kc=kc, topk=topk),
        grid=(b, t // tq),
        in_specs=[pl.BlockSpec((1, tq, IDX_HEADS * IDX_DIM), lambda i, j: (i, j, 0)),
                  pl.BlockSpec((1, t, IDX_DIM), lambda i, j: (i, 0, 0)),
                  pl.BlockSpec((1, IDX_HEADS, tq), lambda i, j: (i, 0, j))],
        out_specs=pl.BlockSpec((1, t, tq), lambda i, j: (i, 0, j)),
        out_shape=jax.ShapeDtypeStruct((b, t, t), BF16),
        scratch_shapes=[pltpu.VMEM((IDX_HEADS * tq, IDX_DIM), BF16),
                        pltpu.VMEM((t, tq), jnp.int32)],
        compiler_params=_params(("parallel", "parallel"), VMEM_BIG),
        name="dsa_select",
    )(qi, ki, wt)


def _row_norm(x, g):
    ms = jnp.mean(x * x, axis=-1, keepdims=True)
    return x * lax.rsqrt(ms + NORM_EPS) * g


def _xattn_block_kernel(x_ref, xn_ref, wq_ref, qn_ref, mk_ref, mv_ref, wo_ref, fn_ref, xo_ref, hn_ref):
    x = x_ref[...]
    q = _dot(_row_norm(x, xn_ref[...]).astype(BF16), wq_ref[...])
    scale = HEAD_DIM ** -0.5
    heads = []
    for h in range(MEM_HEADS):
        sl = slice(h * HEAD_DIM, (h + 1) * HEAD_DIM)
        qh = (_head_norm(q[:, sl], qn_ref[...]) * scale).astype(BF16)
        s = _nt_dot(qh, mk_ref[0, :, sl])
        m = jnp.max(s, axis=-1, keepdims=True)
        e = jnp.exp(s - m)
        p = e / jnp.sum(e, axis=-1, keepdims=True)
        heads.append(_dot(p.astype(BF16), mv_ref[0, :, sl]).astype(BF16))
    xo = x + _dot(jnp.concatenate(heads, axis=1), wo_ref[...])
    xo_ref[...] = xo
    hn_ref[...] = _row_norm(xo, fn_ref[...]).astype(hn_ref.dtype)


def mem_xattn_block(x2, xn, wq, qn, mk, mv, wo, fn, b, t, tm=256):
    m, d = x2.shape
    tm = min(tm, t)
    nt = t // tm
    mtok = mk.shape[1]
    rows = pl.BlockSpec((tm, d), lambda i, j: (i * nt + j, 0))
    full = lambda shape: pl.BlockSpec(shape, lambda i, j: (0,) * len(shape))
    mem = pl.BlockSpec((1, mtok, MEM_WIDTH), lambda i, j: (i, 0, 0))
    return pl.pallas_call(
        _xattn_block_kernel,
        grid=(b, nt),
        in_specs=[rows, full((1, d)), full((d, MEM_WIDTH)), full((1, HEAD_DIM)), mem, mem,
                  full((MEM_WIDTH, d)), full((1, d))],
        out_specs=[rows, rows],
        out_shape=[jax.ShapeDtypeStruct((m, d), F32), jax.ShapeDtypeStruct((m, d), BF16)],
        compiler_params=_params(("parallel", "parallel"), VMEM_BIG),
        name="mem_xattn_block",
    )(x2, xn.reshape(1, d), wq, qn.reshape(1, HEAD_DIM), mk, mv, wo, fn.reshape(1, d))


def _kv_transposed(x2d, lo, b, t, heads, tk=FLASH_TK):
    tk = min(tk, t)
    v = x2d[:, lo:lo + heads * HEAD_DIM].astype(BF16)
    return v.reshape(b, t // tk, tk, heads, HEAD_DIM).transpose(0, 3, 1, 4, 2)


def nsa_mixer(pa, pc, b, t, tabs, cmp_tabs, q_norm, kc_norm, ks_norm, kw_norm, pe_k, pe_v,
              ck_w1, ck_w2, cv_w1, cv_w2):
    kvw = NSA_KV_HEADS * HEAD_DIM
    qn = norm_rope(pa, 0, NSA_HEADS, q_norm, tabs, scale=QK_SCALE, name="nsa_q_prep")
    ksn = norm_rope(pa, 6 * kvw, NSA_KV_HEADS, ks_norm, tabs, name="nsa_ks_prep").reshape(b, t, kvw)
    kwn = norm_rope(pa, 8 * kvw, NSA_KV_HEADS, kw_norm, tabs, name="nsa_kw_prep").reshape(b, t, kvw)
    vst = _kv_transposed(pa, 7 * kvw, b, t, NSA_KV_HEADS)
    vwt = _kv_transposed(pa, 9 * kvw, b, t, NSA_KV_HEADS)
    qn = qn.reshape(b, t, NSA_HEADS * HEAD_DIM)

    nc = t // CMP_STRIDE

    def chunked(lo):
        z = pa[:, lo:lo + kvw].astype(BF16).reshape(b, nc, CMP_STRIDE, NSA_KV_HEADS, HEAD_DIM)
        return z.transpose(0, 3, 1, 2, 4).reshape(b, NSA_KV_HEADS, nc, CMP_STRIDE * HEAD_DIM)

    kc = compress(chunked(4 * kvw), ck_w1, ck_w2, pe_k, key_extras=(kc_norm, cmp_tabs), name="compress_k")
    vc = compress(chunked(5 * kvw), cv_w1, cv_w2, pe_v, name="compress_v")
    vct = vc.transpose(0, 1, 3, 2)
    o_c, sel_bias = nsa_compressed(qn, kc, vct, b, t)
    o_s = flash_masked(qn, ksn, vst, mode="sel", group=NSA_GROUP, b=b, t=t, tq=256,
                       extra=sel_bias, name="nsa_selected")
    o_w = flash_masked(qn, kwn, vwt, mode="window", group=NSA_GROUP, b=b, t=t, tq=FLASH_TK,
                       name="nsa_window")
    m = b * t
    w = NSA_HEADS * HEAD_DIM
    return nsa_merge(o_c.reshape(m, w), o_s.reshape(m, w), o_w.reshape(m, w), pc)


def dsa_mixer_core(p1, p2, b, t, tabs, q_norm, k_norm, ki_norm):
    qw = DSA_HEADS * HEAD_DIM
    kvw = DSA_KV_HEADS * HEAD_DIM
    qn = norm_rope(p1, 0, DSA_HEADS, q_norm, tabs, scale=QK_SCALE, name="dsa_q_prep").reshape(b, t, qw)
    kn = norm_rope(p1, qw, DSA_KV_HEADS, k_norm, tabs, name="dsa_k_prep").reshape(b, t, kvw)
    vt = _kv_transposed(p1, qw + kvw, b, t, DSA_KV_HEADS)
    qi = norm_rope(p1, qw + 2 * kvw, IDX_HEADS, None, tabs, name="dsa_qi_prep")
    qi = qi.reshape(b, t, IDX_HEADS * IDX_DIM)
    ki = norm_rope(p2, 0, 1, ki_norm, tabs, name="dsa_ki_prep").reshape(b, t, IDX_DIM)
    wscale = IDX_HEADS ** -0.5 * IDX_DIM ** -0.5
    wt = (p2[:, IDX_DIM:IDX_DIM + IDX_HEADS] * wscale).reshape(b, t, IDX_HEADS).transpose(0, 2, 1)
    key_bias = dsa_select(qi, ki, wt, b, t)
    o = flash_masked(qn, kn, vt, mode="dsa", group=DSA_GROUP, b=b, t=t, tq=LANES,
                     extra=key_bias, out_dtype=BF16, name="dsa_attention")
    return o.reshape(b * t, qw)


def kernel(x, mem, positions, mem_norm, mem_w_kv, mem_k_norm, l0_mix_norm, l0_w_in, l0_w_out, nsa_q_norm, nsa_kc_norm, nsa_ks_norm, nsa_kw_norm, nsa_pe_k, nsa_pe_v, nsa_ck_w1, nsa_ck_w2, nsa_cv_w1, nsa_cv_w2, rwkv_mu, rwkv_w0, rwkv_w2, rwkv_a0, rwkv_a2, rwkv_g2, rwkv_kk, rwkv_ka, rwkv_rk, rwkv_gn_g, rwkv_gn_b, l0_xattn_norm, l0_mem_wq, l0_mem_q_norm, l0_mem_wo, l0_ffn_norm, l0_w1, l0_w3, l0_w2, l1_mix_norm, l1_w_in, l1_w_out, dsa_q_norm, dsa_k_norm, dsa_ki_norm, l1_xattn_norm, l1_mem_wq, l1_mem_q_norm, l1_mem_wo, l1_ffn_norm, l1_w1, l1_w3, l1_w2):
    b, t, d = x.shape
    m = b * t
    bf = lambda z: z.astype(BF16)
    x2 = x.reshape(m, d)

    tabs = rope_tables(positions.reshape(m))
    nc = t // CMP_STRIDE
    cmp_pos = positions[:, CMP_BLOCK - 1::CMP_STRIDE]
    cmp_pos = jnp.concatenate([cmp_pos, cmp_pos[:, -1:]], axis=1)
    cmp_tabs = rope_tables(cmp_pos.reshape(b * nc))

    mtok = mem.shape[1]
    memn = rmsnorm(mem.reshape(b * mtok, d), mem_norm)
    mkv = matmul(memn, mem_w_kv, tm=512, tn=512, name="mem_kv_proj")
    mk = norm_rope(mkv, 0, MEM_HEADS, mem_k_norm, None, name="mem_k_norm").reshape(b, mtok, MEM_WIDTH)
    mv = bf(mkv[:, MEM_WIDTH:]).reshape(b, mtok, MEM_WIDTH)

    def tail(x2, xn, wq, qn, wo, fn, w1, w3, w2):
        x2, h = mem_xattn_block(x2, xn, bf(wq), qn, mk, mv, bf(wo), fn, b, t)
        u = matmul(h, [w1, w3], swiglu=True, tm=1024, tn=256, out_dtype=BF16, name="ffn_up")
        return matmul(u, bf(w2), tm=512, tn=512, res=x2, name="ffn_down")

    nsa_cols = NSA_HEADS * HEAD_DIM + 6 * NSA_KV_HEADS * HEAD_DIM
    gate_cols = 3 * NSA_HEADS
    rw0 = nsa_cols + gate_cols
    rw1 = rw0 + 3 * RWKV_WIDTH
    pad_cols = lambda wm: jnp.pad(wm, ((0, 0), (0, LANES - wm.shape[1])))
    w_c = jnp.concatenate([
        pad_cols(l0_w_in[:, nsa_cols:rw0]),
        pad_cols(l0_w_in[:, rw1:rw1 + LORA_DECAY]),
        pad_cols(l0_w_in[:, rw1 + LORA_DECAY:rw1 + LORA_DECAY + LORA_AAA]),
        l0_w_in[:, rw1 + LORA_DECAY + LORA_AAA:]], axis=1)
    h = rmsnorm(x2, l0_mix_norm)
    pa = matmul(h, l0_w_in, ncols=nsa_cols, tm=1024, tn=512, name="l0_proj_nsa")
    w_rkv = bf(l0_w_in[:, rw0:rw1]).reshape(d, 3, RWKV_WIDTH)
    pb = matmul(h, _to_jh(w_rkv).reshape(d, 3 * RWKV_WIDTH), tm=1024, tn=512, name="l0_proj_rwkv")
    pc = matmul(h, bf(w_c), tm=1024, tn=w_c.shape[1], name="l0_proj_small")
    o_a = nsa_mixer(pa, pc, b, t, tabs, cmp_tabs, nsa_q_norm, nsa_kc_norm, nsa_ks_norm, nsa_kw_norm,
                    nsa_pe_k, nsa_pe_v, nsa_ck_w1, nsa_ck_w2, nsa_cv_w1, nsa_cv_w2)
    o_b = rwkv7(pb, pc, b, t, rwkv_mu, rwkv_w0, rwkv_w2, rwkv_a0, rwkv_a2, rwkv_g2, rwkv_kk, rwkv_ka,
                rwkv_rk, rwkv_gn_g, rwkv_gn_b)
    nsa_w = NSA_HEADS * HEAD_DIM
    w_out_rwkv = bf(l0_w_out[nsa_w:]).reshape(RWKV_HEADS, RWKV_HEAD, d).swapaxes(0, 1).reshape(RWKV_WIDTH, d)
    x2 = matmul([o_a, o_b], [l0_w_out[:nsa_w], w_out_rwkv], tm=1024, tn=512, res=x2,
                name="l0_out_proj")
    x2 = tail(x2, l0_xattn_norm, l0_mem_wq, l0_mem_q_norm, l0_mem_wo, l0_ffn_norm, l0_w1, l0_w3, l0_w2)

    main_cols = DSA_HEADS * HEAD_DIM + 2 * DSA_KV_HEADS * HEAD_DIM + IDX_HEADS * IDX_DIM
    w_s = jnp.concatenate([l1_w_in[:, main_cols:main_cols + IDX_DIM],
                           pad_cols(l1_w_in[:, main_cols + IDX_DIM:])], axis=1)
    h = rmsnorm(x2, l1_mix_norm)
    p1 = matmul(h, l1_w_in, ncols=main_cols, tm=1024, tn=512, name="l1_proj_main")
    p2 = matmul(h, bf(w_s), tm=1024, tn=w_s.shape[1], name="l1_proj_small")
    o = dsa_mixer_core(p1, p2, b, t, tabs, dsa_q_norm, dsa_k_norm, dsa_ki_norm)
    x2 = matmul(o, l1_w_out, tm=1024, tn=512, res=x2, name="l1_out_proj")
    x2 = tail(x2, l1_xattn_norm, l1_mem_wq, l1_mem_q_norm, l1_mem_wo, l1_ffn_norm, l1_w1, l1_w3, l1_w2)
    return x2.reshape(b, t, d)
```

```python
import functools

import numpy as np
import jax
import jax.numpy as jnp
from jax import lax
from jax.experimental import pallas as pl
from jax.experimental.pallas import tpu as pltpu

F32 = jnp.float32
BF16 = jnp.bfloat16

HEAD_DIM = 128
ROPE_DIM = HEAD_DIM // 4
ROPE_HALF = ROPE_DIM // 2
ROPE_THETA = 500000.0
NORM_EPS = 1e-6
NEG_INF = -1e30

NSA_HEADS = 16
NSA_KV_HEADS = 4
NSA_GROUP = NSA_HEADS // NSA_KV_HEADS
CMP_BLOCK = 32
CMP_STRIDE = 16
SEL_BLOCK = 64
SEL_TOPN = 16
WINDOW = 512
FORCED_SCORE = 1e4

RWKV_WIDTH = 2048
RWKV_HEAD = 64
RWKV_HEADS = RWKV_WIDTH // RWKV_HEAD
LORA_DECAY = 96
LORA_AAA = 96
LORA_GATE = 256
GN_EPS = 64e-5

DSA_HEADS = 32
DSA_KV_HEADS = 4
DSA_GROUP = DSA_HEADS // DSA_KV_HEADS
IDX_HEADS = 32
IDX_DIM = 128
DSA_TOPK_MAX = 256

MEM_HEADS = 4
MEM_WIDTH = MEM_HEADS * HEAD_DIM

LANES = 128
VMEM_BIG = 56 * 1024 * 1024
VMEM_MID = 40 * 1024 * 1024
INT_MIN = -2 ** 31
MASK_BIG = 1e30
FLASH_TK = 512
QK_SCALE = HEAD_DIM ** -0.5 * float(np.log2(np.e))


def _params(sem, vmem=VMEM_MID):
    return pltpu.CompilerParams(dimension_semantics=sem, vmem_limit_bytes=vmem)


def _sigmoid(x):
    return 1.0 / (1.0 + jnp.exp(-x))


def _dot(a, b, preferred_element_type=F32, precision=None):
    return lax.dot_general(a, b, (((1,), (0,)), ((), ())), precision=precision,
                           preferred_element_type=preferred_element_type)


def _nt_dot(a, b):
    return lax.dot_general(a, b, (((1,), (1,)), ((), ())), preferred_element_type=F32)


def _split_dot(x, w_bf16):
    hi = x.astype(BF16)
    lo = (x - hi.astype(F32)).astype(BF16)
    return (_dot(hi, w_bf16, preferred_element_type=F32)
            + _dot(lo, w_bf16, preferred_element_type=F32))


def _rmsnorm_kernel(x_ref, g_ref, o_ref):
    x = x_ref[...]
    ms = jnp.mean(x * x, axis=-1, keepdims=True)
    o_ref[...] = (x * lax.rsqrt(ms + NORM_EPS) * g_ref[...]).astype(o_ref.dtype)


def rmsnorm(x2d, g, tm=256):
    m, d = x2d.shape
    return pl.pallas_call(
        _rmsnorm_kernel,
        grid=(m // tm,),
        in_specs=[pl.BlockSpec((tm, d), lambda i: (i, 0)),
                  pl.BlockSpec((1, d), lambda i: (0, 0))],
        out_specs=pl.BlockSpec((tm, d), lambda i: (i, 0)),
        out_shape=jax.ShapeDtypeStruct((m, d), BF16),
        compiler_params=_params(("parallel",)),
        name="rmsnorm",
    )(x2d, g.reshape(1, d))


def _mm_kernel(*refs, n_a, swiglu, has_res):
    a_refs = refs[:n_a]
    n_b = 2 if swiglu else n_a
    b_refs = refs[n_a:n_a + n_b]
    res_ref = refs[n_a + n_b] if has_res else None
    o_ref = refs[-1]
    if swiglu:
        a = a_refs[0][...]
        gate = _dot(a, b_refs[0][...].astype(BF16))
        y = gate * _sigmoid(gate) * _dot(a, b_refs[1][...].astype(BF16))
    else:
        y = _dot(a_refs[0][...], b_refs[0][...].astype(BF16))
        for a_ref, b_ref in zip(a_refs[1:], b_refs[1:]):
            y = y + _dot(a_ref[...], b_ref[...].astype(BF16))
    if has_res:
        y = y + res_ref[...]
    o_ref[...] = y.astype(o_ref.dtype)


def matmul(a, b, *, tm, tn, swiglu=False, col0=0, ncols=None, res=None, out_dtype=F32, name="matmul"):
    a_list = list(a) if isinstance(a, (list, tuple)) else [a]
    b_list = list(b) if isinstance(b, (list, tuple)) else [b]
    m = a_list[0].shape[0]
    n = b_list[0].shape[1] - col0 if ncols is None else ncols
    tm = min(tm, m)
    tn = min(tn, n)
    assert m % tm == 0 and n % tn == 0 and col0 % tn == 0
    off = col0 // tn
    in_specs = [pl.BlockSpec((tm, ai.shape[1]), lambda i, j: (i, 0)) for ai in a_list]
    in_specs += [pl.BlockSpec((bi.shape[0], tn), lambda i, j: (0, off + j)) for bi in b_list]
    args = a_list + b_list
    if res is not None:
        in_specs.append(pl.BlockSpec((tm, tn), lambda i, j: (i, j)))
        args.append(res)
    return pl.pallas_call(
        functools.partial(_mm_kernel, n_a=len(a_list), swiglu=swiglu, has_res=res is not None),
        grid=(m // tm, n // tn),
        in_specs=in_specs,
        out_specs=pl.BlockSpec((tm, tn), lambda i, j: (i, j)),
        out_shape=jax.ShapeDtypeStruct((m, n), out_dtype),
        compiler_params=_params(("parallel", "parallel"), VMEM_BIG),
        name=name,
    )(*args)


def _rope_tab_kernel(pos_ref, inv_ref, c_ref, sa_ref, sb_ref):
    ang = pos_ref[...].astype(F32) * inv_ref[...]
    c = jnp.cos(ang)
    s = jnp.sin(ang)
    lane = lax.broadcasted_iota(jnp.int32, ang.shape, 1)
    c_ref[...] = jnp.where(lane < ROPE_DIM, c, 1.0)
    sa_ref[...] = jnp.where(lane < ROPE_HALF, -s, 0.0)
    sb_ref[...] = jnp.where((lane >= ROPE_HALF) & (lane < ROPE_DIM), s, 0.0)


def rope_tables(pos_flat, tm=256):
    n = pos_flat.shape[0]
    tm = min(tm, n)
    inv = ROPE_THETA ** (-jnp.arange(0, ROPE_DIM, 2, dtype=F32) / ROPE_DIM)
    inv_row = jnp.concatenate([inv, inv, jnp.zeros((LANES - ROPE_DIM,), F32)]).reshape(1, LANES)
    pos_b = jnp.broadcast_to(pos_flat[:, None], (n, LANES))
    spec = pl.BlockSpec((tm, LANES), lambda i: (i, 0))
    shp = jax.ShapeDtypeStruct((n, LANES), F32)
    return pl.pallas_call(
        _rope_tab_kernel,
        grid=(n // tm,),
        in_specs=[spec, pl.BlockSpec((1, LANES), lambda i: (0, 0))],
        out_specs=[spec, spec, spec],
        out_shape=[shp, shp, shp],
        compiler_params=_params(("parallel",)),
        name="rope_tables",
    )(pos_b, inv_row)


def _head_norm(xh, g):
    ms = jnp.mean(xh * xh, axis=-1, keepdims=True)
    return xh * lax.rsqrt(ms + NORM_EPS) * g


def _rope(xh, c, sa, sb):
    return (xh * c + pltpu.roll(xh, LANES - ROPE_HALF, 1) * sa
            + pltpu.roll(xh, ROPE_HALF, 1) * sb)


def _norm_rope_kernel(*refs, heads, do_norm, do_rope, scale):
    x_ref = refs[0]
    pos = 1
    g = None
    if do_norm:
        g = refs[pos][...]
        pos += 1
    if do_rope:
        c, sa, sb = refs[pos][...], refs[pos + 1][...], refs[pos + 2][...]
        pos += 3
    o_ref = refs[pos]
    for h in range(heads):
        xh = x_ref[:, h * HEAD_DIM:(h + 1) * HEAD_DIM]
        if do_norm:
            xh = _head_norm(xh, g)
        if do_rope:
            xh = _rope(xh, c, sa, sb)
        if scale != 1.0:
            xh = xh * scale
        o_ref[:, h * HEAD_DIM:(h + 1) * HEAD_DIM] = xh.astype(o_ref.dtype)


def norm_rope(x2d, col_start, heads, gain, tabs, scale=1.0, tm=256, name="norm_rope"):
    m = x2d.shape[0]
    hb = heads
    while col_start % (hb * HEAD_DIM):
        hb //= 2
    w = hb * HEAD_DIM
    off = col_start // w
    tm = min(tm, m)
    in_specs = [pl.BlockSpec((tm, w), lambda i, j: (i, off + j))]
    args = [x2d]
    if gain is not None:
        in_specs.append(pl.BlockSpec((1, HEAD_DIM), lambda i, j: (0, 0)))
        args.append(gain.reshape(1, HEAD_DIM))
    if tabs is not None:
        in_specs += [pl.BlockSpec((tm, LANES), lambda i, j: (i, 0))] * 3
        args += list(tabs)
    return pl.pallas_call(
        functools.partial(_norm_rope_kernel, heads=hb, do_norm=gain is not None,
                          do_rope=tabs is not None, scale=scale),
        grid=(m // tm, heads // hb),
        in_specs=in_specs,
        out_specs=pl.BlockSpec((tm, w), lambda i, j: (i, j)),
        out_shape=jax.ShapeDtypeStruct((m, heads * HEAD_DIM), BF16),
        compiler_params=_params(("parallel", "parallel")),
        name=name,
    )(*args)


def _compress_kernel(*refs, is_key):
    if is_key:
        x_ref, w1_ref, w2_ref, pe_ref, g_ref, c_ref, sa_ref, sb_ref, o_ref = refs
    else:
        x_ref, w1_ref, w2_ref, pe_ref, o_ref = refs
    half = CMP_STRIDE * HEAD_DIM
    x = x_ref[0, 0]
    nc = x.shape[0]
    top = _dot(x, w1_ref[:half, :], preferred_element_type=F32)
    bot = _dot(x, w1_ref[half:, :], preferred_element_type=F32)
    pe_term = _dot(pe_ref[...], w1_ref[...], preferred_element_type=F32)[0:1, :]
    pre = top + pltpu.roll(bot, nc - 1, 0) + pe_term
    hid = jax.nn.gelu(pre)
    out = _dot(hid.astype(BF16), w2_ref[...], preferred_element_type=F32)
    if is_key:
        out = _rope(_head_norm(out, g_ref[...]), c_ref[0], sa_ref[0], sb_ref[0])
    o_ref[0, 0] = out.astype(o_ref.dtype)


def compress(xblk, w1, w2, pe, key_extras=None, name="compress"):
    b, h, nc, wdt = xblk.shape
    pe8 = jnp.broadcast_to(pe.reshape(1, CMP_BLOCK * HEAD_DIM), (8, CMP_BLOCK * HEAD_DIM)).astype(BF16)
    in_specs = [pl.BlockSpec((1, 1, nc, wdt), lambda i, j: (i, j, 0, 0)),
                pl.BlockSpec((CMP_BLOCK * HEAD_DIM, HEAD_DIM), lambda i, j: (0, 0)),
                pl.BlockSpec((HEAD_DIM, HEAD_DIM), lambda i, j: (0, 0)),
                pl.BlockSpec((8, CMP_BLOCK * HEAD_DIM), lambda i, j: (0, 0))]
    args = [xblk, w1.astype(BF16), w2.astype(BF16), pe8]
    if key_extras is not None:
        gain, tabs = key_extras
        in_specs.append(pl.BlockSpec((1, HEAD_DIM), lambda i, j: (0, 0)))
        in_specs += [pl.BlockSpec((1, nc, LANES), lambda i, j: (i, 0, 0))] * 3
        args += [gain.reshape(1, HEAD_DIM)] + [t.reshape(b, nc, LANES) for t in tabs]
    out_dtype = BF16
    return pl.pallas_call(
        functools.partial(_compress_kernel, is_key=key_extras is not None),
        grid=(b, h),
        in_specs=in_specs,
        out_specs=pl.BlockSpec((1, 1, nc, HEAD_DIM), lambda i, j: (i, j, 0, 0)),
        out_shape=jax.ShapeDtypeStruct((b, h, nc, HEAD_DIM), out_dtype),
        compiler_params=_params(("parallel", "parallel")),
        name=name,
    )(*args)


def _stack_heads(q_ref, group, tq):
    return jnp.concatenate([q_ref[0, :, g * HEAD_DIM:(g + 1) * HEAD_DIM] for g in range(group)], axis=0)


def _unstack_heads(o_ref, o_t, group, tq):
    o = o_t.T
    for g in range(group):
        o_ref[0, :, g * HEAD_DIM:(g + 1) * HEAD_DIM] = o[g * tq:(g + 1) * tq, :].astype(o_ref.dtype)


def _cmp_kernel(q_ref, kc_ref, vct_ref, mt_ref, o_ref, sel_ref, *, tq, n_sel, topn):
    group = NSA_GROUP
    t0 = pl.program_id(2) * tq
    q = _stack_heads(q_ref, group, tq)
    kc = kc_ref[0, 0]
    nc = kc.shape[0]
    s = _nt_dot(kc, q)
    c_idx = lax.broadcasted_iota(jnp.int32, (nc, tq), 0)
    t_idx = t0 + lax.broadcasted_iota(jnp.int32, (nc, tq), 1)
    ok1 = (CMP_STRIDE * c_idx + CMP_BLOCK - 1) <= t_idx
    bias1 = jnp.where(ok1, 0.0, NEG_INF)
    okf1 = jnp.where(ok1, 1.0, 0.0)
    bias = jnp.concatenate([bias1] * group, axis=1)
    okf = jnp.concatenate([okf1] * group, axis=1)
    s = s + bias
    m = jnp.max(s, axis=0, keepdims=True)
    e = jnp.exp2(s - m) * okf
    l = jnp.sum(e, axis=0, keepdims=True)
    p = e * jnp.where(l > 0.0, 1.0 / l, 0.0)
    o_t = _dot(vct_ref[0, 0], p.astype(BF16), preferred_element_type=F32)
    _unstack_heads(o_ref, o_t, group, tq)

    psum = p[:, 0:tq]
    for g in range(1, group):
        psum = psum + p[:, g * tq:(g + 1) * tq]
    imp = _dot(mt_ref[...], psum, preferred_element_type=F32,
                  precision=lax.Precision.HIGHEST)
    j_idx = lax.broadcasted_iota(jnp.int32, (n_sel, tq), 0)
    jt = (t0 + lax.broadcasted_iota(jnp.int32, (n_sel, tq), 1)) // SEL_BLOCK
    forced = (j_idx == 0) | (j_idx == jt) | (j_idx == jt - 1)
    imp = jnp.where(forced, FORCED_SCORE, imp)
    imp = jnp.where(j_idx <= jt, imp, -jnp.inf)
    rank = jnp.zeros((n_sel, tq), F32)
    for jp in range(n_sel):
        row = imp[jp:jp + 1, :]
        before = (row > imp) | ((row == imp) & (j_idx > jp))
        rank = rank + jnp.where(before, 1.0, 0.0)
    bias_t = jnp.where(rank < topn, 0.0, -MASK_BIG)
    bias_t = jnp.concatenate([bias_t, jnp.zeros((LANES - n_sel, tq), F32)], axis=0)
    sel_ref[0, 0] = bias_t.T.astype(sel_ref.dtype)


def nsa_compressed(qn, kc, vct, b, t, tq=256):
    nc = kc.shape[2]
    n_sel = t // SEL_BLOCK
    assert n_sel <= LANES
    topn = min(SEL_TOPN, n_sel)
    tq = min(tq, t)
    cs = CMP_STRIDE * np.arange(nc)[:, None]
    ss = SEL_BLOCK * np.arange(n_sel)[None, :]
    ov = np.clip(np.minimum(cs + CMP_BLOCK, ss + SEL_BLOCK) - np.maximum(cs, ss), 0, None) / CMP_BLOCK
    ov[nc - 1, :] = 0.0
    mt = jnp.asarray(ov.T, dtype=F32)
    gw = NSA_GROUP * HEAD_DIM
    return pl.pallas_call(
        functools.partial(_cmp_kernel, tq=tq, n_sel=n_sel, topn=topn),
        grid=(b, NSA_KV_HEADS, t // tq),
        in_specs=[pl.BlockSpec((1, tq, gw), lambda i, h, j: (i, j, h)),
                  pl.BlockSpec((1, 1, nc, HEAD_DIM), lambda i, h, j: (i, h, 0, 0)),
                  pl.BlockSpec((1, 1, HEAD_DIM, nc), lambda i, h, j: (i, h, 0, 0)),
                  pl.BlockSpec((n_sel, nc), lambda i, h, j: (0, 0))],
        out_specs=[pl.BlockSpec((1, tq, gw), lambda i, h, j: (i, j, h)),
                   pl.BlockSpec((1, 1, tq, LANES), lambda i, h, j: (i, h, j, 0))],
        out_shape=[jax.ShapeDtypeStruct((b, t, NSA_HEADS * HEAD_DIM), F32),
                   jax.ShapeDtypeStruct((b, NSA_KV_HEADS, t, LANES), BF16)],
        compiler_params=_params(("parallel", "parallel", "parallel")),
        name="nsa_compressed",
    )(qn, kc, vct, mt)


def _flash_kernel(*refs, mode, group, slab, tq, tk):
    if mode == "window":
        q_ref, k_ref, vt_ref, o_ref, q_s, m_s, l_s, acc_s, sa_s, sb_s, p_s = refs
        xq = None
    else:
        q_ref, k_ref, vt_ref, xq_ref, xk_ref, o_ref, q_s, m_s, l_s, acc_s, sa_s, sb_s, p_s = refs
        xq = xq_ref[0, 0] if mode == "sel" else xq_ref[...]
    qi = pl.program_id(2)
    t0 = qi * tq
    q = _stack_heads(q_ref, group, tq)
    if xq is not None:
        q = jnp.concatenate([q, jnp.concatenate([xq] * group, axis=0)], axis=1)
    q_s[...] = q
    m_s[...] = jnp.full(m_s.shape, NEG_INF, F32)
    l_s[...] = jnp.zeros(l_s.shape, F32)
    acc_s[...] = jnp.zeros(acc_s.shape, F32)

    def scores(j):
        k0 = pl.multiple_of(j * tk, tk)
        kt = k_ref[0, pl.ds(k0, tk), :]
        if mode == "sel":
            kt = jnp.concatenate([kt, xk_ref[pl.ds(k0, tk), :]], axis=1)
        elif mode == "dsa":
            kt = jnp.concatenate([kt, xk_ref[0, pl.ds(k0, tk), :]], axis=1)
        return _nt_dot(kt, q_s[...])

    def absorb(s_ref, j, keep):
        if keep is not None:
            kpos = j * tk + lax.broadcasted_iota(jnp.int32, (tk, tq), 0)
            tpos = t0 + lax.broadcasted_iota(jnp.int32, (tk, tq), 1)
            ok = keep(kpos, tpos)
            for g in range(group):
                cols = slice(g * tq, (g + 1) * tq)
                s_ref[:, cols] = jnp.where(ok, s_ref[:, cols], NEG_INF)
        m_old = m_s[...]
        m_new = jnp.maximum(m_old, jnp.max(s_ref[...], axis=0, keepdims=True))
        alpha = jnp.exp2(m_old - m_new)
        p = jnp.exp2(s_ref[...] - m_new)
        l_s[...] = alpha * l_s[...] + jnp.sum(p, axis=0, keepdims=True)
        p_s[...] = p.astype(BF16)
        acc_s[...] = acc_s[...] * alpha + _dot(vt_ref[0, 0, j], p_s[...])
        m_s[...] = m_new

    causal = lambda kpos, tpos: kpos <= tpos

    if mode == "window":
        nw = WINDOW // tk
        edge = lambda kpos, tpos: tpos - kpos < WINDOW
        keeps = [edge] + [None] * (nw - 1) + [causal]

        @pl.when(qi >= nw)
        def _():
            bufs = (sa_s, sb_s)
            sa_s[...] = scores(qi - nw)
            for n, keep in enumerate(keeps):
                if n + 1 < len(keeps):
                    bufs[(n + 1) % 2][...] = scores(qi - nw + n + 1)
                absorb(bufs[n % 2], qi - nw + n, keep)

        @pl.when(qi < nw)
        def _():
            def early(j, carry):
                sa_s[...] = scores(j)
                absorb(sa_s, j, None)
                return carry
            lax.fori_loop(0, qi, early, 0)
            sa_s[...] = scores(qi)
            absorb(sa_s, qi, causal)
    else:
        last = (t0 + tq - 1) // tk
        last_keep = causal if mode == "sel" else None
        sa_s[...] = scores(0)

        def pair(i, carry):
            j = 2 * i
            sb_s[...] = scores(j + 1)
            absorb(sa_s, j, None)
            sa_s[...] = scores(j + 2)
            absorb(sb_s, j + 1, None)
            return carry

        lax.fori_loop(0, last // 2, pair, 0)

        @pl.when(last % 2 == 1)
        def _():
            sb_s[...] = scores(last)
            absorb(sa_s, last - 1, None)
            absorb(sb_s, last, last_keep)

        @pl.when(last % 2 == 0)
        def _():
            absorb(sa_s, last, last_keep)

    _unstack_heads(o_ref, acc_s[...] * (1.0 / l_s[...]), group, tq)


def flash_masked(qn, kn, vt5, *, mode, group, b, t, tq, slab=None, extra=None, out_dtype=F32, name="flash"):
    kvh = kn.shape[2] // HEAD_DIM
    gw = group * HEAD_DIM
    tq = min(tq, t)
    ntk, tk = vt5.shape[2], vt5.shape[4]
    slab = group if slab is None else slab
    in_specs = [pl.BlockSpec((1, tq, gw), lambda i, h, qi: (i, qi, h)),
                pl.BlockSpec((1, t, HEAD_DIM), lambda i, h, qi: (i, 0, h)),
                pl.BlockSpec((1, 1, ntk, HEAD_DIM, tk), lambda i, h, qi: (i, h, 0, 0, 0))]
    args = [qn, kn, vt5]
    kdim = HEAD_DIM
    if mode == "sel":
        assert tk % tq == 0
        et = np.zeros((t, LANES), np.float32)
        et[np.arange(t), np.arange(t) // SEL_BLOCK] = 1.0
        in_specs += [pl.BlockSpec((1, 1, tq, LANES), lambda i, h, qi: (i, h, qi, 0)),
                     pl.BlockSpec((t, LANES), lambda i, h, qi: (0, 0))]
        args += [extra, jnp.asarray(et, dtype=BF16)]
        kdim += LANES
    elif mode == "dsa":
        assert tq == LANES
        in_specs += [pl.BlockSpec((tq, LANES), lambda i, h, qi: (0, 0)),
                     pl.BlockSpec((1, t, tq), lambda i, h, qi: (i, 0, qi))]
        args += [jnp.eye(tq, dtype=BF16), extra]
        kdim += LANES
    else:
        assert tk == tq and WINDOW % tk == 0
    return pl.pallas_call(
        functools.partial(_flash_kernel, mode=mode, group=group, slab=slab, tq=tq, tk=tk),
        grid=(b, kvh, t // tq),
        in_specs=in_specs,
        out_specs=pl.BlockSpec((1, tq, gw), lambda i, h, qi: (i, qi, h)),
        out_shape=jax.ShapeDtypeStruct((b, t, kvh * gw), out_dtype),
        scratch_shapes=[pltpu.VMEM((group * tq, kdim), BF16),
                        pltpu.VMEM((1, group * tq), F32),
                        pltpu.VMEM((1, group * tq), F32),
                        pltpu.VMEM((HEAD_DIM, group * tq), F32),
                        pltpu.VMEM((tk, group * tq), F32),
                        pltpu.VMEM((tk, group * tq), F32),
                        pltpu.VMEM((tk, group * tq), BF16)],
        compiler_params=_params(("parallel", "parallel", "parallel"), VMEM_BIG),
        name=name,
    )(*args)


def _nsa_merge_kernel(oc_ref, os_ref, ow_ref, g_ref, e_ref, o_ref):
    gate = _sigmoid(g_ref[...])
    out = None
    for j, src in enumerate((oc_ref, os_ref, ow_ref)):
        gj = _split_dot(gate, e_ref[j])
        term = gj * src[...]
        out = term if out is None else out + term
    o_ref[...] = out.astype(o_ref.dtype)


def nsa_merge(oc, os_, ow, pc, tm=256):
    m, w = oc.shape
    tm = min(tm, m)
    e = np.zeros((3, LANES, w), np.float32)
    for h in range(NSA_HEADS):
        for j in range(3):
            e[j, h * 3 + j, h * HEAD_DIM:(h + 1) * HEAD_DIM] = 1.0
    spec = pl.BlockSpec((tm, w), lambda i: (i, 0))
    return pl.pallas_call(
        _nsa_merge_kernel,
        grid=(m // tm,),
        in_specs=[spec, spec, spec,
                  pl.BlockSpec((tm, LANES), lambda i: (i, 0)),
                  pl.BlockSpec((3, LANES, w), lambda i: (0, 0, 0))],
        out_specs=spec,
        out_shape=jax.ShapeDtypeStruct((m, w), BF16),
        compiler_params=_params(("parallel",)),
        name="nsa_merge",
    )(oc, os_, ow, pc, jnp.asarray(e, dtype=BF16))


SUB = 8
HEADS_PER_SLAB = LANES // RWKV_HEADS
N_SLABS = RWKV_WIDTH // LANES


def _to_jh(p):
    lead = p.shape[:-1]
    return p.reshape(*lead, RWKV_HEADS, RWKV_HEAD).swapaxes(-1, -2).reshape(*lead, RWKV_WIDTH)


def _lane_table(p, nb):
    tab = p.reshape(RWKV_HEADS, RWKV_HEAD).T
    return jnp.tile(tab, (1, nb)).reshape(RWKV_HEAD, 1, LANES)


def _head_sum_jh(x, bd):
    acc = x[:, 0:LANES]
    for s in range(1, N_SLABS):
        acc = acc + x[:, s * LANES:(s + 1) * LANES]
    tot = _split_dot(acc, bd)
    return jnp.concatenate([tot] * N_SLABS, axis=1)


def _interleave(xs, out_ref):
    nb = len(xs)
    tt = xs[0].shape[0]
    group = (lax.broadcasted_iota(jnp.int32, xs[0].shape, 1) % LANES) // RWKV_HEADS
    moved = {}
    for delta in range(1 - nb, nb):
        if delta == 0:
            continue
        src = None
        for g in range(max(0, -delta), min(nb, nb - delta)):
            src = xs[g + delta] if src is None else jnp.where(group == g, xs[g + delta], src)
        moved[delta] = pltpu.roll(src, (delta * RWKV_HEADS) % RWKV_WIDTH, 1)
    for jj in range(HEADS_PER_SLAB):
        z = xs[jj]
        for bi in range(nb):
            if bi != jj:
                z = jnp.where(group == bi, moved[bi - jj], z)
        for tb in range(tt // SUB):
            for s in range(N_SLABS):
                out_ref[tb, HEADS_PER_SLAB * s + jj] = z[tb * SUB:(tb + 1) * SUB, s * LANES:(s + 1) * LANES]


def _deinterleave(val, o_ref):
    nb = o_ref.shape[0]
    tt = o_ref.shape[1]
    zs = []
    for jj in range(HEADS_PER_SLAB):
        z = jnp.concatenate([val[:, HEADS_PER_SLAB * s + jj] for s in range(N_SLABS)], axis=-1)
        zs.append(z.reshape(tt, RWKV_WIDTH))
    lane = lax.broadcasted_iota(jnp.int32, (tt, RWKV_WIDTH), 1) % LANES
    for bi in range(nb):
        out = None
        for jj in range(HEADS_PER_SLAB):
            shift = ((jj - bi) * RWKV_HEADS) % RWKV_WIDTH
            zj = zs[jj] if shift == 0 else pltpu.roll(zs[jj], shift, 1)
            out = zj if out is None else jnp.where(lane // RWKV_HEADS == jj, zj, out)
        o_ref[bi] = out.astype(o_ref.dtype)


def _shifted(x, prev_row, first):
    rolled = pltpu.roll(x, 1, 0)
    row0 = jnp.where(first, 0.0, prev_row)
    ridx = lax.broadcasted_iota(jnp.int32, x.shape, 0)
    return jnp.where(ridx == 0, row0, rolled)


def _rwkv_prep_kernel(pb_ref, pbp_ref, pc_ref, pcp_ref, mub_ref, muc_ref, w0_ref, w2_ref, a0_ref, a2_ref,
                      g2_ref, kkg_ref, ka_ref, bd_ref,
                      r_ref, w_ref, k_ref, v_ref, kk_ref, b_ref, g_ref):
    first = pl.program_id(0) == 0
    w = RWKV_WIDTH
    nb = pb_ref.shape[0]
    outs = [[] for _ in range(7)]
    for bi in range(nb):
        xb = pb_ref[bi]
        xb = xb + (_shifted(xb, pbp_ref[bi, SUB - 1:SUB, :], first) - xb) * mub_ref[...]
        xc = pc_ref[bi]
        xc = xc + (_shifted(xc, pcp_ref[bi, SUB - 1:SUB, :], first) - xc) * muc_ref[...]
        r, k, v = xb[:, :w], xb[:, w:2 * w], xb[:, 2 * w:]
        wd, ad, gd = xc[:, LANES:2 * LANES], xc[:, 2 * LANES:3 * LANES], xc[:, 3 * LANES:]
        z = w0_ref[...] + _dot(jnp.tanh(wd).astype(BF16), w2_ref[...])
        nz = -z
        softplus = jnp.maximum(nz, 0.0) + jnp.log(1.0 + jnp.exp(-jnp.abs(nz)))
        w_log = -softplus - 0.5
        a = _sigmoid(a0_ref[...] + _dot(ad.astype(BF16), a2_ref[...]))
        g = _dot(_sigmoid(gd).astype(BF16), g2_ref[...])
        kkv = k * kkg_ref[...]
        norm = jnp.sqrt(_head_sum_jh(kkv * kkv, bd_ref[...]))
        kkv = kkv / jnp.maximum(norm, 1e-12)
        vals = (r, jnp.exp(-jnp.exp(w_log)), k * (1.0 + (a - 1.0) * ka_ref[...]), v, kkv, kkv * a, g)
        for lst, val in zip(outs, vals):
            lst.append(val)
    for lst, ref in zip(outs, (r_ref, w_ref, k_ref, v_ref, kk_ref, b_ref, g_ref)):
        _interleave(lst, ref)


def _pad_rows(wm, rows):
    return jnp.pad(wm, ((0, rows - wm.shape[0]), (0, 0)))


def _same_head_ones():
    idx = np.arange(LANES) % RWKV_HEADS
    return jnp.asarray(idx[:, None] == idx[None, :], dtype=BF16)


def rwkv_prep(pb, pc, b, t, mu, w0, w2, a0, a2, g2, kk_gain, k_a, tt=32):
    w = RWKV_WIDTH
    assert b * RWKV_HEADS == LANES and t % tt == 0 and tt % SUB == 0
    mu_r, mu_k, mu_v, mu_wd, mu_ad, mu_gd = jnp.split(
        mu, [int(x) for x in np.cumsum([w, w, w, LORA_DECAY, LORA_AAA])])
    mub = jnp.concatenate([_to_jh(mu_r), _to_jh(mu_k), _to_jh(mu_v)]).reshape(1, 3 * w)
    pad = lambda z: jnp.pad(z, (0, LANES - z.shape[0]))
    muc = jnp.concatenate([jnp.zeros((LANES,), F32), pad(mu_wd), pad(mu_ad), mu_gd]).reshape(1, -1)
    cw = pc.shape[1]
    row = lambda z: _to_jh(z).reshape(1, w)
    lora = lambda wm, rows: _to_jh(_pad_rows(wm, rows)).astype(BF16)
    full = lambda shape: pl.BlockSpec(shape, lambda i: (0,) * len(shape))
    tile = lambda width: pl.BlockSpec((b, tt, width), lambda i: (0, i, 0))
    prev = lambda width: pl.BlockSpec((b, SUB, width), lambda i: (0, jnp.maximum(i * (tt // SUB) - 1, 0), 0))
    out_spec = pl.BlockSpec((tt // SUB, RWKV_HEAD, SUB, LANES), lambda i: (i, 0, 0, 0))
    shp = jax.ShapeDtypeStruct((t // SUB, RWKV_HEAD, SUB, LANES), F32)
    pb3 = pb.reshape(b, t, 3 * w)
    pc3 = pc.reshape(b, t, cw)
    return pl.pallas_call(
        _rwkv_prep_kernel,
        grid=(t // tt,),
        in_specs=[tile(3 * w), prev(3 * w), tile(cw), prev(cw), full((1, 3 * w)), full((1, cw)),
                  full((1, w)), full((LANES, w)), full((1, w)), full((LANES, w)), full((LORA_GATE, w)),
                  full((1, w)), full((1, w)), full((LANES, LANES))],
        out_specs=[out_spec] * 7,
        out_shape=[shp] * 7,
        compiler_params=_params(("parallel",), VMEM_BIG),
        name="rwkv_prep",
    )(pb3, pb3, pc3, pc3, mub, muc, row(w0), lora(w2, LANES), row(a0), lora(a2, LANES), lora(g2, LORA_GATE),
      row(kk_gain), row(k_a), _same_head_ones())


def _rwkv_scan_kernel(w_ref, kk_ref, b_ref, k_ref, r_ref, v_ref, y_ref, s_ref, op_s, p_s, *, tb, rows):
    ig = pl.program_id(1)
    blk = SUB * RWKV_HEAD

    @pl.when(pl.program_id(0) == 0)
    def _():
        s_ref[ig] = jnp.zeros(s_ref.shape[1:], F32)

    @pl.when(ig == 0)
    def _():
        def gather(u, p):
            base = pl.multiple_of(u * blk, blk)
            for q in range(SUB):
                key = pl.ds(base + q, RWKV_HEAD, stride=SUB)
                t = u * SUB + q
                op_s[0, t] = -kk_ref[key, :] * p
                p = p * w_ref[key, :]
                inv = 1.0 / p
                op_s[1, t] = b_ref[key, :] * inv
                op_s[2, t] = k_ref[key, :] * inv
                op_s[3, t] = r_ref[key, :] * p
            return p
        p_s[...] = lax.fori_loop(0, tb // SUB, gather, jnp.ones(p_s.shape, F32))

    def steps(u, z):
        base = pl.multiple_of(u * blk, blk) + ig * (rows * SUB)
        for q in range(SUB):
            t = u * SUB + q
            val = pl.ds(base + q, rows, stride=SUB)
            sa = jnp.sum(z * op_s[0, t][None], axis=1)
            z = z + sa[:, None, :] * op_s[1, t][None] + v_ref[val, :][:, None, :] * op_s[2, t][None]
            y_ref[val, :] = jnp.sum(z * op_s[3, t][None], axis=1)
        return z

    s_ref[ig] = lax.fori_loop(0, tb // SUB, steps, s_ref[ig]) * p_s[...][None]


def rwkv_scan(w, kk, bb, k, r, v, tb=64, rows=4):
    n = RWKV_HEAD
    t = w.shape[0] // n
    groups = n // rows
    tb = min(tb, t)
    op = pl.BlockSpec((tb * n, LANES), lambda i, j: (i, 0))
    return pl.pallas_call(
        functools.partial(_rwkv_scan_kernel, tb=tb, rows=rows),
        grid=(t // tb, groups),
        in_specs=[op] * 6,
        out_specs=op,
        out_shape=jax.ShapeDtypeStruct((t * n, LANES), F32),
        scratch_shapes=[pltpu.VMEM((groups, rows, n, LANES), F32),
                        pltpu.VMEM((4, tb, n, LANES), F32),
                        pltpu.VMEM((n, LANES), F32)],
        compiler_params=_params(("arbitrary", "arbitrary"), VMEM_BIG),
        name="rwkv_scan",
    )(w, kk, bb, k, r, v)


def _rwkv_post_kernel(y_ref, r_ref, k_ref, v_ref, g_ref, rk_ref, gg_ref, gb_ref, o_ref):
    y = y_ref[...]
    inv_n = 1.0 / RWKV_HEAD
    mean = jnp.sum(y, axis=1, keepdims=True) * inv_n
    d = y - mean
    var = jnp.sum(d * d, axis=1, keepdims=True) * inv_n
    yn = d * lax.rsqrt(var + GN_EPS) * gg_ref[...][None] + gb_ref[...][None]
    bonus = jnp.sum(r_ref[...] * k_ref[...] * rk_ref[...][None], axis=1, keepdims=True) * v_ref[...]
    _deinterleave((yn + bonus) * g_ref[...], o_ref)


def rwkv_post(y, r, k, v, g, b, t, r_k, gn_g, gn_b, tt=32):
    tile = pl.BlockSpec((tt // SUB, RWKV_HEAD, SUB, LANES), lambda i: (i, 0, 0, 0))
    tab = pl.BlockSpec((RWKV_HEAD, 1, LANES), lambda i: (0, 0, 0))
    return pl.pallas_call(
        _rwkv_post_kernel,
        grid=(t // tt,),
        in_specs=[tile] * 5 + [tab] * 3,
        out_specs=pl.BlockSpec((b, tt, RWKV_WIDTH), lambda i: (0, i, 0)),
        out_shape=jax.ShapeDtypeStruct((b, t, RWKV_WIDTH), BF16),
        compiler_params=_params(("parallel",)),
        name="rwkv_post",
    )(y, r, k, v, g, _lane_table(r_k.reshape(-1), b), _lane_table(gn_g, b), _lane_table(gn_b, b))


def rwkv7(pb, pc, b, t, mu, w0, w2, a0, a2, g2, kk_gain, k_a, r_k, gn_g, gn_b):
    r, w, k, v, kk, bb, g = rwkv_prep(pb, pc, b, t, mu, w0, w2, a0, a2, g2, kk_gain, k_a)
    rows2d = lambda z: z.reshape(t * RWKV_HEAD, LANES)
    y = rwkv_scan(rows2d(w), rows2d(kk), rows2d(bb), rows2d(k), rows2d(r), rows2d(v))
    y = y.reshape(t // SUB, RWKV_HEAD, SUB, LANES)
    return rwkv_post(y, r, k, v, g, b, t, r_k, gn_g, gn_b).reshape(b * t, RWKV_WIDTH)


def _dsa_select_kernel(qi_ref, ki_ref, wt_ref, bias_ref, q_s, key_s, *, t, tq, kc, topk):
    t0 = pl.program_id(1) * tq
    nchunk = (t0 + tq - 1) // kc + 1
    for h in range(IDX_HEADS):
        q_s[h * tq:(h + 1) * tq, :] = qi_ref[0, :, h * IDX_DIM:(h + 1) * IDX_DIM]
    wt = wt_ref[0]

    def chunk(c, carry):
        k0 = pl.multiple_of(c * kc, kc)
        lg = _nt_dot(ki_ref[0, pl.ds(k0, kc), :], q_s[...])
        acc = jnp.zeros((kc, tq), F32)
        for h in range(IDX_HEADS):
            acc = acc + jnp.maximum(lg[:, h * tq:(h + 1) * tq], 0.0) * wt[h:h + 1, :]
        spos = k0 + lax.broadcasted_iota(jnp.int32, (kc, tq), 0)
        tpos = t0 + lax.broadcasted_iota(jnp.int32, (kc, tq), 1)
        score = jnp.where(spos <= tpos, acc + 0.0, -jnp.inf)
        bits = pltpu.bitcast(score, jnp.int32)
        key_s[pl.ds(k0, kc), :] = jnp.where(bits < 0, bits ^ jnp.int32(0x7FFFFFFF), bits)
        return carry

    lax.fori_loop(0, nchunk, chunk, 0)
    kf = jnp.float32(topk)
    part = 64

    def count(pred):
        def body(c, acc):
            k0 = pl.multiple_of(c * kc, kc)
            ind = jnp.where(pred(key_s[pl.ds(k0, kc), :], k0), 1.0, 0.0)
            return acc + jnp.sum(ind.reshape(kc // part, part, tq), axis=0)
        acc = lax.fori_loop(0, nchunk, body, jnp.zeros((part, tq), F32))
        return jnp.sum(acc, axis=0, keepdims=True)

    def value_bit(it, tau_u):
        cand_u = tau_u | lax.shift_left(jnp.int32(1), 31 - it)
        cand_s = cand_u ^ jnp.int32(INT_MIN)
        return jnp.where(count(lambda blk, k0: blk >= cand_s) >= kf, cand_u, tau_u)

    tau_u = lax.fori_loop(0, 32, value_bit, jnp.zeros((1, tq), jnp.int32))
    tau = tau_u ^ jnp.int32(INT_MIN)
    need = kf - count(lambda blk, k0: blk > tau)
    n_eq = count(lambda blk, k0: blk == tau)
    nbits = int(np.log2(t))

    def tie_search():
        def index_bit(it, jj):
            cand = jj | lax.shift_left(jnp.int32(1), nbits - 1 - it)

            def pred(blk, k0):
                spos = k0 + lax.broadcasted_iota(jnp.int32, (kc, tq), 0)
                return (blk == tau) & (spos < cand)
            return jnp.where(count(pred) < need, cand, jj)
        return lax.fori_loop(0, nbits, index_bit, jnp.zeros((1, tq), jnp.int32))

    jj = lax.cond(jnp.max(n_eq - need) > 0.0, tie_search, lambda: jnp.full((1, tq), t, jnp.int32))

    def emit(c, carry):
        k0 = pl.multiple_of(c * kc, kc)
        blk = key_s[pl.ds(k0, kc), :]
        spos = k0 + lax.broadcasted_iota(jnp.int32, (kc, tq), 0)
        tpos = t0 + lax.broadcasted_iota(jnp.int32, (kc, tq), 1)
        keep = ((blk > tau) | ((blk == tau) & (spos <= jj))) & (spos <= tpos)
        bias_ref[0, pl.ds(k0, kc), :] = jnp.where(keep, 0.0, -MASK_BIG).astype(bias_ref.dtype)
        return carry

    lax.fori_loop(0, nchunk, emit, 0)

    def fill(c, carry):
        k0 = pl.multiple_of(c * kc, kc)
        bias_ref[0, pl.ds(k0, kc), :] = jnp.full((kc, tq), -MASK_BIG, bias_ref.dtype)
        return carry

    lax.fori_loop(nchunk, t // kc, fill, 0)


def dsa_select(qi, ki, wt, b, t, tq=128, kc=512):
    topk = min(DSA_TOPK_MAX, t // 4)
    kc = min(kc, t)
    assert topk <= kc and t % kc == 0
    return pl.pallas_call(
        functools.partial(_dsa_select_kernel, t=t, tq=tq, kc=kc, topk=topk),
        grid=(b, t // tq),
        in_specs=[pl.BlockSpec((1, tq, IDX_HEADS * IDX_DIM), lambda i, j: (i, j, 0)),
                  pl.BlockSpec((1, t, IDX_DIM), lambda i, j: (i, 0, 0)),
                  pl.BlockSpec((1, IDX_HEADS, tq), lambda i, j: (i, 0, j))],
        out_specs=pl.BlockSpec((1, t, tq), lambda i, j: (i, 0, j)),
        out_shape=jax.ShapeDtypeStruct((b, t, t), BF16),
        scratch_shapes=[pltpu.VMEM((IDX_HEADS * tq, IDX_DIM), BF16),
                        pltpu.VMEM((t, tq), jnp.int32)],
        compiler_params=_params(("parallel", "parallel"), VMEM_BIG),
        name="dsa_select",
    )(qi, ki, wt)


def _row_norm(x, g):
    ms = jnp.mean(x * x, axis=-1, keepdims=True)
    return x * lax.rsqrt(ms + NORM_EPS) * g


def _xattn_block_kernel(x_ref, xn_ref, wq_ref, qn_ref, mk_ref, mv_ref, wo_ref, fn_ref, xo_ref, hn_ref):
    x = x_ref[...]
    q = _dot(_row_norm(x, xn_ref[...]).astype(BF16), wq_ref[...])
    scale = HEAD_DIM ** -0.5
    heads = []
    for h in range(MEM_HEADS):
        sl = slice(h * HEAD_DIM, (h + 1) * HEAD_DIM)
        qh = (_head_norm(q[:, sl], qn_ref[...]) * scale).astype(BF16)
        s = _nt_dot(qh, mk_ref[0, :, sl])
        m = jnp.max(s, axis=-1, keepdims=True)
        e = jnp.exp(s - m)
        p = e / jnp.sum(e, axis=-1, keepdims=True)
        heads.append(_dot(p.astype(BF16), mv_ref[0, :, sl]).astype(BF16))
    xo = x + _dot(jnp.concatenate(heads, axis=1), wo_ref[...])
    xo_ref[...] = xo
    hn_ref[...] = _row_norm(xo, fn_ref[...]).astype(hn_ref.dtype)


def mem_xattn_block(x2, xn, wq, qn, mk, mv, wo, fn, b, t, tm=256):
    m, d = x2.shape
    tm = min(tm, t)
    nt = t // tm
    mtok = mk.shape[1]
    rows = pl.BlockSpec((tm, d), lambda i, j: (i * nt + j, 0))
    full = lambda shape: pl.BlockSpec(shape, lambda i, j: (0,) * len(shape))
    mem = pl.BlockSpec((1, mtok, MEM_WIDTH), lambda i, j: (i, 0, 0))
    return pl.pallas_call(
        _xattn_block_kernel,
        grid=(b, nt),
        in_specs=[rows, full((1, d)), full((d, MEM_WIDTH)), full((1, HEAD_DIM)), mem, mem,
                  full((MEM_WIDTH, d)), full((1, d))],
        out_specs=[rows, rows],
        out_shape=[jax.ShapeDtypeStruct((m, d), F32), jax.ShapeDtypeStruct((m, d), BF16)],
        compiler_params=_params(("parallel", "parallel"), VMEM_BIG),
        name="mem_xattn_block",
    )(x2, xn.reshape(1, d), wq, qn.reshape(1, HEAD_DIM), mk, mv, wo, fn.reshape(1, d))


def _kv_transposed(x2d, lo, b, t, heads, tk=FLASH_TK):
    tk = min(tk, t)
    v = x2d[:, lo:lo + heads * HEAD_DIM].astype(BF16)
    return v.reshape(b, t // tk, tk, heads, HEAD_DIM).transpose(0, 3, 1, 4, 2)


def nsa_mixer(pa, pc, b, t, tabs, cmp_tabs, q_norm, kc_norm, ks_norm, kw_norm, pe_k, pe_v,
              ck_w1, ck_w2, cv_w1, cv_w2):
    kvw = NSA_KV_HEADS * HEAD_DIM
    qn = norm_rope(pa, 0, NSA_HEADS, q_norm, tabs, scale=QK_SCALE, name="nsa_q_prep")
    ksn = norm_rope(pa, 6 * kvw, NSA_KV_HEADS, ks_norm, tabs, name="nsa_ks_prep").reshape(b, t, kvw)
    kwn = norm_rope(pa, 8 * kvw, NSA_KV_HEADS, kw_norm, tabs, name="nsa_kw_prep").reshape(b, t, kvw)
    vst = _kv_transposed(pa, 7 * kvw, b, t, NSA_KV_HEADS)
    vwt = _kv_transposed(pa, 9 * kvw, b, t, NSA_KV_HEADS)
    qn = qn.reshape(b, t, NSA_HEADS * HEAD_DIM)

    nc = t // CMP_STRIDE

    def chunked(lo):
        z = pa[:, lo:lo + kvw].astype(BF16).reshape(b, nc, CMP_STRIDE, NSA_KV_HEADS, HEAD_DIM)
        return z.transpose(0, 3, 1, 2, 4).reshape(b, NSA_KV_HEADS, nc, CMP_STRIDE * HEAD_DIM)

    kc = compress(chunked(4 * kvw), ck_w1, ck_w2, pe_k, key_extras=(kc_norm, cmp_tabs), name="compress_k")
    vc = compress(chunked(5 * kvw), cv_w1, cv_w2, pe_v, name="compress_v")
    vct = vc.transpose(0, 1, 3, 2)
    o_c, sel_bias = nsa_compressed(qn, kc, vct, b, t)
    o_s = flash_masked(qn, ksn, vst, mode="sel", group=NSA_GROUP, b=b, t=t, tq=512,
                       extra=sel_bias, name="nsa_selected")
    o_w = flash_masked(qn, kwn, vwt, mode="window", group=NSA_GROUP, b=b, t=t, tq=FLASH_TK,
                       name="nsa_window")
    m = b * t
    w = NSA_HEADS * HEAD_DIM
    return nsa_merge(o_c.reshape(m, w), o_s.reshape(m, w), o_w.reshape(m, w), pc)


def dsa_mixer_core(p1, p2, b, t, tabs, q_norm, k_norm, ki_norm):
    qw = DSA_HEADS * HEAD_DIM
    kvw = DSA_KV_HEADS * HEAD_DIM
    qn = norm_rope(p1, 0, DSA_HEADS, q_norm, tabs, scale=QK_SCALE, name="dsa_q_prep").reshape(b, t, qw)
    kn = norm_rope(p1, qw, DSA_KV_HEADS, k_norm, tabs, name="dsa_k_prep").reshape(b, t, kvw)
    vt = _kv_transposed(p1, qw + kvw, b, t, DSA_KV_HEADS)
    qi = norm_rope(p1, qw + 2 * kvw, IDX_HEADS, None, tabs, name="dsa_qi_prep")
    qi = qi.reshape(b, t, IDX_HEADS * IDX_DIM)
    ki = norm_rope(p2, 0, 1, ki_norm, tabs, name="dsa_ki_prep").reshape(b, t, IDX_DIM)
    wscale = IDX_HEADS ** -0.5 * IDX_DIM ** -0.5
    wt = (p2[:, IDX_DIM:IDX_DIM + IDX_HEADS] * wscale).reshape(b, t, IDX_HEADS).transpose(0, 2, 1)
    key_bias = dsa_select(qi, ki, wt, b, t)
    o = flash_masked(qn, kn, vt, mode="dsa", group=DSA_GROUP, b=b, t=t, tq=LANES,
                     extra=key_bias, out_dtype=BF16, name="dsa_attention")
    return o.reshape(b * t, qw)


def kernel(x, mem, positions, mem_norm, mem_w_kv, mem_k_norm, l0_mix_norm, l0_w_in, l0_w_out, nsa_q_norm, nsa_kc_norm, nsa_ks_norm, nsa_kw_norm, nsa_pe_k, nsa_pe_v, nsa_ck_w1, nsa_ck_w2, nsa_cv_w1, nsa_cv_w2, rwkv_mu, rwkv_w0, rwkv_w2, rwkv_a0, rwkv_a2, rwkv_g2, rwkv_kk, rwkv_ka, rwkv_rk, rwkv_gn_g, rwkv_gn_b, l0_xattn_norm, l0_mem_wq, l0_mem_q_norm, l0_mem_wo, l0_ffn_norm, l0_w1, l0_w3, l0_w2, l1_mix_norm, l1_w_in, l1_w_out, dsa_q_norm, dsa_k_norm, dsa_ki_norm, l1_xattn_norm, l1_mem_wq, l1_mem_q_norm, l1_mem_wo, l1_ffn_norm, l1_w1, l1_w3, l1_w2):
    b, t, d = x.shape
    m = b * t
    bf = lambda z: z.astype(BF16)
    x2 = x.reshape(m, d)

    tabs = rope_tables(positions.reshape(m))
    nc = t // CMP_STRIDE
    cmp_pos = positions[:, CMP_BLOCK - 1::CMP_STRIDE]
    cmp_pos = jnp.concatenate([cmp_pos, cmp_pos[:, -1:]], axis=1)
    cmp_tabs = rope_tables(cmp_pos.reshape(b * nc))

    mtok = mem.shape[1]
    memn = rmsnorm(mem.reshape(b * mtok, d), mem_norm)
    mkv = matmul(memn, mem_w_kv, tm=512, tn=512, name="mem_kv_proj")
    mk = norm_rope(mkv, 0, MEM_HEADS, mem_k_norm, None, name="mem_k_norm").reshape(b, mtok, MEM_WIDTH)
    mv = bf(mkv[:, MEM_WIDTH:]).reshape(b, mtok, MEM_WIDTH)

    def tail(x2, xn, wq, qn, wo, fn, w1, w3, w2):
        x2, h = mem_xattn_block(x2, xn, bf(wq), qn, mk, mv, bf(wo), fn, b, t)
        u = matmul(h, [w1, w3], swiglu=True, tm=1024, tn=256, out_dtype=BF16, name="ffn_up")
        return matmul(u, bf(w2), tm=512, tn=512, res=x2, name="ffn_down")

    nsa_cols = NSA_HEADS * HEAD_DIM + 6 * NSA_KV_HEADS * HEAD_DIM
    gate_cols = 3 * NSA_HEADS
    rw0 = nsa_cols + gate_cols
    rw1 = rw0 + 3 * RWKV_WIDTH
    pad_cols = lambda wm: jnp.pad(wm, ((0, 0), (0, LANES - wm.shape[1])))
    w_c = jnp.concatenate([
        pad_cols(l0_w_in[:, nsa_cols:rw0]),
        pad_cols(l0_w_in[:, rw1:rw1 + LORA_DECAY]),
        pad_cols(l0_w_in[:, rw1 + LORA_DECAY:rw1 + LORA_DECAY + LORA_AAA]),
        l0_w_in[:, rw1 + LORA_DECAY + LORA_AAA:]], axis=1)
    h = rmsnorm(x2, l0_mix_norm)
    pa = matmul(h, l0_w_in, ncols=nsa_cols, tm=1024, tn=512, name="l0_proj_nsa")
    w_rkv = bf(l0_w_in[:, rw0:rw1]).reshape(d, 3, RWKV_WIDTH)
    pb = matmul(h, _to_jh(w_rkv).reshape(d, 3 * RWKV_WIDTH), tm=1024, tn=512, name="l0_proj_rwkv")
    pc = matmul(h, bf(w_c), tm=1024, tn=w_c.shape[1], name="l0_proj_small")
    o_a = nsa_mixer(pa, pc, b, t, tabs, cmp_tabs, nsa_q_norm, nsa_kc_norm, nsa_ks_norm, nsa_kw_norm,
                    nsa_pe_k, nsa_pe_v, nsa_ck_w1, nsa_ck_w2, nsa_cv_w1, nsa_cv_w2)
    o_b = rwkv7(pb, pc, b, t, rwkv_mu, rwkv_w0, rwkv_w2, rwkv_a0, rwkv_a2, rwkv_g2, rwkv_kk, rwkv_ka,
                rwkv_rk, rwkv_gn_g, rwkv_gn_b)
    nsa_w = NSA_HEADS * HEAD_DIM
    w_out_rwkv = bf(l0_w_out[nsa_w:]).reshape(RWKV_HEADS, RWKV_HEAD, d).swapaxes(0, 1).reshape(RWKV_WIDTH, d)
    x2 = matmul([o_a, o_b], [l0_w_out[:nsa_w], w_out_rwkv], tm=1024, tn=512, res=x2,
                name="l0_out_proj")
    x2 = tail(x2, l0_xattn_norm, l0_mem_wq, l0_mem_q_norm, l0_mem_wo, l0_ffn_norm, l0_w1, l0_w3, l0_w2)

    main_cols = DSA_HEADS * HEAD_DIM + 2 * DSA_KV_HEADS * HEAD_DIM + IDX_HEADS * IDX_DIM
    w_s = jnp.concatenate([l1_w_in[:, main_cols:main_cols + IDX_DIM],
                           pad_cols(l1_w_in[:, main_cols + IDX_DIM:])], axis=1)
    h = rmsnorm(x2, l1_mix_norm)
    p1 = matmul(h, l1_w_in, ncols=main_cols, tm=1024, tn=512, name="l1_proj_main")
    p2 = matmul(h, bf(w_s), tm=1024, tn=w_s.shape[1], name="l1_proj_small")
    o = dsa_mixer_core(p1, p2, b, t, tabs, dsa_q_norm, dsa_k_norm, dsa_ki_norm)
    x2 = matmul(o, l1_w_out, tm=1024, tn=512, res=x2, name="l1_out_proj")
    x2 = tail(x2, l1_xattn_norm, l1_mem_wq, l1_mem_q_norm, l1_mem_wo, l1_ffn_norm, l1_w1, l1_w3, l1_w2)
    return x2.reshape(b, t, d)
```
